```python
import math
import jax, jax.numpy as jnp
from jax import lax
import numpy as np

D_MODEL = 1024
BATCH = 4
SEQ = 4096
DEPTH = 4
DEC_BATCH = 128
DEC_SEQ = 1
PAST_LEN = 8192
PAGE_SIZE = 128

N_MIXERS = 3
N_A_LAYERS = (DEPTH + 2) // 3
N_B_LAYERS = (DEPTH + 1) // 3
N_C_LAYERS = DEPTH // 3

A_HEADS = 8
A_NOPE = 64
A_ROPE = 32
A_QK = A_NOPE + A_ROPE
A_V = 64
A_Q_LORA = 384
A_KV_LORA = 256
A_WIDTH = A_HEADS * A_V
A_IN = A_Q_LORA + A_KV_LORA + A_ROPE + A_WIDTH
ROPE_THETA = 10000.0
Q_BLOCK = 128

B_EXPAND = 128
B_HEADS = D_MODEL // B_EXPAND
B_DK = B_EXPAND
B_WIDTH = D_MODEL
B_DV = B_WIDTH // B_HEADS
B_CHUNK = 64

C_WIDTH = D_MODEL
C_HEADS = 8
C_BW = C_WIDTH // C_HEADS
C_CONV = 4
C_GATE_C = 8.0

EPS = 1e-6

kernel_name = 'mla_hgrn2_rglru_hybrid_step'


def rmsnorm(x, g):
    xf = x.astype(jnp.float32)
    y = xf * lax.rsqrt(jnp.mean(xf * xf, axis=-1, keepdims=True) + EPS)
    return (y * g.astype(jnp.float32)).astype(x.dtype)


def rope_cos_sin(pos, dtype):
    inv = 1.0 / (ROPE_THETA ** (jnp.arange(0, A_ROPE, 2, dtype=jnp.float32) / A_ROPE))
    ang = pos.astype(jnp.float32)[:, None] * inv[None, :]
    ang = jnp.concatenate([ang, ang], axis=-1)
    return jnp.cos(ang).astype(dtype), jnp.sin(ang).astype(dtype)


def apply_rope(x, cos, sin):
    x1, x2 = jnp.split(x, 2, axis=-1)
    return x * cos + jnp.concatenate([-x2, x1], axis=-1) * sin


def mla_project(h, pos, w_in, g_ql, g_kv, w_uq, g_q):
    b, s, _ = h.shape
    cq, ckv, kpe, gate = jnp.split(h @ w_in, [A_Q_LORA, A_Q_LORA + A_KV_LORA, A_Q_LORA + A_KV_LORA + A_ROPE], axis=-1)
    cos, sin = rope_cos_sin(pos, h.dtype)
    q = (rmsnorm(cq, g_ql) @ w_uq).reshape(b, s, A_HEADS, A_QK)
    q = jnp.concatenate([q[..., :A_NOPE], apply_rope(q[..., A_NOPE:], cos[:, None], sin[:, None])], axis=-1)
    q = rmsnorm(q, g_q)
    ckv = rmsnorm(ckv, g_kv)
    kpe = apply_rope(kpe, cos, sin)
    return q, ckv, kpe, gate


def mla_expand(ckv, kpe, w_ukv, g_k):
    kv = jnp.einsum('bnc,chd->bnhd', ckv, w_ukv.reshape(A_KV_LORA, A_HEADS, A_NOPE + A_V))
    k_nope, v = kv[..., :A_NOPE], kv[..., A_NOPE:]
    k_pe = jnp.broadcast_to(kpe[:, :, None, :], k_nope.shape[:-1] + (A_ROPE,))
    k = jnp.concatenate([k_nope, k_pe.astype(k_nope.dtype)], axis=-1)
    return rmsnorm(k, g_k), v


def mla_prompt_attention(q, k, v):
    b, s = q.shape[:2]
    nqb = s // Q_BLOCK
    scale = A_QK ** -0.5
    qb = q.reshape(b, nqb, Q_BLOCK, A_HEADS, A_QK).swapaxes(0, 1)
    kpos = jnp.arange(s)

    def one_block(args):
        bi, qblk = args
        sc = jnp.einsum('bqhd,bkhd->bhqk', qblk, k).astype(jnp.float32) * scale
        qpos = bi * Q_BLOCK + jnp.arange(Q_BLOCK)
        sc = jnp.where(kpos[None, :] <= qpos[:, None], sc, -jnp.inf)
        p = jax.nn.softmax(sc, axis=-1).astype(v.dtype)
        return jnp.einsum('bhqk,bkhd->bqhd', p, v)

    o = lax.map(one_block, (jnp.arange(nqb), qb))
    return o.swapaxes(0, 1).reshape(b, s, A_HEADS, A_V)


def mla_sample_attention(q, k_new, v_new, cache_ckv, cache_kpe, la, page_table, w_ukv, g_k):
    db, ds = q.shape[:2]
    scale = A_QK ** -0.5

    def update(carry, k, v, mask):
        m, l, acc = carry
        sc = jnp.einsum('bqhd,bkhd->bhqk', q, k.astype(q.dtype)).astype(jnp.float32) * scale
        if mask is not None:
            sc = jnp.where(mask, sc, -jnp.inf)
        m_new = jnp.maximum(m, sc.max(axis=-1))
        p = jnp.exp(sc - m_new[..., None])
        corr = jnp.exp(m - m_new)
        l = l * corr + p.sum(axis=-1)
        acc = acc * corr[..., None] + jnp.einsum('bhqk,bkhd->bhqd', p, v.astype(jnp.float32))
        return (m_new, l, acc)

    def page_step(carry, phys):
        k, v = mla_expand(cache_ckv[la, phys], cache_kpe[la, phys], w_ukv, g_k)
        return update(carry, k, v, None), None

    init = (jnp.full((db, A_HEADS, ds), -jnp.inf, jnp.float32),
            jnp.zeros((db, A_HEADS, ds), jnp.float32),
            jnp.zeros((db, A_HEADS, ds, A_V), jnp.float32))
    carry, _ = lax.scan(page_step, init, page_table.T)
    causal = jnp.arange(ds)[None, :] <= jnp.arange(ds)[:, None]
    m, l, acc = update(carry, k_new, v_new, causal)
    return (acc / l[..., None]).astype(q.dtype).transpose(0, 2, 1, 3)


def mla_out(o, gate, w_out):
    b, s = o.shape[:2]
    return (o.reshape(b, s, A_WIDTH) * jax.nn.silu(gate)) @ w_out


def hgrn_project(h, w_in, lb):
    b, s, _ = h.shape
    q, f, i, gate = jnp.split(h @ w_in, 4, axis=-1)
    ff = f.astype(jnp.float32)
    lbf = lb.astype(jnp.float32)
    g = jnp.log(lbf + (1.0 - lbf) * jax.nn.sigmoid(ff))
    k = (1.0 - lbf) * jax.nn.sigmoid(-ff)

    def heads(t, d):
        return t.reshape(b, s, B_HEADS, d).transpose(0, 2, 1, 3)

    qf = jax.nn.silu(q).astype(jnp.float32) * (B_DK ** -0.5)
    return heads(qf, B_DK), heads(k, B_DK), heads(i.astype(jnp.float32), B_DV), heads(g, B_DK), gate


def gla_chunk(state, inp):
    q, k, v, g = inp
    c = q.shape[2]
    bcum = jnp.cumsum(g, axis=2)
    inter = jnp.einsum('bhtk,bhkv->bhtv', q * jnp.exp(bcum), state)
    causal = jnp.arange(c)[:, None] >= jnp.arange(c)[None, :]
    diff = jnp.where(causal[None, None, :, :, None], bcum[:, :, :, None, :] - bcum[:, :, None, :, :], -jnp.inf)
    att = jnp.einsum('bhtk,bhtsk,bhsk->bhts', q, jnp.exp(diff), k)
    o = inter + jnp.einsum('bhts,bhsv->bhtv', att, v)
    blast = bcum[:, :, -1:, :]
    new_state = jnp.exp(blast[:, :, 0, :])[..., None] * state + jnp.einsum('bhsk,bhsv->bhkv', k * jnp.exp(blast - bcum), v)
    return new_state, o


def to_chunks(t):
    b, hh, s, d = t.shape
    return t.reshape(b, hh, s // B_CHUNK, B_CHUNK, d).transpose(2, 0, 1, 3, 4)


def from_chunks(t):
    nc, b, hh, c, d = t.shape
    return t.transpose(1, 2, 0, 3, 4).reshape(b, hh, nc * c, d)


def hgrn_out(o, gate, g_norm, w_out):
    b, hh, s, dv = o.shape
    o = rmsnorm(o.transpose(0, 2, 1, 3), g_norm).reshape(b, s, B_WIDTH).astype(gate.dtype)
    return (o * jax.nn.silu(gate)) @ w_out


def lru_combine(lhs, rhs):
    a1, b1 = lhs
    a2, b2 = rhs
    return a1 * a2, a2 * b1 + b2


def rglru_layer(h, conv_buf, h0, w_in, conv_w, conv_b, w_a, b_a, w_x, b_x, lam, w_out):
    b, s, _ = h.shape
    xb, gate = jnp.split(h @ w_in, 2, axis=-1)
    xpad = jnp.concatenate([conv_buf.astype(xb.dtype), xb], axis=1)
    y = conv_b + sum(xpad[:, j:j + s] * conv_w[j] for j in range(C_CONV))
    new_buf = xpad[:, s:]
    yh = y.reshape(b, s, C_HEADS, C_BW)
    r = jax.nn.sigmoid(jnp.einsum('bsnd,nde->bsne', yh, w_a).reshape(b, s, C_WIDTH) + b_a)
    i = jax.nn.sigmoid(jnp.einsum('bsnd,nde->bsne', yh, w_x).reshape(b, s, C_WIDTH) + b_x)
    log_a = -C_GATE_C * r.astype(jnp.float32) * jax.nn.softplus(-lam.astype(jnp.float32))
    a = jnp.exp(log_a)
    u = jnp.sqrt(-jnp.expm1(2.0 * log_a)) * (i * y).astype(jnp.float32)
    u = u.at[:, 0].add(a[:, 0] * h0.astype(jnp.float32))
    _, hs = lax.associative_scan(lru_combine, (a, u), axis=1)
    out = (hs.astype(h.dtype) * jax.nn.silu(gate)) @ w_out
    return out, new_buf, hs[:, -1]


def setup_inputs(seed: int = 0) -> dict:
    key = jax.random.key(seed)
    ks = iter(jax.random.split(key, 40))

    def nrm(shape, scale):
        return jax.random.normal(next(ks), shape, jnp.float32) * scale

    def gain(shape):
        return 1.0 + nrm(shape, 0.1)

    n_pages = PAST_LEN // PAGE_SIZE
    n_used = DEC_BATCH * n_pages
    n_pool = n_used + n_used // 4
    x_prompt = nrm((BATCH, SEQ, D_MODEL), 1.0)
    x_sample = nrm((DEC_BATCH, DEC_SEQ, D_MODEL), 1.0)
    cache_ckv = nrm((N_A_LAYERS, n_pool, PAGE_SIZE, A_KV_LORA), 1.0)
    cache_kpe = nrm((N_A_LAYERS, n_pool, PAGE_SIZE, A_ROPE), 1.0)
    page_table = jax.random.permutation(next(ks), n_pool)[:n_used].reshape(DEC_BATCH, n_pages).astype(jnp.int32)
    state_hgrn = nrm((N_B_LAYERS, DEC_BATCH, B_HEADS, B_DK, B_DV), 0.5)
    state_conv = nrm((N_C_LAYERS, DEC_BATCH, C_CONV - 1, C_WIDTH), 1.0)
    state_lru = nrm((N_C_LAYERS, DEC_BATCH, C_WIDTH), 0.5)
    norm_g = gain((DEPTH, D_MODEL))
    mla_w_in = nrm((N_A_LAYERS, D_MODEL, A_IN), D_MODEL ** -0.5)
    mla_g_q_lora = gain((N_A_LAYERS, A_Q_LORA))
    mla_g_kv = gain((N_A_LAYERS, A_KV_LORA))
    mla_w_uq = nrm((N_A_LAYERS, A_Q_LORA, A_HEADS * A_QK), A_Q_LORA ** -0.5)
    mla_w_ukv = nrm((N_A_LAYERS, A_KV_LORA, A_HEADS * (A_NOPE + A_V)), A_KV_LORA ** -0.5)
    mla_g_q = gain((N_A_LAYERS, A_QK))
    mla_g_k = gain((N_A_LAYERS, A_QK))
    mla_w_out = nrm((N_A_LAYERS, A_WIDTH, D_MODEL), A_WIDTH ** -0.5)
    hgrn_w_in = nrm((N_B_LAYERS, D_MODEL, 4 * B_WIDTH), D_MODEL ** -0.5)
    hgrn_lower_bounds = nrm((DEPTH, B_HEADS * B_DK), 1.0)
    hgrn_g_norm = gain((N_B_LAYERS, B_DV))
    hgrn_w_out = nrm((N_B_LAYERS, B_WIDTH, D_MODEL), B_WIDTH ** -0.5)
    rglru_w_in = nrm((N_C_LAYERS, D_MODEL, 2 * C_WIDTH), D_MODEL ** -0.5)
    rglru_conv_w = nrm((N_C_LAYERS, C_CONV, C_WIDTH), C_CONV ** -0.5)
    rglru_conv_b = nrm((N_C_LAYERS, C_WIDTH), 0.02)
    rglru_w_a = nrm((N_C_LAYERS, C_HEADS, C_BW, C_BW), C_BW ** -0.5)
    rglru_b_a = nrm((N_C_LAYERS, C_WIDTH), 0.1)
    rglru_w_x = nrm((N_C_LAYERS, C_HEADS, C_BW, C_BW), C_BW ** -0.5)
    rglru_b_x = nrm((N_C_LAYERS, C_WIDTH), 0.1)
    a_pow = jax.random.uniform(next(ks), (N_C_LAYERS, C_WIDTH), jnp.float32, 0.9, 0.999)
    a_base = a_pow ** (1.0 / C_GATE_C)
    rglru_L = jnp.log(a_base) - jnp.log1p(-a_base)
    rglru_w_out = nrm((N_C_LAYERS, C_WIDTH, D_MODEL), C_WIDTH ** -0.5)
    return {'x_prompt': x_prompt, 'x_sample': x_sample, 'cache_ckv': cache_ckv, 'cache_kpe': cache_kpe,
            'page_table': page_table, 'state_hgrn': state_hgrn, 'state_conv': state_conv, 'state_lru': state_lru,
            'norm_g': norm_g, 'mla_w_in': mla_w_in, 'mla_g_q_lora': mla_g_q_lora, 'mla_g_kv': mla_g_kv,
            'mla_w_uq': mla_w_uq, 'mla_w_ukv': mla_w_ukv, 'mla_g_q': mla_g_q, 'mla_g_k': mla_g_k,
            'mla_w_out': mla_w_out, 'hgrn_w_in': hgrn_w_in, 'hgrn_lower_bounds': hgrn_lower_bounds,
            'hgrn_g_norm': hgrn_g_norm, 'hgrn_w_out': hgrn_w_out, 'rglru_w_in': rglru_w_in,
            'rglru_conv_w': rglru_conv_w, 'rglru_conv_b': rglru_conv_b, 'rglru_w_a': rglru_w_a,
            'rglru_b_a': rglru_b_a, 'rglru_w_x': rglru_w_x, 'rglru_b_x': rglru_b_x, 'rglru_L': rglru_L,
            'rglru_w_out': rglru_w_out}


def reference(x_prompt, x_sample, cache_ckv, cache_kpe, page_table, state_hgrn, state_conv, state_lru,
              norm_g, mla_w_in, mla_g_q_lora, mla_g_kv, mla_w_uq, mla_w_ukv, mla_g_q, mla_g_k, mla_w_out,
              hgrn_w_in, hgrn_lower_bounds, hgrn_g_norm, hgrn_w_out, rglru_w_in, rglru_conv_w, rglru_conv_b,
              rglru_w_a, rglru_b_a, rglru_w_x, rglru_b_x, rglru_L, rglru_w_out):
    f32 = jnp.float32
    bsz, s_p, _ = x_prompt.shape
    s_s = x_sample.shape[1]
    past_len = page_table.shape[1] * PAGE_SIZE
    pos_p = jnp.arange(s_p)
    pos_s = past_len + jnp.arange(s_s)
    smx = jax.nn.softmax(hgrn_lower_bounds.astype(f32), axis=0)
    lower_bounds = jnp.cumsum(smx, axis=0) - smx[0]

    xp, xs = x_prompt, x_sample
    ckv_p, kpe_p, ckv_s, kpe_s = [], [], [], []
    hg_p, hg_s = [], []
    cv_p, cv_s, lr_p, lr_s = [], [], [], []
    for li in range(DEPTH):
        kind, j = li % N_MIXERS, li // N_MIXERS
        hp = rmsnorm(xp, norm_g[li])
        hs = rmsnorm(xs, norm_g[li])
        if kind == 0:
            qp, cp, kp, gp = mla_project(hp, pos_p, mla_w_in[j], mla_g_q_lora[j], mla_g_kv[j], mla_w_uq[j], mla_g_q[j])
            k_p, v_p = mla_expand(cp, kp, mla_w_ukv[j], mla_g_k[j])
            dp = mla_out(mla_prompt_attention(qp, k_p, v_p), gp, mla_w_out[j])
            qs, cs, kss, gs = mla_project(hs, pos_s, mla_w_in[j], mla_g_q_lora[j], mla_g_kv[j], mla_w_uq[j], mla_g_q[j])
            k_s, v_s = mla_expand(cs, kss, mla_w_ukv[j], mla_g_k[j])
            o_s = mla_sample_attention(qs, k_s, v_s, cache_ckv, cache_kpe, j, page_table, mla_w_ukv[j], mla_g_k[j])
            ds_ = mla_out(o_s, gs, mla_w_out[j])
            ckv_p.append(cp)
            kpe_p.append(kp)
            ckv_s.append(cs)
            kpe_s.append(kss)
        elif kind == 1:
            qp, kp, vp, gp, gatep = hgrn_project(hp, hgrn_w_in[j], lower_bounds[li])
            s0 = jnp.zeros((bsz, B_HEADS, B_DK, B_DV), f32)
            sp, op = lax.scan(gla_chunk, s0, (to_chunks(qp), to_chunks(kp), to_chunks(vp), to_chunks(gp)))
            dp = hgrn_out(from_chunks(op), gatep, hgrn_g_norm[j], hgrn_w_out[j])
            qs, kss, vs, gs, gates = hgrn_project(hs, hgrn_w_in[j], lower_bounds[li])
            ss, o_s = gla_chunk(state_hgrn[j].astype(f32), (qs, kss, vs, gs))
            ds_ = hgrn_out(o_s, gates, hgrn_g_norm[j], hgrn_w_out[j])
            hg_p.append(sp)
            hg_s.append(ss)
        else:
            dp, bufp, hlp = rglru_layer(hp, jnp.zeros((bsz, C_CONV - 1, C_WIDTH), hp.dtype), jnp.zeros((bsz, C_WIDTH), f32),
                                        rglru_w_in[j], rglru_conv_w[j], rglru_conv_b[j], rglru_w_a[j], rglru_b_a[j],
                                        rglru_w_x[j], rglru_b_x[j], rglru_L[j], rglru_w_out[j])
            ds_, bufs, hls = rglru_layer(hs, state_conv[j], state_lru[j],
                                         rglru_w_in[j], rglru_conv_w[j], rglru_conv_b[j], rglru_w_a[j], rglru_b_a[j],
                                         rglru_w_x[j], rglru_b_x[j], rglru_L[j], rglru_w_out[j])
            cv_p.append(bufp)
            cv_s.append(bufs)
            lr_p.append(hlp)
            lr_s.append(hls)
        xp = xp + dp.astype(xp.dtype)
        xs = xs + ds_.astype(xs.dtype)

    new_ckv_prompt = jnp.stack(ckv_p)
    new_kpe_prompt = jnp.stack(kpe_p)
    new_ckv_sample = jnp.stack(ckv_s)
    new_kpe_sample = jnp.stack(kpe_s)
    hgrn_prompt = jnp.stack(hg_p)
    hgrn_sample = jnp.stack(hg_s)
    conv_prompt = jnp.stack(cv_p)
    conv_sample = jnp.stack(cv_s)
    lru_prompt = jnp.stack(lr_p)
    lru_sample = jnp.stack(lr_s)
    return (xp, xs, new_ckv_prompt, new_kpe_prompt, new_ckv_sample, new_kpe_sample,
            hgrn_prompt, hgrn_sample, conv_prompt, conv_sample, lru_prompt, lru_sample)
```

```python
import functools
import math

import jax
import jax.numpy as jnp
from jax import lax
from jax.experimental import pallas as pl
from jax.experimental.pallas import tpu as pltpu

F32 = jnp.float32
BF16 = jnp.bfloat16

LANES = 128
SUBLANES = 8
VMEM_LIMIT_BYTES = 56 * 1024 * 1024

D_MODEL = 1024
PAGE_SIZE = 128
N_MIXERS = 3
EPS = 1e-6
ROPE_THETA = 10000.0
NEG_BIG = -1e30

A_HEADS = 8
A_NOPE = 64
A_ROPE = 32
A_QK = A_NOPE + A_ROPE
A_V = 64
A_Q_LORA = 384
A_KV_LORA = 256
A_WIDTH = A_HEADS * A_V
A_HALF = A_ROPE // 2
A_PADW = A_HEADS * LANES

B_HEADS = 8
B_DK = 128
B_DV = 128
B_CHUNK = 64

C_WIDTH = 1024
C_HEADS = 8
C_BW = 128
C_CONV = 4
C_GATE_C = 8.0


def _cparams(*sem):
    return pltpu.CompilerParams(dimension_semantics=sem, vmem_limit_bytes=VMEM_LIMIT_BYTES)


def _sigmoid(x):
    return 1.0 / (1.0 + jnp.exp(-x))


def _rms_rows(x, g):
    ms = jnp.mean(x * x, axis=-1, keepdims=True)
    return x * lax.rsqrt(ms + EPS) * g


def _dot_nt(a, b):
    return lax.dot_general(a, b, (((1,), (1,)), ((), ())), preferred_element_type=F32)


def _dot(a, b):
    return jnp.dot(a, b, preferred_element_type=F32)


def _norm_matmul_kernel(x_ref, g_ref, w_ref, o_ref):
    h = _rms_rows(x_ref[...], g_ref[...])
    o_ref[...] = _dot(h.astype(BF16), w_ref[...])


def norm_matmul(x, g, w_bf16, tm):
    m, d = x.shape
    n = w_bf16.shape[1]
    return pl.pallas_call(
        _norm_matmul_kernel,
        grid=(m // tm,),
        in_specs=[pl.BlockSpec((tm, d), lambda i: (i, 0)),
                  pl.BlockSpec((1, d), lambda i: (0, 0)),
                  pl.BlockSpec((d, n), lambda i: (0, 0))],
        out_specs=pl.BlockSpec((tm, n), lambda i: (i, 0)),
        out_shape=jax.ShapeDtypeStruct((m, n), F32),
        compiler_params=_cparams("parallel"),
        name="norm_matmul",
    )(x, g.reshape(1, d), w_bf16)


def _gated_out_kernel(o_ref, gate_ref, w_ref, x_ref, gn_ref, y_ref, *, head_norm):
    o = o_ref[...]
    if head_norm:
        parts = []
        for h in range(o.shape[1] // LANES):
            parts.append(_rms_rows(o[:, h * LANES:(h + 1) * LANES], gn_ref[...]))
        o = jnp.concatenate(parts, axis=1)
    gate = gate_ref[...]
    z = o * (gate * _sigmoid(gate))
    y_ref[...] = x_ref[...] + _dot(z.astype(BF16), w_ref[...])


def gated_out(o, gate_arr, gate_col, w_bf16, x, tm, g_norm=None):
    m, w = o.shape
    d = x.shape[1]
    head_norm = g_norm is not None
    gn = (g_norm if head_norm else jnp.ones((LANES,), F32)).reshape(1, LANES)
    return pl.pallas_call(
        functools.partial(_gated_out_kernel, head_norm=head_norm),
        grid=(m // tm,),
        in_specs=[pl.BlockSpec((tm, w), lambda i: (i, 0)),
                  pl.BlockSpec((tm, w), lambda i: (i, gate_col)),
                  pl.BlockSpec((w, d), lambda i: (0, 0)),
                  pl.BlockSpec((tm, d), lambda i: (i, 0)),
                  pl.BlockSpec((1, LANES), lambda i: (0, 0))],
        out_specs=pl.BlockSpec((tm, d), lambda i: (i, 0)),
        out_shape=jax.ShapeDtypeStruct((m, d), F32),
        compiler_params=_cparams("parallel"),
        name="gated_out",
    )(o, gate_arr, w_bf16, x, gn)


def _rope_lanes(x, c, s1, s2):
    n = x.shape[1]
    return x * c + pltpu.roll(x, n - A_HALF, 1) * s1 + pltpu.roll(x, A_HALF, 1) * s2


def _head_norm_blocks(x, g):
    parts = []
    for h in range(A_HEADS):
        blk = x[:, h * LANES:(h + 1) * LANES]
        ss = jnp.sum(blk * blk, axis=-1, keepdims=True)
        parts.append(blk * lax.rsqrt(ss * (1.0 / A_QK) + EPS) * g[:, h * LANES:(h + 1) * LANES])
    return jnp.concatenate(parts, axis=1)


def _mla_proj_kernel(x_ref, gn_ref, win_ref, gql_ref, gkv_ref, wuq_ref, gq_ref, wuk_ref, wuv_ref, gk_ref,
                     c_ref, s1_ref, s2_ref,
                     q_ref, k_ref, v_ref, gate_ref, ckv_ref, kpe_ref):
    h = _rms_rows(x_ref[...], gn_ref[...])
    p = _dot(h.astype(BF16), win_ref[...])
    gate_ref[...] = p[:, :A_WIDTH]
    cq = p[:, A_WIDTH:A_WIDTH + A_Q_LORA]
    ckv = p[:, A_WIDTH + A_Q_LORA:A_WIDTH + A_Q_LORA + A_KV_LORA]
    kpe = p[:, A_WIDTH + A_Q_LORA + A_KV_LORA:]
    c, s1, s2 = c_ref[...], s1_ref[...], s2_ref[...]
    c8 = jnp.concatenate([c] * A_HEADS, axis=1)
    s18 = jnp.concatenate([s1] * A_HEADS, axis=1)
    s28 = jnp.concatenate([s2] * A_HEADS, axis=1)

    q = _dot(_rms_rows(cq, gql_ref[...]).astype(BF16), wuq_ref[...])
    q = _head_norm_blocks(_rope_lanes(q, c8, s18, s28), gq_ref[...])
    q_ref[...] = q.astype(q_ref.dtype)

    ckvn = _rms_rows(ckv, gkv_ref[...])
    ckv_ref[...] = ckvn
    kper = _rope_lanes(kpe, c, s1, s2)
    kpe_ref[...] = kper
    cb = ckvn.astype(BF16)
    kn = _dot(cb, wuk_ref[...]) + jnp.concatenate([kper] * A_HEADS, axis=1)
    k_ref[...] = _head_norm_blocks(kn, gk_ref[...]).astype(k_ref.dtype)
    v_ref[...] = _dot(cb, wuv_ref[...]).astype(v_ref.dtype)


def mla_proj(x, g_norm, wts, rope_tabs, rows_per_seq, tm, qkv_dtype):
    m, d = x.shape
    c, s1, s2 = rope_tabs
    nblk = rows_per_seq // tm
    full = lambda a: pl.BlockSpec(a.shape, lambda i: (0, 0))
    tab = pl.BlockSpec((tm, LANES), lambda i: (i % nblk, 0))
    row = lambda n: pl.BlockSpec((tm, n), lambda i: (i, 0))
    n_in = wts["w_in"].shape[1]
    return pl.pallas_call(
        _mla_proj_kernel,
        grid=(m // tm,),
        in_specs=[row(d), pl.BlockSpec((1, d), lambda i: (0, 0)), full(wts["w_in"]),
                  full(wts["g_ql"]), full(wts["g_kv"]), full(wts["w_uq"]), full(wts["g_q"]),
                  full(wts["w_uk"]), full(wts["w_uv"]), full(wts["g_k"]), tab, tab, tab],
        out_specs=[row(A_PADW), row(A_PADW), row(A_WIDTH), row(A_WIDTH), row(A_KV_LORA), row(LANES)],
        out_shape=[jax.ShapeDtypeStruct((m, A_PADW), qkv_dtype),
                   jax.ShapeDtypeStruct((m, A_PADW), qkv_dtype),
                   jax.ShapeDtypeStruct((m, A_WIDTH), qkv_dtype),
                   jax.ShapeDtypeStruct((m, A_WIDTH), F32),
                   jax.ShapeDtypeStruct((m, A_KV_LORA), F32),
                   jax.ShapeDtypeStruct((m, LANES), F32)],
        compiler_params=_cparams("parallel"),
        name="mla_proj",
    )(x, g_norm.reshape(1, d), wts["w_in"], wts["g_ql"], wts["g_kv"], wts["w_uq"], wts["g_q"],
      wts["w_uk"], wts["w_uv"], wts["g_k"], c, s1, s2)


def _pad_heads(a, used):
    pad = [(0, 0)] * (a.ndim - 1) + [(0, LANES - used)]
    a = jnp.pad(a, pad)
    return a.reshape(a.shape[:-2] + (a.shape[-2] * LANES,))


def _mla_weights(w_in, g_ql, g_kv, w_uq, w_ukv, g_q, g_k):
    cq, ckv, kpe, gate = jnp.split(w_in, [A_Q_LORA, A_Q_LORA + A_KV_LORA, A_Q_LORA + A_KV_LORA + A_ROPE], axis=1)
    kpe_blk = jnp.pad(kpe, ((0, 0), (A_NOPE, LANES - A_QK)))
    scale = A_QK ** -0.5
    ukv = w_ukv.reshape(A_KV_LORA, A_HEADS, A_NOPE + A_V)
    uk, uv = ukv[:, :, :A_NOPE], ukv[:, :, A_NOPE:]
    ukt_pad = _pad_heads(uk, A_NOPE).T
    ukt_hi = ukt_pad.astype(BF16)
    return {
        "w_in": jnp.concatenate([gate, cq, ckv, kpe_blk], axis=1).astype(BF16),
        "g_ql": g_ql.reshape(1, -1), "g_kv": g_kv.reshape(1, -1),
        "w_uq": _pad_heads(w_uq.reshape(A_Q_LORA, A_HEADS, A_QK), A_QK).astype(BF16),
        "g_q": _pad_heads(jnp.broadcast_to(g_q * scale, (A_HEADS, A_QK)), A_QK).reshape(1, -1),
        "g_k": _pad_heads(jnp.broadcast_to(g_k, (A_HEADS, A_QK)), A_QK).reshape(1, -1),
        "w_uk": _pad_heads(uk, A_NOPE).astype(BF16),
        "w_uv": uv.reshape(A_KV_LORA, A_WIDTH).astype(BF16),
        "ukt": uk.reshape(A_KV_LORA, A_HEADS * A_NOPE).T.astype(BF16),
        "ukt_hi": ukt_hi,
        "ukt_lo": (ukt_pad - ukt_hi.astype(F32)).astype(BF16),
    }


def _rope_tables(pos):
    inv = 1.0 / (ROPE_THETA ** (jnp.arange(0, A_ROPE, 2, dtype=F32) / A_ROPE))
    ang = pos.astype(F32)[:, None] * inv[None, :]
    cos, sin = jnp.cos(ang), jnp.sin(ang)
    n = pos.shape[0]
    z = lambda w: jnp.zeros((n, w), F32)
    c = jnp.concatenate([jnp.ones((n, A_NOPE), F32), cos, cos, z(LANES - A_QK)], axis=1)
    s1 = jnp.concatenate([z(A_NOPE), -sin, z(A_HALF), z(LANES - A_QK)], axis=1)
    s2 = jnp.concatenate([z(A_NOPE + A_HALF), sin, z(LANES - A_QK)], axis=1)
    return c, s1, s2


def _flash_kernel(qi_ref, kj_ref, q_ref, k_ref, v_ref, o_ref, m_scr, l_scr, acc_scr):
    p_id = pl.program_id(1)
    qi = qi_ref[p_id]
    kj = kj_ref[p_id]
    tq, tk = q_ref.shape[0], k_ref.shape[0]

    @pl.when(kj == 0)
    def _():
        m_scr[...] = jnp.full(m_scr.shape, NEG_BIG, F32)
        l_scr[...] = jnp.zeros(l_scr.shape, F32)
        acc_scr[...] = jnp.zeros(acc_scr.shape, F32)

    def step(masked):
        if masked:
            row = lax.broadcasted_iota(jnp.int32, (tq, tk), 0)
            col = lax.broadcasted_iota(jnp.int32, (tq, tk), 1)
            keep = col <= row
        for h in range(A_HEADS):
            s = _dot_nt(q_ref[:, h * LANES:(h + 1) * LANES], k_ref[:, h * LANES:(h + 1) * LANES])
            if masked:
                s = jnp.where(keep, s, NEG_BIG)
            m_prev = m_scr[h]
            m_new = jnp.maximum(m_prev, jnp.max(s, axis=-1, keepdims=True))
            p = jnp.exp(s - m_new)
            corr = jnp.exp(m_prev - m_new)
            l_scr[h] = l_scr[h] * corr + jnp.sum(p, axis=-1, keepdims=True)
            acc_scr[h] = acc_scr[h] * corr + _dot(p.astype(BF16), v_ref[:, h * A_V:(h + 1) * A_V])
            m_scr[h] = m_new

    @pl.when(kj < qi)
    def _():
        step(False)

    @pl.when(kj == qi)
    def _():
        step(True)
        o_ref[...] = jnp.concatenate([acc_scr[h] / l_scr[h] for h in range(A_HEADS)], axis=1)


def flash_prompt(q, k, v, batch, seq, tq):
    nq = seq // tq
    pairs = [(i, j) for i in range(nq) for j in range(i + 1)]
    qi = jnp.asarray([p[0] for p in pairs], jnp.int32)
    kj = jnp.asarray([p[1] for p in pairs], jnp.int32)
    grid_spec = pltpu.PrefetchScalarGridSpec(
        num_scalar_prefetch=2,
        grid=(batch, len(pairs)),
        in_specs=[pl.BlockSpec((tq, A_PADW), lambda b, p, qi, kj: (b * nq + qi[p], 0)),
                  pl.BlockSpec((tq, A_PADW), lambda b, p, qi, kj: (b * nq + kj[p], 0)),
                  pl.BlockSpec((tq, A_WIDTH), lambda b, p, qi, kj: (b * nq + kj[p], 0))],
        out_specs=pl.BlockSpec((tq, A_WIDTH), lambda b, p, qi, kj: (b * nq + qi[p], 0)),
        scratch_shapes=[pltpu.VMEM((A_HEADS, tq, 1), F32), pltpu.VMEM((A_HEADS, tq, 1), F32),
                        pltpu.VMEM((A_HEADS, tq, A_V), F32)])
    return pl.pallas_call(
        _flash_kernel, grid_spec=grid_spec,
        out_shape=jax.ShapeDtypeStruct((batch * seq, A_WIDTH), F32),
        compiler_params=_cparams("parallel", "arbitrary"),
        name="flash_prompt",
    )(qi, kj, q, k, v)


def _sample_attn_kernel(pt_ref, q_ref, gk_ref, ukt_ref, ukhi_ref, uklo_ref, wuv_ref, cnew_ref, knew_ref, *rest,
                        n_groups, pages_per_step):
    g = pages_per_step
    ckv_refs, kpe_refs = rest[:g], rest[g:2 * g]
    o_ref = rest[2 * g]
    lhs_scr, qpe_scr, m_scr, l_scr, acc_scr = rest[2 * g + 1:]
    step = pl.program_id(1)
    n_nope = A_HEADS * A_NOPE

    @pl.when(step == 0)
    def _():
        qg = q_ref[...] * gk_ref[...]
        lane = lax.broadcasted_iota(jnp.int32, (A_HEADS, A_PADW), 1)
        row = lax.broadcasted_iota(jnp.int32, (A_HEADS, A_PADW), 0)
        sel = (lane >= row * LANES) & (lane < row * LANES + A_NOPE)
        qexp = jnp.where(sel, jnp.broadcast_to(qg, (A_HEADS, A_PADW)), 0.0)
        q_hi = qexp.astype(BF16)
        q_lo = (qexp - q_hi.astype(F32)).astype(BF16)
        qabs = _dot(q_hi, ukhi_ref[...]) + _dot(q_hi, uklo_ref[...]) + _dot(q_lo, ukhi_ref[...])
        lhs_scr[:n_nope, :] = ukt_ref[...]
        lhs_scr[n_nope:, :] = jnp.concatenate(
            [qabs, jnp.zeros((lhs_scr.shape[0] - n_nope - A_HEADS, A_KV_LORA), F32)], axis=0).astype(BF16)
        qpe = jnp.concatenate([qg[:, h * LANES + A_NOPE:h * LANES + A_QK] for h in range(A_HEADS)], axis=0)
        qpe_scr[...] = jnp.concatenate([qpe, jnp.zeros_like(qpe)], axis=0).astype(BF16)
        m_scr[...] = jnp.full(m_scr.shape, NEG_BIG, F32)
        l_scr[...] = jnp.zeros(l_scr.shape, F32)
        acc_scr[...] = jnp.zeros(acc_scr.shape, F32)

    def attend(c, kpe, n_valid):
        n = c.shape[0]
        cb = c.astype(BF16)
        a = _dot_nt(lhs_scr[...], cb)
        kn = a[:n_nope]
        ss = jnp.sum((kn * kn).reshape(A_HEADS, A_NOPE, n), axis=1)
        s_pe = _dot_nt(qpe_scr[...], kpe.astype(BF16))[:A_HEADS]
        pe_ss = _dot_nt(jnp.ones((A_HEADS, A_ROPE), BF16), (kpe * kpe).astype(BF16))
        rs = lax.rsqrt((ss + pe_ss) * (1.0 / A_QK) + EPS)
        s = (a[n_nope:n_nope + A_HEADS] + s_pe) * rs
        if n_valid is not None:
            s = jnp.where(lax.broadcasted_iota(jnp.int32, s.shape, 1) < n_valid, s, NEG_BIG)
        m_prev = m_scr[...]
        m_new = jnp.maximum(m_prev, jnp.max(s, axis=-1, keepdims=True))
        p = jnp.exp(s - m_new)
        corr = jnp.exp(m_prev - m_new)
        l_scr[...] = l_scr[...] * corr + jnp.sum(p, axis=-1, keepdims=True)
        acc_scr[...] = acc_scr[...] * corr + _dot(p.astype(BF16), cb)
        m_scr[...] = m_new

    @pl.when(step < n_groups)
    def _():
        c = jnp.concatenate([r[...] for r in ckv_refs], axis=0)
        kpe = jnp.concatenate([r[...] for r in kpe_refs], axis=0)
        attend(c, kpe, None)

    @pl.when(step == n_groups)
    def _():
        attend(jnp.broadcast_to(cnew_ref[...], (LANES, A_KV_LORA)),
               jnp.broadcast_to(knew_ref[...], (LANES, A_ROPE)), 1)
        lat = acc_scr[...] / l_scr[...]
        full = _dot(lat.astype(BF16), wuv_ref[...])
        lane = lax.broadcasted_iota(jnp.int32, full.shape, 1)
        row = lax.broadcasted_iota(jnp.int32, full.shape, 0)
        sel = (lane >= row * A_V) & (lane < (row + 1) * A_V)
        o_ref[...] = jnp.sum(jnp.where(sel, full, 0.0), axis=0, keepdims=True)


def _page_index(b, s, pt, *, i, la, n_groups, pages_per_step, n_pages):
    grp = jnp.minimum(s, n_groups - 1)
    return (la, pt[b * n_pages + grp * pages_per_step + i], 0, 0)


def mla_sample_attention(q_s, ckv_new, kpe_new, cache_ckv, cache_kpe, la, page_table, wts, pages_per_step):
    db, n_pages = page_table.shape
    n_groups = n_pages // pages_per_step
    pidx = functools.partial(_page_index, la=la, n_groups=n_groups, pages_per_step=pages_per_step, n_pages=n_pages)
    row3 = lambda n: pl.BlockSpec((None, 1, n), lambda b, s, pt: (b, 0, 0))
    full = lambda a: pl.BlockSpec(a.shape, lambda b, s, pt: (0, 0))
    lhs_rows = A_HEADS * A_NOPE + 2 * SUBLANES
    grid_spec = pltpu.PrefetchScalarGridSpec(
        num_scalar_prefetch=1,
        grid=(db, n_groups + 1),
        in_specs=[row3(A_PADW), full(wts["g_k"]), full(wts["ukt"]), full(wts["ukt_hi"]), full(wts["ukt_lo"]),
                  full(wts["w_uv"]), row3(A_KV_LORA), row3(A_ROPE)]
        + [pl.BlockSpec((None, None, PAGE_SIZE, A_KV_LORA), functools.partial(pidx, i=i))
           for i in range(pages_per_step)]
        + [pl.BlockSpec((None, None, PAGE_SIZE, A_ROPE), functools.partial(pidx, i=i))
           for i in range(pages_per_step)],
        out_specs=pl.BlockSpec((None, 1, A_WIDTH), lambda b, s, pt: (b, 0, 0)),
        scratch_shapes=[pltpu.VMEM((lhs_rows, A_KV_LORA), BF16), pltpu.VMEM((2 * A_HEADS, A_ROPE), BF16),
                        pltpu.VMEM((A_HEADS, 1), F32), pltpu.VMEM((A_HEADS, 1), F32),
                        pltpu.VMEM((A_HEADS, A_KV_LORA), F32)])
    out = pl.pallas_call(
        functools.partial(_sample_attn_kernel, n_groups=n_groups, pages_per_step=pages_per_step),
        grid_spec=grid_spec,
        out_shape=jax.ShapeDtypeStruct((db, 1, A_WIDTH), F32),
        compiler_params=_cparams("parallel", "arbitrary"),
        name="mla_sample_attention",
    )(page_table.reshape(-1), q_s.reshape(db, 1, A_PADW), wts["g_k"], wts["ukt"], wts["ukt_hi"], wts["ukt_lo"],
      wts["w_uv"], ckv_new.reshape(db, 1, A_KV_LORA), kpe_new.reshape(db, 1, A_ROPE),
      *([cache_ckv] * pages_per_step), *([cache_kpe] * pages_per_step))
    return out.reshape(db, A_WIDTH)


def _hgrn_lower_bound(lb_all, li):
    e = jnp.exp(lb_all - jnp.max(lb_all, axis=0, keepdims=True))
    smx = e / jnp.sum(e, axis=0, keepdims=True)
    return jnp.sum(smx[1:li + 1], axis=0, keepdims=True) if li > 0 else jnp.zeros_like(smx[:1])


def _hgrn_gates(q, f, lb):
    g = jnp.log(lb + (1.0 - lb) * _sigmoid(f))
    k = (1.0 - lb) * _sigmoid(-f)
    qf = q * _sigmoid(q) * (B_DK ** -0.5)
    return qf, k, g


def _cumsum_rows(x):
    n = x.shape[0]
    row = lax.broadcasted_iota(jnp.int32, x.shape, 0)
    d = 1
    while d < n:
        x = x + jnp.where(row >= d, pltpu.roll(x, d, 0), 0.0)
        d *= 2
    return x


def _bcast_row_in_groups(x, j):
    n, w = x.shape
    x3 = x.reshape(n // SUBLANES, SUBLANES, w)
    return jnp.broadcast_to(x3[:, j:j + 1, :], x3.shape).reshape(n, w)


def _intra_chunk_att(qf, k, b):
    c = qf.shape[0]
    row = lax.broadcasted_iota(jnp.int32, (c, 1), 0)
    tt = lax.broadcasted_iota(jnp.int32, (c, c), 0)
    ss = lax.broadcasted_iota(jnp.int32, (c, c), 1)
    att = jnp.where(tt == ss, jnp.sum(qf * k, axis=-1, keepdims=True), 0.0)
    half = c // 2
    while half >= 1:
        blk = 2 * half
        if blk >= SUBLANES:
            bref = jnp.concatenate(
                [jnp.broadcast_to(b[m:m + 1], (blk, b.shape[1])) for m in range(half, c, blk)], axis=0)
        else:
            rm = row % SUBLANES
            bref = _bcast_row_in_groups(b, SUBLANES - half)
            for m in range(SUBLANES - half - blk, 0, -blk):
                bref = jnp.where(rm < m + half, _bcast_row_in_groups(b, m), bref)
        upper = (row % blk) >= half
        e = jnp.exp(jnp.where(upper, b - bref, bref - b))
        qt = jnp.where(upper, qf * e, 0.0)
        kt = jnp.where(upper, 0.0, k * e)
        p = _dot_nt(qt.astype(BF16), kt.astype(BF16))
        if blk < c:
            p = jnp.where((tt // blk) == (ss // blk), p, 0.0)
        att = att + p
        half //= 2
    return att


def _hgrn_prompt_kernel(q_ref, f_ref, v_ref, lb_ref, o_ref, st_ref, state_scr, *, li):
    t = pl.program_id(2)

    @pl.when(t == 0)
    def _():
        state_scr[...] = jnp.zeros(state_scr.shape, F32)

    lb = _hgrn_lower_bound(lb_ref[...], li)
    for ci in range(q_ref.shape[0] // B_CHUNK):
        sl = slice(ci * B_CHUNK, (ci + 1) * B_CHUNK)
        qf, k, g = _hgrn_gates(q_ref[sl, :], f_ref[sl, :], lb)
        v = v_ref[sl, :]
        b = _cumsum_rows(g)
        st = state_scr[...]
        vb = v.astype(BF16)
        inter = _dot_nt((qf * jnp.exp(b)).astype(BF16), st.astype(BF16))
        att = _intra_chunk_att(qf, k, b)
        o_ref[sl, :] = inter + _dot(att.astype(BF16), vb)
        blast = b[B_CHUNK - 1:B_CHUNK]
        kd = (k * jnp.exp(blast - b)).astype(BF16)
        upd = lax.dot_general(vb, kd, (((0,), (0,)), ((), ())), preferred_element_type=F32)
        state_scr[...] = jnp.exp(blast) * st + upd

    @pl.when(t == pl.num_programs(2) - 1)
    def _():
        st_ref[...] = state_scr[...].T


def hgrn_prompt(proj, lower_bounds, li, batch, seq, tblk):
    nt = seq // tblk
    col = lambda off: pl.BlockSpec((tblk, LANES), lambda b, h, t: (b * nt + t, off + h))
    return pl.pallas_call(
        functools.partial(_hgrn_prompt_kernel, li=li),
        grid=(batch, B_HEADS, nt),
        in_specs=[col(0), col(B_HEADS), col(2 * B_HEADS),
                  pl.BlockSpec((lower_bounds.shape[0], LANES), lambda b, h, t: (0, h))],
        out_specs=[pl.BlockSpec((tblk, LANES), lambda b, h, t: (b * nt + t, h)),
                   pl.BlockSpec((None, None, B_DK, B_DV), lambda b, h, t: (b, h, 0, 0))],
        out_shape=[jax.ShapeDtypeStruct((batch * seq, B_HEADS * B_DV), F32),
                   jax.ShapeDtypeStruct((batch, B_HEADS, B_DK, B_DV), F32)],
        scratch_shapes=[pltpu.VMEM((B_DV, B_DK), F32)],
        compiler_params=_cparams("parallel", "parallel", "arbitrary"),
        name="hgrn_prompt",
    )(proj, proj, proj, lower_bounds)


def _hgrn_sample_gates_kernel(p_ref, lb_ref, qe_ref, eg_ref, k_ref, av_ref, *, li):
    w = B_HEADS * B_DK
    lb = _hgrn_lower_bound(lb_ref[...], li)
    qf, k, g = _hgrn_gates(p_ref[:, :w], p_ref[:, w:2 * w], lb)
    v = p_ref[:, 2 * w:3 * w]
    eg = jnp.exp(g)
    qe_ref[...] = qf * eg
    eg_ref[...] = eg
    k_ref[...] = k
    qk = qf * k
    av_ref[...] = jnp.concatenate(
        [jnp.sum(qk[:, h * LANES:(h + 1) * LANES], axis=-1, keepdims=True) * v[:, h * LANES:(h + 1) * LANES]
         for h in range(B_HEADS)], axis=1)


def _hgrn_sample_state_kernel(st_ref, qe_ref, egt_ref, kt_ref, v_ref, av_ref, o_ref, ns_ref):
    for h in range(B_HEADS):
        st = st_ref[h]
        ns_ref[h] = egt_ref[:, h:h + 1] * st + kt_ref[:, h:h + 1] * v_ref[h:h + 1, :]
        qe = jnp.broadcast_to(qe_ref[h:h + 1, :], (2 * SUBLANES, B_DK)).astype(BF16)
        o_ref[h:h + 1, :] = _dot(qe, st.astype(BF16))[:1] + av_ref[h:h + 1, :]


def hgrn_sample(proj_s, lower_bounds, li, state):
    db = proj_s.shape[0]
    w = B_HEADS * B_DK
    whole = lambda a: pl.BlockSpec(a.shape, lambda i: (0,) * a.ndim)
    qe, eg, k, av = pl.pallas_call(
        functools.partial(_hgrn_sample_gates_kernel, li=li),
        grid=(1,),
        in_specs=[whole(proj_s), whole(lower_bounds)],
        out_specs=[pl.BlockSpec((db, w), lambda i: (0, 0))] * 4,
        out_shape=[jax.ShapeDtypeStruct((db, w), F32)] * 4,
        compiler_params=_cparams("arbitrary"),
        name="hgrn_sample_gates",
    )(proj_s, lower_bounds)
    heads = lambda a: a.reshape(db, B_HEADS, B_DK)
    cols = lambda a: heads(a).transpose(0, 2, 1)
    hrow = pl.BlockSpec((None, B_HEADS, B_DK), lambda b: (b, 0, 0))
    hcol = pl.BlockSpec((None, B_DK, B_HEADS), lambda b: (b, 0, 0))
    stspec = pl.BlockSpec((None, B_HEADS, B_DK, B_DV), lambda b: (b, 0, 0, 0))
    o, new_state = pl.pallas_call(
        _hgrn_sample_state_kernel,
        grid=(db,),
        in_specs=[stspec, hrow, hcol, hcol, hrow, hrow],
        out_specs=[hrow, stspec],
        out_shape=[jax.ShapeDtypeStruct((db, B_HEADS, B_DV), F32),
                   jax.ShapeDtypeStruct(state.shape, F32)],
        compiler_params=_cparams("parallel"),
        name="hgrn_sample_state",
    )(state, heads(qe), cols(eg), cols(k), heads(proj_s[:, 2 * w:3 * w]), heads(av))
    return o.reshape(db, w), new_state


def _rglru_gates(y, wax_ref, ba, bx, lam):
    yb = y.astype(BF16)
    rs, is_ = [], []
    for n in range(C_HEADS):
        ax = _dot(yb[:, n * C_BW:(n + 1) * C_BW], wax_ref[n])
        rs.append(ax[:, :C_BW])
        is_.append(ax[:, C_BW:])
    r = _sigmoid(jnp.concatenate(rs, axis=1) + ba)
    i = _sigmoid(jnp.concatenate(is_, axis=1) + bx)
    softplus = jnp.maximum(-lam, 0.0) + jnp.log1p(jnp.exp(-jnp.abs(lam)))
    log_a = -C_GATE_C * r * softplus
    a = jnp.exp(log_a)
    th = jnp.tanh(log_a)
    u = jnp.sqrt(-2.0 * th / (1.0 - th)) * (i * y)
    return a, u


def _rglru_prompt_kernel(xb_ref, cw_ref, cb_ref, wax_ref, ba_ref, bx_ref, lam_ref,
                         hs_ref, tail_ref, hl_ref, prev_scr, h_scr, a_scr, u_scr):
    t = pl.program_id(1)
    tb = xb_ref.shape[0]

    @pl.when(t == 0)
    def _():
        prev_scr[...] = jnp.zeros(prev_scr.shape, F32)
        h_scr[...] = jnp.zeros(h_scr.shape, F32)

    xb = xb_ref[...]
    prev = prev_scr[...]
    row = lax.broadcasted_iota(jnp.int32, (tb, 1), 0)
    y = cb_ref[...] + cw_ref[C_CONV - 1:C_CONV, :] * xb
    for d in range(1, C_CONV):
        sh = pltpu.roll(xb, d, 0)
        for r in range(d):
            sh = jnp.where(row == r, prev[SUBLANES - d + r:SUBLANES - d + r + 1, :], sh)
        y = y + cw_ref[C_CONV - 1 - d:C_CONV - d, :] * sh
    prev_scr[...] = xb[tb - SUBLANES:, :]
    a, u = _rglru_gates(y, wax_ref, ba_ref[...], bx_ref[...], lam_ref[...])
    a_scr[...] = a
    u_scr[...] = u

    def body(i, h):
        base = pl.multiple_of(i * SUBLANES, SUBLANES)
        for r in range(SUBLANES):
            h = a_scr[pl.ds(base + r, 1), :] * h + u_scr[pl.ds(base + r, 1), :]
            hs_ref[pl.ds(base + r, 1), :] = h
        return h

    h = lax.fori_loop(0, tb // SUBLANES, body, h_scr[...])
    h_scr[...] = h

    @pl.when(t == pl.num_programs(1) - 1)
    def _():
        tail_ref[...] = xb[tb - SUBLANES:, :]
        hl_ref[...] = h


def _rglru_weights(conv_w, conv_b, w_a, b_a, w_x, b_x, lam):
    r1 = lambda a: a.reshape(1, -1)
    return {"cw": conv_w, "cb": r1(conv_b), "wax": jnp.concatenate([w_a, w_x], axis=2).astype(BF16),
            "ba": r1(b_a), "bx": r1(b_x), "lam": r1(lam)}


def rglru_prompt(proj, wts, batch, seq, tblk):
    nt = seq // tblk
    w = C_WIDTH
    full = lambda a: pl.BlockSpec(a.shape, lambda b, t: (0,) * a.ndim)
    names = ("cw", "cb", "wax", "ba", "bx", "lam")
    return pl.pallas_call(
        _rglru_prompt_kernel,
        grid=(batch, nt),
        in_specs=[pl.BlockSpec((tblk, w), lambda b, t: (b * nt + t, 0))] + [full(wts[n]) for n in names],
        out_specs=[pl.BlockSpec((tblk, w), lambda b, t: (b * nt + t, 0)),
                   pl.BlockSpec((None, SUBLANES, w), lambda b, t: (b, 0, 0)),
                   pl.BlockSpec((None, 1, w), lambda b, t: (b, 0, 0))],
        out_shape=[jax.ShapeDtypeStruct((batch * seq, w), F32),
                   jax.ShapeDtypeStruct((batch, SUBLANES, w), F32),
                   jax.ShapeDtypeStruct((batch, 1, w), F32)],
        scratch_shapes=[pltpu.VMEM((SUBLANES, w), F32), pltpu.VMEM((1, w), F32),
                        pltpu.VMEM((tblk, w), F32), pltpu.VMEM((tblk, w), F32)],
        compiler_params=_cparams("parallel", "arbitrary"),
        name="rglru_prompt",
    )(proj, *[wts[n] for n in names])


def _rglru_sample_kernel(xb_ref, buf_ref, h0_ref, cw_ref, cb_ref, wax_ref, ba_ref, bx_ref, lam_ref,
                         hs_ref, nbuf_ref):
    xb = xb_ref[...]
    y = cb_ref[...] + cw_ref[C_CONV - 1:C_CONV, :] * xb
    for j in range(C_CONV - 1):
        y = y + cw_ref[j:j + 1, :] * buf_ref[j]
    a, u = _rglru_gates(y, wax_ref, ba_ref[...], bx_ref[...], lam_ref[...])
    hs_ref[...] = a * h0_ref[...] + u
    for j in range(C_CONV - 2):
        nbuf_ref[j] = buf_ref[j + 1]
    nbuf_ref[C_CONV - 2] = xb


def rglru_sample(proj_s, buf_t, h0, wts):
    db = proj_s.shape[0]
    w = C_WIDTH
    whole = lambda a: pl.BlockSpec(a.shape, lambda i: (0,) * a.ndim)
    names = ("cw", "cb", "wax", "ba", "bx", "lam")
    return pl.pallas_call(
        _rglru_sample_kernel,
        grid=(1,),
        in_specs=[pl.BlockSpec((db, w), lambda i: (0, 0)), whole(buf_t), whole(h0)] + [whole(wts[n]) for n in names],
        out_specs=[pl.BlockSpec((db, w), lambda i: (0, 0)), whole(buf_t)],
        out_shape=[jax.ShapeDtypeStruct((db, w), F32), jax.ShapeDtypeStruct(buf_t.shape, F32)],
        compiler_params=_cparams("arbitrary"),
        name="rglru_sample",
    )(proj_s, buf_t, h0, *[wts[n] for n in names])


TM_PROMPT = 512
TQ_FLASH = 512
T_HGRN = 256
T_RGLRU = 256
PAGES_PER_STEP = 8


def kernel(x_prompt, x_sample, cache_ckv, cache_kpe, page_table, state_hgrn, state_conv, state_lru, norm_g, mla_w_in, mla_g_q_lora, mla_g_kv, mla_w_uq, mla_w_ukv, mla_g_q, mla_g_k, mla_w_out, hgrn_w_in, hgrn_lower_bounds, hgrn_g_norm, hgrn_w_out, rglru_w_in, rglru_conv_w, rglru_conv_b, rglru_w_a, rglru_b_a, rglru_w_x, rglru_b_x, rglru_L, rglru_w_out):
    bsz, s_p, d = x_prompt.shape
    db, s_s, _ = x_sample.shape
    assert s_s == 1 and d == D_MODEL
    depth = norm_g.shape[0]
    past_len = page_table.shape[1] * PAGE_SIZE
    xp = x_prompt.reshape(bsz * s_p, d)
    xs = x_sample.reshape(db, d)
    tabs_p = _rope_tables(jnp.arange(s_p))
    tabs_s = _rope_tables(jnp.full((db,), past_len))

    ckv_p, kpe_p, ckv_s, kpe_s = [], [], [], []
    hg_p, hg_s, cv_p, cv_s, lr_p, lr_s = [], [], [], [], [], []
    for li in range(depth):
        kind, j = li % N_MIXERS, li // N_MIXERS
        if kind == 0:
            wts = _mla_weights(mla_w_in[j], mla_g_q_lora[j], mla_g_kv[j], mla_w_uq[j], mla_w_ukv[j],
                               mla_g_q[j], mla_g_k[j])
            w_out = mla_w_out[j].astype(BF16)
            q, k, v, gate, ckv, kpe_blk = mla_proj(xp, norm_g[li], wts, tabs_p, s_p, TM_PROMPT, BF16)
            o = flash_prompt(q, k, v, bsz, s_p, TQ_FLASH)
            xp = gated_out(o, gate, 0, w_out, xp, TM_PROMPT)
            ckv_p.append(ckv.reshape(bsz, s_p, A_KV_LORA))
            kpe_p.append(kpe_blk[:, A_NOPE:A_QK].reshape(bsz, s_p, A_ROPE))

            q_s, _, _, gate_s, ckv_n, kpe_blk_s = mla_proj(xs, norm_g[li], wts, tabs_s, db, db, F32)
            kpe_n = kpe_blk_s[:, A_NOPE:A_QK]
            o_s = mla_sample_attention(q_s, ckv_n, kpe_n, cache_ckv, cache_kpe, j, page_table, wts, PAGES_PER_STEP)
            xs = gated_out(o_s, gate_s, 0, w_out, xs, db)
            ckv_s.append(ckv_n.reshape(db, 1, A_KV_LORA))
            kpe_s.append(kpe_n.reshape(db, 1, A_ROPE))
        elif kind == 1:
            w_in = hgrn_w_in[j].astype(BF16)
            w_out = hgrn_w_out[j].astype(BF16)
            proj = norm_matmul(xp, norm_g[li], w_in, TM_PROMPT // 2)
            o, st = hgrn_prompt(proj, hgrn_lower_bounds, li, bsz, s_p, T_HGRN)
            xp = gated_out(o, proj, 3, w_out, xp, TM_PROMPT, hgrn_g_norm[j])
            hg_p.append(st)

            proj_s = norm_matmul(xs, norm_g[li], w_in, db)
            o_s, st_s = hgrn_sample(proj_s, hgrn_lower_bounds, li, state_hgrn[j])
            xs = gated_out(o_s, proj_s, 3, w_out, xs, db, hgrn_g_norm[j])
            hg_s.append(st_s)
        else:
            w_in = rglru_w_in[j].astype(BF16)
            w_out = rglru_w_out[j].astype(BF16)
            wts = _rglru_weights(rglru_conv_w[j], rglru_conv_b[j], rglru_w_a[j], rglru_b_a[j],
                                 rglru_w_x[j], rglru_b_x[j], rglru_L[j])
            proj = norm_matmul(xp, norm_g[li], w_in, TM_PROMPT)
            hs, tail, hl = rglru_prompt(proj, wts, bsz, s_p, T_RGLRU)
            xp = gated_out(hs, proj, 1, w_out, xp, TM_PROMPT)
            cv_p.append(tail[:, SUBLANES - (C_CONV - 1):, :])
            lr_p.append(hl.reshape(bsz, C_WIDTH))

            proj_s = norm_matmul(xs, norm_g[li], w_in, db)
            hs_s, nbuf = rglru_sample(proj_s, state_conv[j].transpose(1, 0, 2), state_lru[j], wts)
            xs = gated_out(hs_s, proj_s, 1, w_out, xs, db)
            cv_s.append(nbuf.transpose(1, 0, 2))
            lr_s.append(hs_s)

    return (xp.reshape(bsz, s_p, d), xs.reshape(db, 1, d),
            jnp.stack(ckv_p), jnp.stack(kpe_p), jnp.stack(ckv_s), jnp.stack(kpe_s),
            jnp.stack(hg_p), jnp.stack(hg_s), jnp.stack(cv_p), jnp.stack(cv_s),
            jnp.stack(lr_p), jnp.stack(lr_s))
```

```python
import functools
import math

import jax
import jax.numpy as jnp
from jax import lax
from jax.experimental import pallas as pl
from jax.experimental.pallas import tpu as pltpu

F32 = jnp.float32
BF16 = jnp.bfloat16

LANES = 128
SUBLANES = 8
VMEM_LIMIT_BYTES = 56 * 1024 * 1024

D_MODEL = 1024
PAGE_SIZE = 128
N_MIXERS = 3
EPS = 1e-6
ROPE_THETA = 10000.0
NEG_BIG = -1e30

A_HEADS = 8
A_NOPE = 64
A_ROPE = 32
A_QK = A_NOPE + A_ROPE
A_V = 64
A_Q_LORA = 384
A_KV_LORA = 256
A_WIDTH = A_HEADS * A_V
A_HALF = A_ROPE // 2
A_PADW = A_HEADS * LANES

B_HEADS = 8
B_DK = 128
B_DV = 128
B_CHUNK = 64

C_WIDTH = 1024
C_HEADS = 8
C_BW = 128
C_CONV = 4
C_GATE_C = 8.0


def _cparams(*sem):
    return pltpu.CompilerParams(dimension_semantics=sem, vmem_limit_bytes=VMEM_LIMIT_BYTES)


def _sigmoid(x):
    return 1.0 / (1.0 + jnp.exp(-x))


def _rms_rows(x, g):
    ms = jnp.mean(x * x, axis=-1, keepdims=True)
    return x * lax.rsqrt(ms + EPS) * g


def _dot_nt(a, b):
    return lax.dot_general(a, b, (((1,), (1,)), ((), ())), preferred_element_type=F32)


def _dot(a, b):
    return jnp.dot(a, b, preferred_element_type=F32)


def _norm_matmul_kernel(x_ref, g_ref, w_ref, o_ref):
    h = _rms_rows(x_ref[...], g_ref[...])
    o_ref[...] = _dot(h.astype(BF16), w_ref[...])


def norm_matmul(x, g, w_bf16, tm):
    m, d = x.shape
    n = w_bf16.shape[1]
    return pl.pallas_call(
        _norm_matmul_kernel,
        grid=(m // tm,),
        in_specs=[pl.BlockSpec((tm, d), lambda i: (i, 0)),
                  pl.BlockSpec((1, d), lambda i: (0, 0)),
                  pl.BlockSpec((d, n), lambda i: (0, 0))],
        out_specs=pl.BlockSpec((tm, n), lambda i: (i, 0)),
        out_shape=jax.ShapeDtypeStruct((m, n), F32),
        compiler_params=_cparams("parallel"),
        name="norm_matmul",
    )(x, g.reshape(1, d), w_bf16)


def _gated_out_kernel(o_ref, gate_ref, w_ref, x_ref, gn_ref, y_ref, *, head_norm):
    o = o_ref[...]
    if head_norm:
        parts = []
        for h in range(o.shape[1] // LANES):
            parts.append(_rms_rows(o[:, h * LANES:(h + 1) * LANES], gn_ref[...]))
        o = jnp.concatenate(parts, axis=1)
    gate = gate_ref[...]
    z = o * (gate * _sigmoid(gate))
    y_ref[...] = x_ref[...] + _dot(z.astype(BF16), w_ref[...])


def gated_out(o, gate_arr, gate_col, w_bf16, x, tm, g_norm=None):
    m, w = o.shape
    d = x.shape[1]
    head_norm = g_norm is not None
    gn = (g_norm if head_norm else jnp.ones((LANES,), F32)).reshape(1, LANES)
    return pl.pallas_call(
        functools.partial(_gated_out_kernel, head_norm=head_norm),
        grid=(m // tm,),
        in_specs=[pl.BlockSpec((tm, w), lambda i: (i, 0)),
                  pl.BlockSpec((tm, w), lambda i: (i, gate_col)),
                  pl.BlockSpec((w, d), lambda i: (0, 0)),
                  pl.BlockSpec((tm, d), lambda i: (i, 0)),
                  pl.BlockSpec((1, LANES), lambda i: (0, 0))],
        out_specs=pl.BlockSpec((tm, d), lambda i: (i, 0)),
        out_shape=jax.ShapeDtypeStruct((m, d), F32),
        compiler_params=_cparams("parallel"),
        name="gated_out",
    )(o, gate_arr, w_bf16, x, gn)


def _rope_lanes(x, c, s1, s2):
    n = x.shape[1]
    return x * c + pltpu.roll(x, n - A_HALF, 1) * s1 + pltpu.roll(x, A_HALF, 1) * s2


def _head_norm_blocks(x, g):
    parts = []
    for h in range(A_HEADS):
        blk = x[:, h * LANES:(h + 1) * LANES]
        ss = jnp.sum(blk * blk, axis=-1, keepdims=True)
        parts.append(blk * lax.rsqrt(ss * (1.0 / A_QK) + EPS) * g[:, h * LANES:(h + 1) * LANES])
    return jnp.concatenate(parts, axis=1)


def _mla_proj_kernel(x_ref, gn_ref, win_ref, gql_ref, gkv_ref, wuq_ref, gq_ref, wuk_ref, wuvt_ref, gk_ref,
                     c_ref, s1_ref, s2_ref,
                     q_ref, k_ref, vt_ref, gate_ref, ckv_ref, kpe_ref):
    h = _rms_rows(x_ref[...], gn_ref[...])
    p = _dot(h.astype(BF16), win_ref[...])
    gate_ref[...] = p[:, :A_WIDTH]
    cq = p[:, A_WIDTH:A_WIDTH + A_Q_LORA]
    ckv = p[:, A_WIDTH + A_Q_LORA:A_WIDTH + A_Q_LORA + A_KV_LORA]
    kpe = p[:, A_WIDTH + A_Q_LORA + A_KV_LORA:]
    c, s1, s2 = c_ref[...], s1_ref[...], s2_ref[...]
    c8 = jnp.concatenate([c] * A_HEADS, axis=1)
    s18 = jnp.concatenate([s1] * A_HEADS, axis=1)
    s28 = jnp.concatenate([s2] * A_HEADS, axis=1)

    q = _dot(_rms_rows(cq, gql_ref[...]).astype(BF16), wuq_ref[...])
    q = _head_norm_blocks(_rope_lanes(q, c8, s18, s28), gq_ref[...])
    q_ref[...] = q.astype(q_ref.dtype)

    ckvn = _rms_rows(ckv, gkv_ref[...])
    ckv_ref[...] = ckvn
    kper = _rope_lanes(kpe, c, s1, s2)
    kpe_ref[...] = kper
    cb = ckvn.astype(BF16)
    kn = _dot(cb, wuk_ref[...]) + jnp.concatenate([kper] * A_HEADS, axis=1)
    k_ref[...] = _head_norm_blocks(kn, gk_ref[...]).astype(k_ref.dtype)
    vt_ref[...] = _dot_nt(wuvt_ref[...], cb).astype(vt_ref.dtype)


def mla_proj(x, g_norm, wts, rope_tabs, rows_per_seq, tm, qkv_dtype):
    m, d = x.shape
    c, s1, s2 = rope_tabs
    nblk = rows_per_seq // tm
    full = lambda a: pl.BlockSpec(a.shape, lambda i: (0, 0))
    tab = pl.BlockSpec((tm, LANES), lambda i: (i % nblk, 0))
    row = lambda n: pl.BlockSpec((tm, n), lambda i: (i, 0))
    return pl.pallas_call(
        _mla_proj_kernel,
        grid=(m // tm,),
        in_specs=[row(d), pl.BlockSpec((1, d), lambda i: (0, 0)), full(wts["w_in"]),
                  full(wts["g_ql"]), full(wts["g_kv"]), full(wts["w_uq"]), full(wts["g_q"]),
                  full(wts["w_uk"]), full(wts["w_uvt"]), full(wts["g_k"]), tab, tab, tab],
        out_specs=[row(A_PADW), row(A_PADW), pl.BlockSpec((A_WIDTH, tm), lambda i: (0, i)),
                   row(A_WIDTH), row(A_KV_LORA), row(LANES)],
        out_shape=[jax.ShapeDtypeStruct((m, A_PADW), qkv_dtype),
                   jax.ShapeDtypeStruct((m, A_PADW), qkv_dtype),
                   jax.ShapeDtypeStruct((A_WIDTH, m), qkv_dtype),
                   jax.ShapeDtypeStruct((m, A_WIDTH), F32),
                   jax.ShapeDtypeStruct((m, A_KV_LORA), F32),
                   jax.ShapeDtypeStruct((m, LANES), F32)],
        compiler_params=_cparams("parallel"),
        name="mla_proj",
    )(x, g_norm.reshape(1, d), wts["w_in"], wts["g_ql"], wts["g_kv"], wts["w_uq"], wts["g_q"],
      wts["w_uk"], wts["w_uvt"], wts["g_k"], c, s1, s2)


def _pad_heads(a, used):
    pad = [(0, 0)] * (a.ndim - 1) + [(0, LANES - used)]
    a = jnp.pad(a, pad)
    return a.reshape(a.shape[:-2] + (a.shape[-2] * LANES,))


def _mla_weights(w_in, g_ql, g_kv, w_uq, w_ukv, g_q, g_k):
    cq, ckv, kpe, gate = jnp.split(w_in, [A_Q_LORA, A_Q_LORA + A_KV_LORA, A_Q_LORA + A_KV_LORA + A_ROPE], axis=1)
    kpe_blk = jnp.pad(kpe, ((0, 0), (A_NOPE, LANES - A_QK)))
    scale = A_QK ** -0.5
    ukv = w_ukv.reshape(A_KV_LORA, A_HEADS, A_NOPE + A_V)
    uk, uv = ukv[:, :, :A_NOPE], ukv[:, :, A_NOPE:]
    ukt_pad = _pad_heads(uk, A_NOPE).T
    ukt_hi = ukt_pad.astype(BF16)
    return {
        "w_in": jnp.concatenate([gate, cq, ckv, kpe_blk], axis=1).astype(BF16),
        "g_ql": g_ql.reshape(1, -1), "g_kv": g_kv.reshape(1, -1),
        "w_uq": _pad_heads(w_uq.reshape(A_Q_LORA, A_HEADS, A_QK), A_QK).astype(BF16),
        "g_q": _pad_heads(jnp.broadcast_to(g_q * scale, (A_HEADS, A_QK)), A_QK).reshape(1, -1),
        "g_k": _pad_heads(jnp.broadcast_to(g_k, (A_HEADS, A_QK)), A_QK).reshape(1, -1),
        "w_uk": _pad_heads(uk, A_NOPE).astype(BF16),
        "w_uv": uv.reshape(A_KV_LORA, A_WIDTH).astype(BF16),
        "w_uvt": uv.reshape(A_KV_LORA, A_WIDTH).T.astype(BF16),
        "ukt": uk.reshape(A_KV_LORA, A_HEADS * A_NOPE).T.astype(BF16),
        "ukt_hi": ukt_hi,
        "ukt_lo": (ukt_pad - ukt_hi.astype(F32)).astype(BF16),
    }


def _rope_tables(pos):
    inv = 1.0 / (ROPE_THETA ** (jnp.arange(0, A_ROPE, 2, dtype=F32) / A_ROPE))
    ang = pos.astype(F32)[:, None] * inv[None, :]
    cos, sin = jnp.cos(ang), jnp.sin(ang)
    n = pos.shape[0]
    z = lambda w: jnp.zeros((n, w), F32)
    c = jnp.concatenate([jnp.ones((n, A_NOPE), F32), cos, cos, z(LANES - A_QK)], axis=1)
    s1 = jnp.concatenate([z(A_NOPE), -sin, z(A_HALF), z(LANES - A_QK)], axis=1)
    s2 = jnp.concatenate([z(A_NOPE + A_HALF), sin, z(LANES - A_QK)], axis=1)
    return c, s1, s2


def _flash_kernel(qi_ref, kj_ref, q_ref, k_ref, vt_ref, o_ref, m_scr, l_scr, acc_scr):
    p_id = pl.program_id(1)
    qi = qi_ref[p_id]
    kj = kj_ref[p_id]
    tq, tk = q_ref.shape[0], k_ref.shape[0]

    @pl.when(kj == 0)
    def _():
        m_scr[...] = jnp.full(m_scr.shape, NEG_BIG, F32)
        l_scr[...] = jnp.zeros(l_scr.shape, F32)
        acc_scr[...] = jnp.zeros(acc_scr.shape, F32)

    def step(masked):
        if masked:
            key = lax.broadcasted_iota(jnp.int32, (tk, tq), 0)
            qry = lax.broadcasted_iota(jnp.int32, (tk, tq), 1)
            keep = key <= qry
        def scores(h):
            return _dot_nt(k_ref[:, h * LANES:(h + 1) * LANES], q_ref[:, h * LANES:(h + 1) * LANES])

        st_next = scores(0)
        for h in range(A_HEADS):
            st = st_next
            if h + 1 < A_HEADS:
                st_next = scores(h + 1)
            if masked:
                st = jnp.where(keep, st, NEG_BIG)
            m_prev = m_scr[h]
            m_new = jnp.maximum(m_prev, jnp.max(st, axis=0, keepdims=True))
            p = jnp.exp(st - m_new)
            corr = jnp.exp(m_prev - m_new)
            l_scr[h] = l_scr[h] * corr + jnp.sum(p, axis=0, keepdims=True)
            rows = slice(h * A_V, (h + 1) * A_V)
            acc_scr[rows, :] = acc_scr[rows, :] * corr + _dot(vt_ref[rows, :], p.astype(BF16))
            m_scr[h] = m_new

    @pl.when(kj < qi)
    def _():
        step(False)

    @pl.when(kj == qi)
    def _():
        step(True)
        ot = jnp.concatenate([acc_scr[h * A_V:(h + 1) * A_V, :] / l_scr[h] for h in range(A_HEADS)], axis=0)
        o_ref[...] = ot.T


def flash_prompt(q, k, vt, batch, seq, tq):
    nq = seq // tq
    pairs = [(i, j) for i in range(nq) for j in range(i + 1)]
    qi = jnp.asarray([p[0] for p in pairs], jnp.int32)
    kj = jnp.asarray([p[1] for p in pairs], jnp.int32)
    grid_spec = pltpu.PrefetchScalarGridSpec(
        num_scalar_prefetch=2,
        grid=(batch, len(pairs)),
        in_specs=[pl.BlockSpec((tq, A_PADW), lambda b, p, qi, kj: (b * nq + qi[p], 0)),
                  pl.BlockSpec((tq, A_PADW), lambda b, p, qi, kj: (b * nq + kj[p], 0)),
                  pl.BlockSpec((A_WIDTH, tq), lambda b, p, qi, kj: (0, b * nq + kj[p]))],
        out_specs=pl.BlockSpec((tq, A_WIDTH), lambda b, p, qi, kj: (b * nq + qi[p], 0)),
        scratch_shapes=[pltpu.VMEM((A_HEADS, 1, tq), F32), pltpu.VMEM((A_HEADS, 1, tq), F32),
                        pltpu.VMEM((A_WIDTH, tq), F32)])
    return pl.pallas_call(
        _flash_kernel, grid_spec=grid_spec,
        out_shape=jax.ShapeDtypeStruct((batch * seq, A_WIDTH), F32),
        compiler_params=_cparams("parallel", "arbitrary"),
        name="flash_prompt",
    )(qi, kj, q, k, vt)


def _absorb_query_kernel(q_ref, gk_ref, ukhi_ref, uklo_ref, qg_ref, qabs_ref):
    qg = q_ref[...] * gk_ref[...]
    qg_ref[...] = qg
    for h in range(A_HEADS):
        blk = qg[:, h * LANES:(h + 1) * LANES]
        q_hi = blk.astype(BF16)
        q_lo = (blk - q_hi.astype(F32)).astype(BF16)
        w_hi = ukhi_ref[h * LANES:(h + 1) * LANES, :]
        qabs_ref[h] = _dot(q_hi, w_hi) + _dot(q_hi, uklo_ref[h * LANES:(h + 1) * LANES, :]) + _dot(q_lo, w_hi)


def absorb_query(q_s, wts):
    db = q_s.shape[0]
    whole = lambda a: pl.BlockSpec(a.shape, lambda i: (0,) * a.ndim)
    return pl.pallas_call(
        _absorb_query_kernel,
        grid=(1,),
        in_specs=[whole(q_s), whole(wts["g_k"]), whole(wts["ukt_hi"]), whole(wts["ukt_lo"])],
        out_specs=[pl.BlockSpec((db, A_PADW), lambda i: (0, 0)),
                   pl.BlockSpec((A_HEADS, db, A_KV_LORA), lambda i: (0, 0, 0))],
        out_shape=[jax.ShapeDtypeStruct((db, A_PADW), F32),
                   jax.ShapeDtypeStruct((A_HEADS, db, A_KV_LORA), F32)],
        compiler_params=_cparams("arbitrary"),
        name="absorb_query",
    )(q_s, wts["g_k"], wts["ukt_hi"], wts["ukt_lo"])


def _sample_attn_kernel(pt_ref, qabs_ref, qpe_ref, ukt_ref, wuv_ref, cnew_ref, knew_ref, *rest,
                        n_pages, pages_per_block):
    ckv_refs, kpe_refs = rest[:n_pages], rest[n_pages:2 * n_pages]
    o_ref = rest[2 * n_pages]
    lhs_scr, cb_scr, s_scr = rest[2 * n_pages + 1:]
    n_nope = A_HEADS * A_NOPE
    n_past = n_pages * PAGE_SIZE

    lhs_scr[:n_nope, :] = ukt_ref[...]
    lhs_scr[n_nope:, :] = qabs_ref[...]

    def score(c, kpe_t, start):
        n = c.shape[0]
        cb = c.astype(BF16)
        cb_scr[start:start + n, :] = cb
        a = _dot_nt(lhs_scr[...], cb)
        kn = a[:n_nope]
        ss = jnp.sum((kn * kn).reshape(A_HEADS, A_NOPE, n), axis=1)
        s_pe = _dot(qpe_ref[...], kpe_t.astype(BF16))[:A_HEADS]
        pe_ss = jnp.sum(kpe_t * kpe_t, axis=0, keepdims=True)
        rs = lax.rsqrt((ss + pe_ss) * (1.0 / A_QK) + EPS)
        return (a[n_nope:n_nope + A_HEADS] + s_pe) * rs

    for blk in range(0, n_pages, pages_per_block):
        c = jnp.concatenate([r[...] for r in ckv_refs[blk:blk + pages_per_block]], axis=0)
        kpe_t = jnp.concatenate([r[...] for r in kpe_refs[blk:blk + pages_per_block]], axis=1)
        start = blk * PAGE_SIZE
        s_scr[:, start:start + c.shape[0]] = score(c, kpe_t, start)
    s_new = score(jnp.broadcast_to(cnew_ref[...], (LANES, A_KV_LORA)),
                  jnp.broadcast_to(knew_ref[...], (A_ROPE, LANES)), n_past)
    s_scr[:, n_past:] = jnp.where(lax.broadcasted_iota(jnp.int32, s_new.shape, 1) < 1, s_new, NEG_BIG)

    s = s_scr[...]
    p = jnp.exp(s - jnp.max(s, axis=-1, keepdims=True))
    pb = p.astype(BF16)
    n_tok = pb.shape[1]
    cuts = [(n_tok * i // PV_SPLIT) // LANES * LANES for i in range(PV_SPLIT)] + [n_tok]
    parts = [_dot(pb[:, lo:hi], cb_scr[lo:hi, :]) for lo, hi in zip(cuts[:-1], cuts[1:])]
    lat = sum(parts[1:], parts[0]) / jnp.sum(p, axis=-1, keepdims=True)
    full = _dot(lat.astype(BF16), wuv_ref[...])
    lane = lax.broadcasted_iota(jnp.int32, full.shape, 1)
    row = lax.broadcasted_iota(jnp.int32, full.shape, 0)
    sel = (lane >= row * A_V) & (lane < (row + 1) * A_V)
    o_ref[...] = jnp.sum(jnp.where(sel, full, 0.0), axis=0, keepdims=True)


def _page_index(b, pt, *, i, la, n_pages):
    return (la, pt[b * n_pages + i], 0, 0)


def mla_sample_attention(q_s, ckv_new, kpe_new, cache_ckv, cache_kpe_t, la, page_table, wts, pages_per_block):
    db, n_pages = page_table.shape
    n_tok = n_pages * PAGE_SIZE + LANES
    pidx = functools.partial(_page_index, la=la, n_pages=n_pages)
    row3 = lambda r, n: pl.BlockSpec((None, r, n), lambda b, pt: (b, 0, 0))
    full = lambda a: pl.BlockSpec(a.shape, lambda b, pt: (0, 0))
    q_rows = 2 * SUBLANES
    lhs_rows = A_HEADS * A_NOPE + q_rows
    qg, qabs = absorb_query(q_s, wts)
    pad_rows = lambda a: jnp.pad(a, ((0, 0), (0, q_rows - A_HEADS), (0, 0))).astype(BF16)
    qabs = pad_rows(qabs.transpose(1, 0, 2))
    qpe = pad_rows(qg.reshape(db, A_HEADS, LANES)[:, :, A_NOPE:A_QK])
    grid_spec = pltpu.PrefetchScalarGridSpec(
        num_scalar_prefetch=1,
        grid=(db,),
        in_specs=[row3(q_rows, A_KV_LORA), row3(q_rows, A_ROPE), full(wts["ukt"]),
                  full(wts["w_uv"]), row3(1, A_KV_LORA), row3(A_ROPE, 1)]
        + [pl.BlockSpec((None, None, PAGE_SIZE, A_KV_LORA), functools.partial(pidx, i=i)) for i in range(n_pages)]
        + [pl.BlockSpec((None, None, A_ROPE, PAGE_SIZE), functools.partial(pidx, i=i)) for i in range(n_pages)],
        out_specs=pl.BlockSpec((None, 1, A_WIDTH), lambda b, pt: (b, 0, 0)),
        scratch_shapes=[pltpu.VMEM((lhs_rows, A_KV_LORA), BF16),
                        pltpu.VMEM((n_tok, A_KV_LORA), BF16), pltpu.VMEM((A_HEADS, n_tok), F32)])
    out = pl.pallas_call(
        functools.partial(_sample_attn_kernel, n_pages=n_pages, pages_per_block=pages_per_block),
        grid_spec=grid_spec,
        out_shape=jax.ShapeDtypeStruct((db, 1, A_WIDTH), F32),
        compiler_params=_cparams("parallel"),
        name="mla_sample_attention",
    )(page_table.reshape(-1), qabs, qpe, wts["ukt"],
      wts["w_uv"], ckv_new.reshape(db, 1, A_KV_LORA), kpe_new.reshape(db, A_ROPE, 1),
      *([cache_ckv] * n_pages), *([cache_kpe_t] * n_pages))
    return out.reshape(db, A_WIDTH)


def _hgrn_lower_bound(lb_all, li):
    e = jnp.exp(lb_all - jnp.max(lb_all, axis=0, keepdims=True))
    smx = e / jnp.sum(e, axis=0, keepdims=True)
    return jnp.sum(smx[1:li + 1], axis=0, keepdims=True) if li > 0 else jnp.zeros_like(smx[:1])


def _hgrn_gates(q, f, lb):
    g = jnp.log(lb + (1.0 - lb) * _sigmoid(f))
    k = (1.0 - lb) * _sigmoid(-f)
    qf = q * _sigmoid(q) * (B_DK ** -0.5)
    return qf, k, g


def _cumsum_rows(x):
    n = x.shape[0]
    row = lax.broadcasted_iota(jnp.int32, x.shape, 0)
    d = 1
    while d < n:
        x = x + jnp.where(row >= d, pltpu.roll(x, d, 0), 0.0)
        d *= 2
    return x


def _bcast_row_in_groups(x, j):
    n, w = x.shape
    x3 = x.reshape(n // SUBLANES, SUBLANES, w)
    return jnp.broadcast_to(x3[:, j:j + 1, :], x3.shape).reshape(n, w)


def _intra_chunk_att(qf, k, b):
    c = qf.shape[0]
    row = lax.broadcasted_iota(jnp.int32, (c, 1), 0)
    tt = lax.broadcasted_iota(jnp.int32, (c, c), 0)
    ss = lax.broadcasted_iota(jnp.int32, (c, c), 1)
    att = jnp.where(tt == ss, jnp.sum(qf * k, axis=-1, keepdims=True), 0.0)
    half = c // 2
    while half >= 1:
        blk = 2 * half
        if blk >= SUBLANES:
            bref = jnp.concatenate(
                [jnp.broadcast_to(b[m:m + 1], (blk, b.shape[1])) for m in range(half, c, blk)], axis=0)
        else:
            rm = row % SUBLANES
            bref = _bcast_row_in_groups(b, SUBLANES - half)
            for m in range(SUBLANES - half - blk, 0, -blk):
                bref = jnp.where(rm < m + half, _bcast_row_in_groups(b, m), bref)
        upper = (row % blk) >= half
        e = jnp.exp(jnp.where(upper, b - bref, bref - b))
        qt = jnp.where(upper, qf * e, 0.0)
        kt = jnp.where(upper, 0.0, k * e)
        p = _dot_nt(qt.astype(BF16), kt.astype(BF16))
        if blk < c:
            p = jnp.where((tt // blk) == (ss // blk), p, 0.0)
        att = att + p
        half //= 2
    return att


def _hgrn_prompt_kernel(q_ref, f_ref, v_ref, lb_ref, o_ref, st_ref, state_scr, *, li):
    t = pl.program_id(2)

    @pl.when(t == 0)
    def _():
        state_scr[...] = jnp.zeros(state_scr.shape, F32)

    lb = _hgrn_lower_bound(lb_ref[...], li)
    for ci in range(q_ref.shape[0] // B_CHUNK):
        sl = slice(ci * B_CHUNK, (ci + 1) * B_CHUNK)
        qf, k, g = _hgrn_gates(q_ref[sl, :], f_ref[sl, :], lb)
        v = v_ref[sl, :]
        b = _cumsum_rows(g)
        st = state_scr[...]
        vb = v.astype(BF16)
        inter = _dot_nt((qf * jnp.exp(b)).astype(BF16), st.astype(BF16))
        att = _intra_chunk_att(qf, k, b)
        o_ref[sl, :] = inter + _dot(att.astype(BF16), vb)
        blast = b[B_CHUNK - 1:B_CHUNK]
        kd = (k * jnp.exp(blast - b)).astype(BF16)
        upd = lax.dot_general(vb, kd, (((0,), (0,)), ((), ())), preferred_element_type=F32)
        state_scr[...] = jnp.exp(blast) * st + upd

    @pl.when(t == pl.num_programs(2) - 1)
    def _():
        st_ref[...] = state_scr[...].T


def hgrn_prompt(proj, lower_bounds, li, batch, seq, tblk):
    nt = seq // tblk
    col = lambda off: pl.BlockSpec((tblk, LANES), lambda b, h, t: (b * nt + t, off + h))
    return pl.pallas_call(
        functools.partial(_hgrn_prompt_kernel, li=li),
        grid=(batch, B_HEADS, nt),
        in_specs=[col(0), col(B_HEADS), col(2 * B_HEADS),
                  pl.BlockSpec((lower_bounds.shape[0], LANES), lambda b, h, t: (0, h))],
        out_specs=[pl.BlockSpec((tblk, LANES), lambda b, h, t: (b * nt + t, h)),
                   pl.BlockSpec((None, None, B_DK, B_DV), lambda b, h, t: (b, h, 0, 0))],
        out_shape=[jax.ShapeDtypeStruct((batch * seq, B_HEADS * B_DV), F32),
                   jax.ShapeDtypeStruct((batch, B_HEADS, B_DK, B_DV), F32)],
        scratch_shapes=[pltpu.VMEM((B_DV, B_DK), F32)],
        compiler_params=_cparams("parallel", "parallel", "arbitrary"),
        name="hgrn_prompt",
    )(proj, proj, proj, lower_bounds)


def _hgrn_sample_gates_kernel(p_ref, lb_ref, qe_ref, eg_ref, k_ref, av_ref, *, li):
    w = B_HEADS * B_DK
    lb = _hgrn_lower_bound(lb_ref[...], li)
    qf, k, g = _hgrn_gates(p_ref[:, :w], p_ref[:, w:2 * w], lb)
    v = p_ref[:, 2 * w:3 * w]
    eg = jnp.exp(g)
    qe_ref[...] = qf * eg
    eg_ref[...] = eg
    k_ref[...] = k
    qk = qf * k
    av_ref[...] = jnp.concatenate(
        [jnp.sum(qk[:, h * LANES:(h + 1) * LANES], axis=-1, keepdims=True) * v[:, h * LANES:(h + 1) * LANES]
         for h in range(B_HEADS)], axis=1)


def _hgrn_sample_state_kernel(st_ref, qe_ref, egt_ref, kt_ref, v_ref, av_ref, o_ref, ns_ref):
    for h in range(B_HEADS):
        st = st_ref[h]
        ns_ref[h] = egt_ref[:, h:h + 1] * st + kt_ref[:, h:h + 1] * v_ref[h:h + 1, :]
        qe = jnp.broadcast_to(qe_ref[h:h + 1, :], (2 * SUBLANES, B_DK)).astype(BF16)
        o_ref[h:h + 1, :] = _dot(qe, st.astype(BF16))[:1] + av_ref[h:h + 1, :]


def hgrn_sample(proj_s, lower_bounds, li, state):
    db = proj_s.shape[0]
    w = B_HEADS * B_DK
    whole = lambda a: pl.BlockSpec(a.shape, lambda i: (0,) * a.ndim)
    qe, eg, k, av = pl.pallas_call(
        functools.partial(_hgrn_sample_gates_kernel, li=li),
        grid=(1,),
        in_specs=[whole(proj_s), whole(lower_bounds)],
        out_specs=[pl.BlockSpec((db, w), lambda i: (0, 0))] * 4,
        out_shape=[jax.ShapeDtypeStruct((db, w), F32)] * 4,
        compiler_params=_cparams("arbitrary"),
        name="hgrn_sample_gates",
    )(proj_s, lower_bounds)
    heads = lambda a: a.reshape(db, B_HEADS, B_DK)
    cols = lambda a: heads(a).transpose(0, 2, 1)
    hrow = pl.BlockSpec((None, B_HEADS, B_DK), lambda b: (b, 0, 0))
    hcol = pl.BlockSpec((None, B_DK, B_HEADS), lambda b: (b, 0, 0))
    stspec = pl.BlockSpec((None, B_HEADS, B_DK, B_DV), lambda b: (b, 0, 0, 0))
    o, new_state = pl.pallas_call(
        _hgrn_sample_state_kernel,
        grid=(db,),
        in_specs=[stspec, hrow, hcol, hcol, hrow, hrow],
        out_specs=[hrow, stspec],
        out_shape=[jax.ShapeDtypeStruct((db, B_HEADS, B_DV), F32),
                   jax.ShapeDtypeStruct(state.shape, F32)],
        compiler_params=_cparams("parallel"),
        name="hgrn_sample_state",
    )(state, heads(qe), cols(eg), cols(k), heads(proj_s[:, 2 * w:3 * w]), heads(av))
    return o.reshape(db, w), new_state


def _rglru_gates(y, wax_ref, ba, bx, lam):
    yb = y.astype(BF16)
    rs, is_ = [], []
    for n in range(C_HEADS):
        ax = _dot(yb[:, n * C_BW:(n + 1) * C_BW], wax_ref[n])
        rs.append(ax[:, :C_BW])
        is_.append(ax[:, C_BW:])
    r = _sigmoid(jnp.concatenate(rs, axis=1) + ba)
    i = _sigmoid(jnp.concatenate(is_, axis=1) + bx)
    softplus = jnp.maximum(-lam, 0.0) + jnp.log1p(jnp.exp(-jnp.abs(lam)))
    log_a = -C_GATE_C * r * softplus
    a = jnp.exp(log_a)
    th = jnp.tanh(log_a)
    u = jnp.sqrt(-2.0 * th / (1.0 - th)) * (i * y)
    return a, u


def _rglru_prompt_kernel(xb_ref, cw_ref, cb_ref, wax_ref, ba_ref, bx_ref, lam_ref,
                         hs_ref, tail_ref, hl_ref, prev_scr, h_scr, a_scr, u_scr):
    t = pl.program_id(1)
    tb = xb_ref.shape[0]

    @pl.when(t == 0)
    def _():
        prev_scr[...] = jnp.zeros(prev_scr.shape, F32)
        h_scr[...] = jnp.zeros(h_scr.shape, F32)

    xb = xb_ref[...]
    prev = prev_scr[...]
    row = lax.broadcasted_iota(jnp.int32, (tb, 1), 0)
    y = cb_ref[...] + cw_ref[C_CONV - 1:C_CONV, :] * xb
    for d in range(1, C_CONV):
        sh = pltpu.roll(xb, d, 0)
        for r in range(d):
            sh = jnp.where(row == r, prev[SUBLANES - d + r:SUBLANES - d + r + 1, :], sh)
        y = y + cw_ref[C_CONV - 1 - d:C_CONV - d, :] * sh
    prev_scr[...] = xb[tb - SUBLANES:, :]
    a, u = _rglru_gates(y, wax_ref, ba_ref[...], bx_ref[...], lam_ref[...])
    a_scr[...] = a
    u_scr[...] = u

    def body(i, h):
        base = pl.multiple_of(i * SUBLANES, SUBLANES)
        for r in range(SUBLANES):
            h = a_scr[pl.ds(base + r, 1), :] * h + u_scr[pl.ds(base + r, 1), :]
            hs_ref[pl.ds(base + r, 1), :] = h
        return h

    h = lax.fori_loop(0, tb // SUBLANES, body, h_scr[...])
    h_scr[...] = h

    @pl.when(t == pl.num_programs(1) - 1)
    def _():
        tail_ref[...] = xb[tb - SUBLANES:, :]
        hl_ref[...] = h


def _rglru_weights(conv_w, conv_b, w_a, b_a, w_x, b_x, lam):
    r1 = lambda a: a.reshape(1, -1)
    return {"cw": conv_w, "cb": r1(conv_b), "wax": jnp.concatenate([w_a, w_x], axis=2).astype(BF16),
            "ba": r1(b_a), "bx": r1(b_x), "lam": r1(lam)}


def rglru_prompt(proj, wts, batch, seq, tblk):
    nt = seq // tblk
    w = C_WIDTH
    full = lambda a: pl.BlockSpec(a.shape, lambda b, t: (0,) * a.ndim)
    names = ("cw", "cb", "wax", "ba", "bx", "lam")
    return pl.pallas_call(
        _rglru_prompt_kernel,
        grid=(batch, nt),
        in_specs=[pl.BlockSpec((tblk, w), lambda b, t: (b * nt + t, 0))] + [full(wts[n]) for n in names],
        out_specs=[pl.BlockSpec((tblk, w), lambda b, t: (b * nt + t, 0)),
                   pl.BlockSpec((None, SUBLANES, w), lambda b, t: (b, 0, 0)),
                   pl.BlockSpec((None, 1, w), lambda b, t: (b, 0, 0))],
        out_shape=[jax.ShapeDtypeStruct((batch * seq, w), F32),
                   jax.ShapeDtypeStruct((batch, SUBLANES, w), F32),
                   jax.ShapeDtypeStruct((batch, 1, w), F32)],
        scratch_shapes=[pltpu.VMEM((SUBLANES, w), F32), pltpu.VMEM((1, w), F32),
                        pltpu.VMEM((tblk, w), F32), pltpu.VMEM((tblk, w), F32)],
        compiler_params=_cparams("parallel", "arbitrary"),
        name="rglru_prompt",
    )(proj, *[wts[n] for n in names])


def _rglru_sample_kernel(xb_ref, buf_ref, h0_ref, cw_ref, cb_ref, wax_ref, ba_ref, bx_ref, lam_ref,
                         hs_ref, nbuf_ref):
    xb = xb_ref[...]
    y = cb_ref[...] + cw_ref[C_CONV - 1:C_CONV, :] * xb
    for j in range(C_CONV - 1):
        y = y + cw_ref[j:j + 1, :] * buf_ref[j]
    a, u = _rglru_gates(y, wax_ref, ba_ref[...], bx_ref[...], lam_ref[...])
    hs_ref[...] = a * h0_ref[...] + u
    for j in range(C_CONV - 2):
        nbuf_ref[j] = buf_ref[j + 1]
    nbuf_ref[C_CONV - 2] = xb


def rglru_sample(proj_s, buf_t, h0, wts):
    db = proj_s.shape[0]
    w = C_WIDTH
    whole = lambda a: pl.BlockSpec(a.shape, lambda i: (0,) * a.ndim)
    names = ("cw", "cb", "wax", "ba", "bx", "lam")
    return pl.pallas_call(
        _rglru_sample_kernel,
        grid=(1,),
        in_specs=[pl.BlockSpec((db, w), lambda i: (0, 0)), whole(buf_t), whole(h0)] + [whole(wts[n]) for n in names],
        out_specs=[pl.BlockSpec((db, w), lambda i: (0, 0)), whole(buf_t)],
        out_shape=[jax.ShapeDtypeStruct((db, w), F32), jax.ShapeDtypeStruct(buf_t.shape, F32)],
        compiler_params=_cparams("arbitrary"),
        name="rglru_sample",
    )(proj_s, buf_t, h0, *[wts[n] for n in names])


TM_PROMPT = 512
TQ_FLASH = 512
T_HGRN = 256
T_RGLRU = 256
PAGES_PER_BLOCK = 8
PV_SPLIT = 4


def kernel(x_prompt, x_sample, cache_ckv, cache_kpe, page_table, state_hgrn, state_conv, state_lru, norm_g, mla_w_in, mla_g_q_lora, mla_g_kv, mla_w_uq, mla_w_ukv, mla_g_q, mla_g_k, mla_w_out, hgrn_w_in, hgrn_lower_bounds, hgrn_g_norm, hgrn_w_out, rglru_w_in, rglru_conv_w, rglru_conv_b, rglru_w_a, rglru_b_a, rglru_w_x, rglru_b_x, rglru_L, rglru_w_out):
    bsz, s_p, d = x_prompt.shape
    db, s_s, _ = x_sample.shape
    assert s_s == 1 and d == D_MODEL
    depth = norm_g.shape[0]
    past_len = page_table.shape[1] * PAGE_SIZE
    xp = x_prompt.reshape(bsz * s_p, d)
    xs = x_sample.reshape(db, d)
    tabs_p = _rope_tables(jnp.arange(s_p))
    tabs_s = _rope_tables(jnp.full((db,), past_len))
    cache_kpe_t = cache_kpe.transpose(0, 1, 3, 2)

    ckv_p, kpe_p, ckv_s, kpe_s = [], [], [], []
    hg_p, hg_s, cv_p, cv_s, lr_p, lr_s = [], [], [], [], [], []
    for li in range(depth):
        kind, j = li % N_MIXERS, li // N_MIXERS
        if kind == 0:
            wts = _mla_weights(mla_w_in[j], mla_g_q_lora[j], mla_g_kv[j], mla_w_uq[j], mla_w_ukv[j],
                               mla_g_q[j], mla_g_k[j])
            w_out = mla_w_out[j].astype(BF16)
            q, k, vt, gate, ckv, kpe_blk = mla_proj(xp, norm_g[li], wts, tabs_p, s_p, TM_PROMPT, BF16)
            o = flash_prompt(q, k, vt, bsz, s_p, TQ_FLASH)
            xp = gated_out(o, gate, 0, w_out, xp, TM_PROMPT)
            ckv_p.append(ckv.reshape(bsz, s_p, A_KV_LORA))
            kpe_p.append(kpe_blk[:, A_NOPE:A_QK].reshape(bsz, s_p, A_ROPE))

            q_s, _, _, gate_s, ckv_n, kpe_blk_s = mla_proj(xs, norm_g[li], wts, tabs_s, db, db, F32)
            kpe_n = kpe_blk_s[:, A_NOPE:A_QK]
            o_s = mla_sample_attention(q_s, ckv_n, kpe_n, cache_ckv, cache_kpe_t, j, page_table, wts,
                                       PAGES_PER_BLOCK)
            xs = gated_out(o_s, gate_s, 0, w_out, xs, db)
            ckv_s.append(ckv_n.reshape(db, 1, A_KV_LORA))
            kpe_s.append(kpe_n.reshape(db, 1, A_ROPE))
        elif kind == 1:
            w_in = hgrn_w_in[j].astype(BF16)
            w_out = hgrn_w_out[j].astype(BF16)
            proj = norm_matmul(xp, norm_g[li], w_in, TM_PROMPT // 2)
            o, st = hgrn_prompt(proj, hgrn_lower_bounds, li, bsz, s_p, T_HGRN)
            xp = gated_out(o, proj, 3, w_out, xp, TM_PROMPT, hgrn_g_norm[j])
            hg_p.append(st)

            proj_s = norm_matmul(xs, norm_g[li], w_in, db)
            o_s, st_s = hgrn_sample(proj_s, hgrn_lower_bounds, li, state_hgrn[j])
            xs = gated_out(o_s, proj_s, 3, w_out, xs, db, hgrn_g_norm[j])
            hg_s.append(st_s)
        else:
            w_in = rglru_w_in[j].astype(BF16)
            w_out = rglru_w_out[j].astype(BF16)
            wts = _rglru_weights(rglru_conv_w[j], rglru_conv_b[j], rglru_w_a[j], rglru_b_a[j],
                                 rglru_w_x[j], rglru_b_x[j], rglru_L[j])
            proj = norm_matmul(xp, norm_g[li], w_in, TM_PROMPT)
            hs, tail, hl = rglru_prompt(proj, wts, bsz, s_p, T_RGLRU)
            xp = gated_out(hs, proj, 1, w_out, xp, TM_PROMPT)
            cv_p.append(tail[:, SUBLANES - (C_CONV - 1):, :])
            lr_p.append(hl.reshape(bsz, C_WIDTH))

            proj_s = norm_matmul(xs, norm_g[li], w_in, db)
            hs_s, nbuf = rglru_sample(proj_s, state_conv[j].transpose(1, 0, 2), state_lru[j], wts)
            xs = gated_out(hs_s, proj_s, 1, w_out, xs, db)
            cv_s.append(nbuf.transpose(1, 0, 2))
            lr_s.append(hs_s)

    return (xp.reshape(bsz, s_p, d), xs.reshape(db, 1, d),
            jnp.stack(ckv_p), jnp.stack(kpe_p), jnp.stack(ckv_s), jnp.stack(kpe_s),
            jnp.stack(hg_p), jnp.stack(hg_s), jnp.stack(cv_p), jnp.stack(cv_s),
            jnp.stack(lr_p), jnp.stack(lr_s))
```

```python
import functools
import math

import jax
import jax.numpy as jnp
from jax import lax
from jax.experimental import pallas as pl
from jax.experimental.pallas import tpu as pltpu

F32 = jnp.float32
BF16 = jnp.bfloat16

LANES = 128
SUBLANES = 8
VMEM_LIMIT_BYTES = 56 * 1024 * 1024

D_MODEL = 1024
PAGE_SIZE = 128
N_MIXERS = 3
EPS = 1e-6
ROPE_THETA = 10000.0
NEG_BIG = -1e30

A_HEADS = 8
A_NOPE = 64
A_ROPE = 32
A_QK = A_NOPE + A_ROPE
A_V = 64
A_Q_LORA = 384
A_KV_LORA = 256
A_WIDTH = A_HEADS * A_V
A_HALF = A_ROPE // 2
A_PADW = A_HEADS * LANES

B_HEADS = 8
B_DK = 128
B_DV = 128
B_CHUNK = 64

C_WIDTH = 1024
C_HEADS = 8
C_BW = 128
C_CONV = 4
C_GATE_C = 8.0


def _cparams(*sem):
    return pltpu.CompilerParams(dimension_semantics=sem, vmem_limit_bytes=VMEM_LIMIT_BYTES)


def _sigmoid(x):
    return 1.0 / (1.0 + jnp.exp(-x))


def _rms_rows(x, g):
    ms = jnp.mean(x * x, axis=-1, keepdims=True)
    return x * lax.rsqrt(ms + EPS) * g


def _dot_nt(a, b):
    return lax.dot_general(a, b, (((1,), (1,)), ((), ())), preferred_element_type=F32)


def _dot(a, b):
    return jnp.dot(a, b, preferred_element_type=F32)


def _norm_matmul_kernel(x_ref, g_ref, w_ref, o_ref):
    h = _rms_rows(x_ref[...], g_ref[...])
    o_ref[...] = _dot(h.astype(BF16), w_ref[...])


def norm_matmul(x, g, w_bf16, tm):
    m, d = x.shape
    n = w_bf16.shape[1]
    return pl.pallas_call(
        _norm_matmul_kernel,
        grid=(m // tm,),
        in_specs=[pl.BlockSpec((tm, d), lambda i: (i, 0)),
                  pl.BlockSpec((1, d), lambda i: (0, 0)),
                  pl.BlockSpec((d, n), lambda i: (0, 0))],
        out_specs=pl.BlockSpec((tm, n), lambda i: (i, 0)),
        out_shape=jax.ShapeDtypeStruct((m, n), F32),
        compiler_params=_cparams("parallel"),
        name="norm_matmul",
    )(x, g.reshape(1, d), w_bf16)


def _gated_out_kernel(o_ref, gate_ref, w_ref, x_ref, gn_ref, y_ref, *, head_norm):
    o = o_ref[...]
    if head_norm:
        parts = []
        for h in range(o.shape[1] // LANES):
            parts.append(_rms_rows(o[:, h * LANES:(h + 1) * LANES], gn_ref[...]))
        o = jnp.concatenate(parts, axis=1)
    gate = gate_ref[...]
    z = o * (gate * _sigmoid(gate))
    y_ref[...] = x_ref[...] + _dot(z.astype(BF16), w_ref[...])


def gated_out(o, gate_arr, gate_col, w_bf16, x, tm, g_norm=None):
    m, w = o.shape
    d = x.shape[1]
    head_norm = g_norm is not None
    gn = (g_norm if head_norm else jnp.ones((LANES,), F32)).reshape(1, LANES)
    return pl.pallas_call(
        functools.partial(_gated_out_kernel, head_norm=head_norm),
        grid=(m // tm,),
        in_specs=[pl.BlockSpec((tm, w), lambda i: (i, 0)),
                  pl.BlockSpec((tm, w), lambda i: (i, gate_col)),
                  pl.BlockSpec((w, d), lambda i: (0, 0)),
                  pl.BlockSpec((tm, d), lambda i: (i, 0)),
                  pl.BlockSpec((1, LANES), lambda i: (0, 0))],
        out_specs=pl.BlockSpec((tm, d), lambda i: (i, 0)),
        out_shape=jax.ShapeDtypeStruct((m, d), F32),
        compiler_params=_cparams("parallel"),
        name="gated_out",
    )(o, gate_arr, w_bf16, x, gn)


def _rope_lanes(x, c, s1, s2):
    n = x.shape[1]
    return x * c + pltpu.roll(x, n - A_HALF, 1) * s1 + pltpu.roll(x, A_HALF, 1) * s2


def _head_norm_blocks(x, g):
    parts = []
    for h in range(A_HEADS):
        blk = x[:, h * LANES:(h + 1) * LANES]
        ss = jnp.sum(blk * blk, axis=-1, keepdims=True)
        parts.append(blk * lax.rsqrt(ss * (1.0 / A_QK) + EPS) * g[:, h * LANES:(h + 1) * LANES])
    return jnp.concatenate(parts, axis=1)


def _mla_proj_kernel(x_ref, gn_ref, win_ref, gql_ref, gkv_ref, wuq_ref, gq_ref, wuk_ref, wuvt_ref, gk_ref,
                     c_ref, s1_ref, s2_ref,
                     q_ref, k_ref, vt_ref, gate_ref, ckv_ref, kpe_ref):
    h = _rms_rows(x_ref[...], gn_ref[...])
    p = _dot(h.astype(BF16), win_ref[...])
    gate_ref[...] = p[:, :A_WIDTH]
    cq = p[:, A_WIDTH:A_WIDTH + A_Q_LORA]
    ckv = p[:, A_WIDTH + A_Q_LORA:A_WIDTH + A_Q_LORA + A_KV_LORA]
    kpe = p[:, A_WIDTH + A_Q_LORA + A_KV_LORA:]
    c, s1, s2 = c_ref[...], s1_ref[...], s2_ref[...]
    c8 = jnp.concatenate([c] * A_HEADS, axis=1)
    s18 = jnp.concatenate([s1] * A_HEADS, axis=1)
    s28 = jnp.concatenate([s2] * A_HEADS, axis=1)

    q = _dot(_rms_rows(cq, gql_ref[...]).astype(BF16), wuq_ref[...])
    q = _head_norm_blocks(_rope_lanes(q, c8, s18, s28), gq_ref[...])
    q_ref[...] = q.astype(q_ref.dtype)

    ckvn = _rms_rows(ckv, gkv_ref[...])
    ckv_ref[...] = ckvn
    kper = _rope_lanes(kpe, c, s1, s2)
    kpe_ref[...] = kper
    cb = ckvn.astype(BF16)
    kn = _dot(cb, wuk_ref[...]) + jnp.concatenate([kper] * A_HEADS, axis=1)
    k_ref[...] = _head_norm_blocks(kn, gk_ref[...]).astype(k_ref.dtype)
    vt_ref[...] = _dot_nt(wuvt_ref[...], cb).astype(vt_ref.dtype)


def mla_proj(x, g_norm, wts, rope_tabs, rows_per_seq, tm, qkv_dtype, q_gain):
    m, d = x.shape
    c, s1, s2 = rope_tabs
    nblk = rows_per_seq // tm
    full = lambda a: pl.BlockSpec(a.shape, lambda i: (0, 0))
    tab = pl.BlockSpec((tm, LANES), lambda i: (i % nblk, 0))
    row = lambda n: pl.BlockSpec((tm, n), lambda i: (i, 0))
    return pl.pallas_call(
        _mla_proj_kernel,
        grid=(m // tm,),
        in_specs=[row(d), pl.BlockSpec((1, d), lambda i: (0, 0)), full(wts["w_in"]),
                  full(wts["g_ql"]), full(wts["g_kv"]), full(wts["w_uq"]), full(wts[q_gain]),
                  full(wts["w_uk"]), full(wts["w_uvt"]), full(wts["g_k"]), tab, tab, tab],
        out_specs=[row(A_PADW), row(A_PADW), pl.BlockSpec((A_WIDTH, tm), lambda i: (0, i)),
                   row(A_WIDTH), row(A_KV_LORA), row(LANES)],
        out_shape=[jax.ShapeDtypeStruct((m, A_PADW), qkv_dtype),
                   jax.ShapeDtypeStruct((m, A_PADW), qkv_dtype),
                   jax.ShapeDtypeStruct((A_WIDTH, m), qkv_dtype),
                   jax.ShapeDtypeStruct((m, A_WIDTH), F32),
                   jax.ShapeDtypeStruct((m, A_KV_LORA), F32),
                   jax.ShapeDtypeStruct((m, LANES), F32)],
        compiler_params=_cparams("parallel"),
        name="mla_proj",
    )(x, g_norm.reshape(1, d), wts["w_in"], wts["g_ql"], wts["g_kv"], wts["w_uq"], wts[q_gain],
      wts["w_uk"], wts["w_uvt"], wts["g_k"], c, s1, s2)


def _pad_heads(a, used):
    pad = [(0, 0)] * (a.ndim - 1) + [(0, LANES - used)]
    a = jnp.pad(a, pad)
    return a.reshape(a.shape[:-2] + (a.shape[-2] * LANES,))


def _mla_weights(w_in, g_ql, g_kv, w_uq, w_ukv, g_q, g_k):
    cq, ckv, kpe, gate = jnp.split(w_in, [A_Q_LORA, A_Q_LORA + A_KV_LORA, A_Q_LORA + A_KV_LORA + A_ROPE], axis=1)
    kpe_blk = jnp.pad(kpe, ((0, 0), (A_NOPE, LANES - A_QK)))
    scale = A_QK ** -0.5
    ukv = w_ukv.reshape(A_KV_LORA, A_HEADS, A_NOPE + A_V)
    uk, uv = ukv[:, :, :A_NOPE], ukv[:, :, A_NOPE:]
    ukt_pad = _pad_heads(uk, A_NOPE).T
    ukt_hi = ukt_pad.astype(BF16)
    return {
        "w_in": jnp.concatenate([gate, cq, ckv, kpe_blk], axis=1).astype(BF16),
        "g_ql": g_ql.reshape(1, -1), "g_kv": g_kv.reshape(1, -1),
        "w_uq": _pad_heads(w_uq.reshape(A_Q_LORA, A_HEADS, A_QK), A_QK).astype(BF16),
        "g_q": _pad_heads(jnp.broadcast_to(g_q * scale, (A_HEADS, A_QK)), A_QK).reshape(1, -1),
        "g_q_base2": _pad_heads(jnp.broadcast_to(g_q * (scale * math.log2(math.e)), (A_HEADS, A_QK)),
                                A_QK).reshape(1, -1),
        "g_k": _pad_heads(jnp.broadcast_to(g_k, (A_HEADS, A_QK)), A_QK).reshape(1, -1),
        "w_uk": _pad_heads(uk, A_NOPE).astype(BF16),
        "w_uv": uv.reshape(A_KV_LORA, A_WIDTH).astype(BF16),
        "w_uvt": uv.reshape(A_KV_LORA, A_WIDTH).T.astype(BF16),
        "ukt": uk.reshape(A_KV_LORA, A_HEADS * A_NOPE).T.astype(BF16),
        "ukt_hi": ukt_hi,
        "ukt_lo": (ukt_pad - ukt_hi.astype(F32)).astype(BF16),
    }


def _rope_tables(pos):
    inv = 1.0 / (ROPE_THETA ** (jnp.arange(0, A_ROPE, 2, dtype=F32) / A_ROPE))
    ang = pos.astype(F32)[:, None] * inv[None, :]
    cos, sin = jnp.cos(ang), jnp.sin(ang)
    n = pos.shape[0]
    z = lambda w: jnp.zeros((n, w), F32)
    c = jnp.concatenate([jnp.ones((n, A_NOPE), F32), cos, cos, z(LANES - A_QK)], axis=1)
    s1 = jnp.concatenate([z(A_NOPE), -sin, z(A_HALF), z(LANES - A_QK)], axis=1)
    s2 = jnp.concatenate([z(A_NOPE + A_HALF), sin, z(LANES - A_QK)], axis=1)
    return c, s1, s2


def _flash_kernel(qi_ref, kj_ref, q_ref, k_ref, vt_ref, o_ref, m_scr, l_scr, acc_scr):
    p_id = pl.program_id(1)
    qi = qi_ref[p_id]
    kj = kj_ref[p_id]
    tq, tk = q_ref.shape[0], k_ref.shape[0]

    @pl.when(kj == 0)
    def _():
        m_scr[...] = jnp.full(m_scr.shape, NEG_BIG, F32)
        l_scr[...] = jnp.zeros(l_scr.shape, F32)
        acc_scr[...] = jnp.zeros(acc_scr.shape, F32)

    def step(masked):
        if masked:
            key = lax.broadcasted_iota(jnp.int32, (tk, tq), 0)
            qry = lax.broadcasted_iota(jnp.int32, (tk, tq), 1)
            keep = key <= qry
        def scores(h):
            return _dot_nt(k_ref[:, h * LANES:(h + 1) * LANES], q_ref[:, h * LANES:(h + 1) * LANES])

        st_next = scores(0)
        for h in range(A_HEADS):
            st = st_next
            if h + 1 < A_HEADS:
                st_next = scores(h + 1)
            if masked:
                st = jnp.where(keep, st, NEG_BIG)
            m_prev = m_scr[h]
            m_new = jnp.maximum(m_prev, jnp.max(st, axis=0, keepdims=True))
            p = jnp.exp2(st - m_new)
            corr = jnp.exp2(m_prev - m_new)
            l_scr[h] = l_scr[h] * corr + jnp.sum(p, axis=0, keepdims=True)
            rows = slice(h * A_V, (h + 1) * A_V)
            acc_scr[rows, :] = acc_scr[rows, :] * corr + _dot(vt_ref[rows, :], p.astype(BF16))
            m_scr[h] = m_new

    @pl.when(kj < qi)
    def _():
        step(False)

    @pl.when(kj == qi)
    def _():
        step(True)
        ot = jnp.concatenate([acc_scr[h * A_V:(h + 1) * A_V, :] / l_scr[h] for h in range(A_HEADS)], axis=0)
        o_ref[...] = ot.T


def flash_prompt(q, k, vt, batch, seq, tq):
    nq = seq // tq
    pairs = [(i, j) for i in range(nq) for j in range(i + 1)]
    qi = jnp.asarray([p[0] for p in pairs], jnp.int32)
    kj = jnp.asarray([p[1] for p in pairs], jnp.int32)
    grid_spec = pltpu.PrefetchScalarGridSpec(
        num_scalar_prefetch=2,
        grid=(batch, len(pairs)),
        in_specs=[pl.BlockSpec((tq, A_PADW), lambda b, p, qi, kj: (b * nq + qi[p], 0)),
                  pl.BlockSpec((tq, A_PADW), lambda b, p, qi, kj: (b * nq + kj[p], 0)),
                  pl.BlockSpec((A_WIDTH, tq), lambda b, p, qi, kj: (0, b * nq + kj[p]))],
        out_specs=pl.BlockSpec((tq, A_WIDTH), lambda b, p, qi, kj: (b * nq + qi[p], 0)),
        scratch_shapes=[pltpu.VMEM((A_HEADS, 1, tq), F32), pltpu.VMEM((A_HEADS, 1, tq), F32),
                        pltpu.VMEM((A_WIDTH, tq), F32)])
    return pl.pallas_call(
        _flash_kernel, grid_spec=grid_spec,
        out_shape=jax.ShapeDtypeStruct((batch * seq, A_WIDTH), F32),
        compiler_params=_cparams("parallel", "arbitrary"),
        name="flash_prompt",
    )(qi, kj, q, k, vt)


def _absorb_query_kernel(q_ref, gk_ref, ukhi_ref, uklo_ref, qg_ref, qabs_ref):
    qg = q_ref[...] * gk_ref[...]
    qg_ref[...] = qg
    for h in range(A_HEADS):
        blk = qg[:, h * LANES:(h + 1) * LANES]
        q_hi = blk.astype(BF16)
        q_lo = (blk - q_hi.astype(F32)).astype(BF16)
        w_hi = ukhi_ref[h * LANES:(h + 1) * LANES, :]
        qabs_ref[h] = _dot(q_hi, w_hi) + _dot(q_hi, uklo_ref[h * LANES:(h + 1) * LANES, :]) + _dot(q_lo, w_hi)


def absorb_query(q_s, wts):
    db = q_s.shape[0]
    whole = lambda a: pl.BlockSpec(a.shape, lambda i: (0,) * a.ndim)
    return pl.pallas_call(
        _absorb_query_kernel,
        grid=(1,),
        in_specs=[whole(q_s), whole(wts["g_k"]), whole(wts["ukt_hi"]), whole(wts["ukt_lo"])],
        out_specs=[pl.BlockSpec((db, A_PADW), lambda i: (0, 0)),
                   pl.BlockSpec((A_HEADS, db, A_KV_LORA), lambda i: (0, 0, 0))],
        out_shape=[jax.ShapeDtypeStruct((db, A_PADW), F32),
                   jax.ShapeDtypeStruct((A_HEADS, db, A_KV_LORA), F32)],
        compiler_params=_cparams("arbitrary"),
        name="absorb_query",
    )(q_s, wts["g_k"], wts["ukt_hi"], wts["ukt_lo"])


def _sample_attn_kernel(pt_ref, qabs_ref, qpe_ref, ukt_ref, wuv_ref, cnew_ref, knew_ref, ckv_hbm, kpe_hbm, o_ref,
                        lhs_scr, cb_scr, s_scr, ckv_buf, kpe_buf, sems, *, la, n_pages, pages_per_block):
    b = pl.program_id(0)
    last = pl.num_programs(0) - 1
    slot = b % 2
    nxt = jnp.minimum(b + 1, last)
    n_nope = A_HEADS * A_NOPE
    n_past = n_pages * PAGE_SIZE

    def page_copies(seq, to_slot, i):
        page = pt_ref[seq * n_pages + i]
        rows = pl.ds(i * PAGE_SIZE, PAGE_SIZE)
        return (pltpu.make_async_copy(ckv_hbm.at[la, page], ckv_buf.at[to_slot, rows, :], sems.at[0, to_slot]),
                pltpu.make_async_copy(kpe_hbm.at[la, page], kpe_buf.at[to_slot, :, rows], sems.at[1, to_slot]))

    def start_pages(seq, to_slot, pages):
        for i in pages:
            for cp in page_copies(seq, to_slot, i):
                cp.start()

    def wait_pages(seq, to_slot):
        for i in range(n_pages):
            for cp in page_copies(seq, to_slot, i):
                cp.wait()

    @pl.when(b == 0)
    def _():
        start_pages(b, slot, range(n_pages))

    wait_pages(b, slot)
    lhs_scr[:n_nope, :] = ukt_ref[...]
    lhs_scr[n_nope:, :] = qabs_ref[...]

    def score(c, kpe_t, start):
        n = c.shape[0]
        cb = c.astype(BF16)
        cb_scr[start:start + n, :] = cb
        a = _dot_nt(lhs_scr[...], cb)
        kn = a[:n_nope]
        ss = jnp.sum((kn * kn).reshape(A_HEADS, A_NOPE, n), axis=1)
        s_pe = _dot(qpe_ref[...], kpe_t.astype(BF16))[:A_HEADS]
        pe_ss = jnp.sum(kpe_t * kpe_t, axis=0, keepdims=True)
        rs = lax.rsqrt((ss + pe_ss) * (1.0 / A_QK) + EPS)
        return (a[n_nope:n_nope + A_HEADS] + s_pe) * rs

    for blk in range(0, n_pages, pages_per_block):
        start, n = blk * PAGE_SIZE, pages_per_block * PAGE_SIZE
        s_scr[:, start:start + n] = score(ckv_buf[slot, start:start + n, :], kpe_buf[slot, :, start:start + n], start)
        start_pages(nxt, 1 - slot, range(blk, blk + pages_per_block))
    s_new = score(jnp.broadcast_to(cnew_ref[...], (LANES, A_KV_LORA)),
                  jnp.broadcast_to(knew_ref[...], (A_ROPE, LANES)), n_past)
    s_scr[:, n_past:] = jnp.where(lax.broadcasted_iota(jnp.int32, s_new.shape, 1) < 1, s_new, NEG_BIG)

    s = s_scr[...]
    p = jnp.exp(s - jnp.max(s, axis=-1, keepdims=True))
    lat = _dot(p.astype(BF16), cb_scr[...]) / jnp.sum(p, axis=-1, keepdims=True)
    full = _dot(lat.astype(BF16), wuv_ref[...])
    lane = lax.broadcasted_iota(jnp.int32, full.shape, 1)
    row = lax.broadcasted_iota(jnp.int32, full.shape, 0)
    sel = (lane >= row * A_V) & (lane < (row + 1) * A_V)
    o_ref[...] = jnp.sum(jnp.where(sel, full, 0.0), axis=0, keepdims=True)

    @pl.when(b == last)
    def _():
        wait_pages(nxt, 1 - slot)


def mla_sample_attention(q_s, ckv_new, kpe_new, cache_ckv, cache_kpe_t, la, page_table, wts, pages_per_block):
    db, n_pages = page_table.shape
    n_past = n_pages * PAGE_SIZE
    n_tok = n_past + LANES
    row3 = lambda r, n: pl.BlockSpec((None, r, n), lambda b, pt: (b, 0, 0))
    full = lambda a: pl.BlockSpec(a.shape, lambda b, pt: (0, 0))
    hbm = pl.BlockSpec(memory_space=pl.ANY)
    q_rows = 2 * SUBLANES
    lhs_rows = A_HEADS * A_NOPE + q_rows
    qg, qabs = absorb_query(q_s, wts)
    pad_rows = lambda a: jnp.pad(a, ((0, 0), (0, q_rows - A_HEADS), (0, 0))).astype(BF16)
    qabs = pad_rows(qabs.transpose(1, 0, 2))
    qpe = pad_rows(qg.reshape(db, A_HEADS, LANES)[:, :, A_NOPE:A_QK])
    grid_spec = pltpu.PrefetchScalarGridSpec(
        num_scalar_prefetch=1,
        grid=(db,),
        in_specs=[row3(q_rows, A_KV_LORA), row3(q_rows, A_ROPE), full(wts["ukt"]),
                  full(wts["w_uv"]), row3(1, A_KV_LORA), row3(A_ROPE, 1), hbm, hbm],
        out_specs=pl.BlockSpec((None, 1, A_WIDTH), lambda b, pt: (b, 0, 0)),
        scratch_shapes=[pltpu.VMEM((lhs_rows, A_KV_LORA), BF16),
                        pltpu.VMEM((n_tok, A_KV_LORA), BF16), pltpu.VMEM((A_HEADS, n_tok), F32),
                        pltpu.VMEM((2, n_past, A_KV_LORA), F32), pltpu.VMEM((2, A_ROPE, n_past), F32),
                        pltpu.SemaphoreType.DMA((2, 2))])
    out = pl.pallas_call(
        functools.partial(_sample_attn_kernel, la=la, n_pages=n_pages, pages_per_block=pages_per_block),
        grid_spec=grid_spec,
        out_shape=jax.ShapeDtypeStruct((db, 1, A_WIDTH), F32),
        compiler_params=_cparams("arbitrary"),
        name="mla_sample_attention",
    )(page_table.reshape(-1), qabs, qpe, wts["ukt"],
      wts["w_uv"], ckv_new.reshape(db, 1, A_KV_LORA), kpe_new.reshape(db, A_ROPE, 1), cache_ckv, cache_kpe_t)
    return out.reshape(db, A_WIDTH)


def _hgrn_lower_bound(lb_all, li):
    e = jnp.exp(lb_all - jnp.max(lb_all, axis=0, keepdims=True))
    smx = e / jnp.sum(e, axis=0, keepdims=True)
    return jnp.sum(smx[1:li + 1], axis=0, keepdims=True) if li > 0 else jnp.zeros_like(smx[:1])


def _hgrn_gates(q, f, lb):
    g = jnp.log(lb + (1.0 - lb) * _sigmoid(f))
    k = (1.0 - lb) * _sigmoid(-f)
    qf = q * _sigmoid(q) * (B_DK ** -0.5)
    return qf, k, g


def _cumsum_rows(x):
    n = x.shape[0]
    row = lax.broadcasted_iota(jnp.int32, x.shape, 0)
    d = 1
    while d < n:
        x = x + jnp.where(row >= d, pltpu.roll(x, d, 0), 0.0)
        d *= 2
    return x


def _bcast_row_in_groups(x, j):
    n, w = x.shape
    x3 = x.reshape(n // SUBLANES, SUBLANES, w)
    return jnp.broadcast_to(x3[:, j:j + 1, :], x3.shape).reshape(n, w)


def _intra_chunk_masks(c):
    tt = lax.broadcasted_iota(jnp.int32, (c, c), 0)
    ss = lax.broadcasted_iota(jnp.int32, (c, c), 1)
    masks = {"diag": tt == ss}
    half = c // 2
    while half >= 1:
        blk = 2 * half
        keep = ((tt % blk) >= half) & ((ss % blk) < half)
        masks[half] = keep & ((tt // blk) == (ss // blk)) if blk < c else keep
        half //= 2
    return masks


def _intra_chunk_att(qf, k, b, masks):
    c = qf.shape[0]
    row = lax.broadcasted_iota(jnp.int32, (c, 1), 0)
    att = jnp.where(masks["diag"], jnp.sum(qf * k, axis=-1, keepdims=True), 0.0)
    half = c // 2
    while half >= 1:
        blk = 2 * half
        if blk >= SUBLANES:
            bref = jnp.concatenate(
                [jnp.broadcast_to(b[m:m + 1], (blk, b.shape[1])) for m in range(half, c, blk)], axis=0)
        else:
            rm = row % SUBLANES
            bref = _bcast_row_in_groups(b, SUBLANES - half)
            for m in range(SUBLANES - half - blk, 0, -blk):
                bref = jnp.where(rm < m + half, _bcast_row_in_groups(b, m), bref)
        e = jnp.exp(-jnp.abs(b - bref))
        p = _dot_nt((qf * e).astype(BF16), (k * e).astype(BF16))
        att = att + jnp.where(masks[half], p, 0.0)
        half //= 2
    return att


def _hgrn_prompt_kernel(q_ref, f_ref, v_ref, lb_ref, o_ref, st_ref, state_scr, *, li):
    t = pl.program_id(2)

    @pl.when(t == 0)
    def _():
        state_scr[...] = jnp.zeros(state_scr.shape, F32)

    lb = _hgrn_lower_bound(lb_ref[...], li)
    masks = _intra_chunk_masks(B_CHUNK)
    for ci in range(q_ref.shape[0] // B_CHUNK):
        sl = slice(ci * B_CHUNK, (ci + 1) * B_CHUNK)
        qf, k, g = _hgrn_gates(q_ref[sl, :], f_ref[sl, :], lb)
        v = v_ref[sl, :]
        b = _cumsum_rows(g)
        vb = v.astype(BF16)
        intra = _dot(_intra_chunk_att(qf, k, b, masks).astype(BF16), vb)
        blast = b[B_CHUNK - 1:B_CHUNK]
        kd = (k * jnp.exp(blast - b)).astype(BF16)
        upd = lax.dot_general(vb, kd, (((0,), (0,)), ((), ())), preferred_element_type=F32)
        st = state_scr[...]
        o_ref[sl, :] = intra + _dot_nt((qf * jnp.exp(b)).astype(BF16), st.astype(BF16))
        state_scr[...] = jnp.exp(blast) * st + upd

    @pl.when(t == pl.num_programs(2) - 1)
    def _():
        st_ref[...] = state_scr[...].T


def hgrn_prompt(proj, lower_bounds, li, batch, seq, tblk):
    nt = seq // tblk
    col = lambda off: pl.BlockSpec((tblk, LANES), lambda b, h, t: (b * nt + t, off + h))
    return pl.pallas_call(
        functools.partial(_hgrn_prompt_kernel, li=li),
        grid=(batch, B_HEADS, nt),
        in_specs=[col(0), col(B_HEADS), col(2 * B_HEADS),
                  pl.BlockSpec((lower_bounds.shape[0], LANES), lambda b, h, t: (0, h))],
        out_specs=[pl.BlockSpec((tblk, LANES), lambda b, h, t: (b * nt + t, h)),
                   pl.BlockSpec((None, None, B_DK, B_DV), lambda b, h, t: (b, h, 0, 0))],
        out_shape=[jax.ShapeDtypeStruct((batch * seq, B_HEADS * B_DV), F32),
                   jax.ShapeDtypeStruct((batch, B_HEADS, B_DK, B_DV), F32)],
        scratch_shapes=[pltpu.VMEM((B_DV, B_DK), F32)],
        compiler_params=_cparams("parallel", "parallel", "arbitrary"),
        name="hgrn_prompt",
    )(proj, proj, proj, lower_bounds)


def _hgrn_sample_gates_kernel(p_ref, lb_ref, qe_ref, eg_ref, k_ref, av_ref, *, li):
    w = B_HEADS * B_DK
    lb = _hgrn_lower_bound(lb_ref[...], li)
    qf, k, g = _hgrn_gates(p_ref[:, :w], p_ref[:, w:2 * w], lb)
    v = p_ref[:, 2 * w:3 * w]
    eg = jnp.exp(g)
    qe_ref[...] = qf * eg
    eg_ref[...] = eg
    k_ref[...] = k
    qk = qf * k
    av_ref[...] = jnp.concatenate(
        [jnp.sum(qk[:, h * LANES:(h + 1) * LANES], axis=-1, keepdims=True) * v[:, h * LANES:(h + 1) * LANES]
         for h in range(B_HEADS)], axis=1)


def _hgrn_sample_state_kernel(st_ref, qe_ref, egt_ref, kt_ref, v_ref, av_ref, o_ref, ns_ref):
    for s in range(st_ref.shape[0]):
        for h in range(B_HEADS):
            st = st_ref[s, h]
            ns_ref[s, h] = egt_ref[s, :, h:h + 1] * st + kt_ref[s, :, h:h + 1] * v_ref[s, h:h + 1, :]
            qe = jnp.broadcast_to(qe_ref[s, h:h + 1, :], (2 * SUBLANES, B_DK)).astype(BF16)
            o_ref[s, h:h + 1, :] = _dot(qe, st.astype(BF16))[:1] + av_ref[s, h:h + 1, :]


def hgrn_sample(proj_s, lower_bounds, li, state):
    db = proj_s.shape[0]
    w = B_HEADS * B_DK
    whole = lambda a: pl.BlockSpec(a.shape, lambda i: (0,) * a.ndim)
    qe, eg, k, av = pl.pallas_call(
        functools.partial(_hgrn_sample_gates_kernel, li=li),
        grid=(1,),
        in_specs=[whole(proj_s), whole(lower_bounds)],
        out_specs=[pl.BlockSpec((db, w), lambda i: (0, 0))] * 4,
        out_shape=[jax.ShapeDtypeStruct((db, w), F32)] * 4,
        compiler_params=_cparams("arbitrary"),
        name="hgrn_sample_gates",
    )(proj_s, lower_bounds)
    heads = lambda a: a.reshape(db, B_HEADS, B_DK)
    cols = lambda a: heads(a).transpose(0, 2, 1)
    sb = SEQS_HGRN_SAMPLE
    hrow = pl.BlockSpec((sb, B_HEADS, B_DK), lambda b: (b, 0, 0))
    hcol = pl.BlockSpec((sb, B_DK, B_HEADS), lambda b: (b, 0, 0))
    stspec = pl.BlockSpec((sb, B_HEADS, B_DK, B_DV), lambda b: (b, 0, 0, 0))
    o, new_state = pl.pallas_call(
        _hgrn_sample_state_kernel,
        grid=(db // sb,),
        in_specs=[stspec, hrow, hcol, hcol, hrow, hrow],
        out_specs=[hrow, stspec],
        out_shape=[jax.ShapeDtypeStruct((db, B_HEADS, B_DV), F32),
                   jax.ShapeDtypeStruct(state.shape, F32)],
        compiler_params=_cparams("parallel"),
        name="hgrn_sample_state",
    )(state, heads(qe), cols(eg), cols(k), heads(proj_s[:, 2 * w:3 * w]), heads(av))
    return o.reshape(db, w), new_state


def _rglru_gates(y, wax_ref, ba, bx, lam):
    yb = y.astype(BF16)
    rs, is_ = [], []
    for n in range(C_HEADS):
        ax = _dot(yb[:, n * C_BW:(n + 1) * C_BW], wax_ref[n])
        rs.append(ax[:, :C_BW])
        is_.append(ax[:, C_BW:])
    r = _sigmoid(jnp.concatenate(rs, axis=1) + ba)
    i = _sigmoid(jnp.concatenate(is_, axis=1) + bx)
    softplus = jnp.maximum(-lam, 0.0) + jnp.log1p(jnp.exp(-jnp.abs(lam)))
    log_a = -C_GATE_C * r * softplus
    a = jnp.exp(log_a)
    th = jnp.tanh(log_a)
    u = jnp.sqrt(-2.0 * th / (1.0 - th)) * (i * y)
    return a, u


def _rglru_prompt_kernel(xb_ref, cw_ref, cb_ref, wax_ref, ba_ref, bx_ref, lam_ref,
                         hs_ref, tail_ref, hl_ref, prev_scr, h_scr, a_scr, u_scr):
    t = pl.program_id(1)
    tb = xb_ref.shape[0]

    @pl.when(t == 0)
    def _():
        prev_scr[...] = jnp.zeros(prev_scr.shape, F32)
        h_scr[...] = jnp.zeros(h_scr.shape, F32)

    xb = xb_ref[...]
    prev = prev_scr[...]
    row = lax.broadcasted_iota(jnp.int32, (tb, 1), 0)
    y = cb_ref[...] + cw_ref[C_CONV - 1:C_CONV, :] * xb
    for d in range(1, C_CONV):
        sh = pltpu.roll(xb, d, 0)
        for r in range(d):
            sh = jnp.where(row == r, prev[SUBLANES - d + r:SUBLANES - d + r + 1, :], sh)
        y = y + cw_ref[C_CONV - 1 - d:C_CONV - d, :] * sh
    prev_scr[...] = xb[tb - SUBLANES:, :]
    a, u = _rglru_gates(y, wax_ref, ba_ref[...], bx_ref[...], lam_ref[...])
    a_scr[...] = a
    u_scr[...] = u

    def body(i, h):
        base = pl.multiple_of(i * SUBLANES, SUBLANES)
        for r in range(SUBLANES):
            h = a_scr[pl.ds(base + r, 1), :] * h + u_scr[pl.ds(base + r, 1), :]
            hs_ref[pl.ds(base + r, 1), :] = h
        return h

    h = lax.fori_loop(0, tb // SUBLANES, body, h_scr[...])
    h_scr[...] = h

    @pl.when(t == pl.num_programs(1) - 1)
    def _():
        tail_ref[...] = xb[tb - SUBLANES:, :]
        hl_ref[...] = h


def _rglru_weights(conv_w, conv_b, w_a, b_a, w_x, b_x, lam):
    r1 = lambda a: a.reshape(1, -1)
    return {"cw": conv_w, "cb": r1(conv_b), "wax": jnp.concatenate([w_a, w_x], axis=2).astype(BF16),
            "ba": r1(b_a), "bx": r1(b_x), "lam": r1(lam)}


def rglru_prompt(proj, wts, batch, seq, tblk):
    nt = seq // tblk
    w = C_WIDTH
    full = lambda a: pl.BlockSpec(a.shape, lambda b, t: (0,) * a.ndim)
    names = ("cw", "cb", "wax", "ba", "bx", "lam")
    return pl.pallas_call(
        _rglru_prompt_kernel,
        grid=(batch, nt),
        in_specs=[pl.BlockSpec((tblk, w), lambda b, t: (b * nt + t, 0))] + [full(wts[n]) for n in names],
        out_specs=[pl.BlockSpec((tblk, w), lambda b, t: (b * nt + t, 0)),
                   pl.BlockSpec((None, SUBLANES, w), lambda b, t: (b, 0, 0)),
                   pl.BlockSpec((None, 1, w), lambda b, t: (b, 0, 0))],
        out_shape=[jax.ShapeDtypeStruct((batch * seq, w), F32),
                   jax.ShapeDtypeStruct((batch, SUBLANES, w), F32),
                   jax.ShapeDtypeStruct((batch, 1, w), F32)],
        scratch_shapes=[pltpu.VMEM((SUBLANES, w), F32), pltpu.VMEM((1, w), F32),
                        pltpu.VMEM((tblk, w), F32), pltpu.VMEM((tblk, w), F32)],
        compiler_params=_cparams("parallel", "arbitrary"),
        name="rglru_prompt",
    )(proj, *[wts[n] for n in names])


def _rglru_sample_kernel(xb_ref, buf_ref, h0_ref, cw_ref, cb_ref, wax_ref, ba_ref, bx_ref, lam_ref,
                         hs_ref, nbuf_ref):
    xb = xb_ref[...]
    y = cb_ref[...] + cw_ref[C_CONV - 1:C_CONV, :] * xb
    for j in range(C_CONV - 1):
        y = y + cw_ref[j:j + 1, :] * buf_ref[j]
    a, u = _rglru_gates(y, wax_ref, ba_ref[...], bx_ref[...], lam_ref[...])
    hs_ref[...] = a * h0_ref[...] + u
    for j in range(C_CONV - 2):
        nbuf_ref[j] = buf_ref[j + 1]
    nbuf_ref[C_CONV - 2] = xb


def rglru_sample(proj_s, buf_t, h0, wts):
    db = proj_s.shape[0]
    w = C_WIDTH
    whole = lambda a: pl.BlockSpec(a.shape, lambda i: (0,) * a.ndim)
    names = ("cw", "cb", "wax", "ba", "bx", "lam")
    return pl.pallas_call(
        _rglru_sample_kernel,
        grid=(1,),
        in_specs=[pl.BlockSpec((db, w), lambda i: (0, 0)), whole(buf_t), whole(h0)] + [whole(wts[n]) for n in names],
        out_specs=[pl.BlockSpec((db, w), lambda i: (0, 0)), whole(buf_t)],
        out_shape=[jax.ShapeDtypeStruct((db, w), F32), jax.ShapeDtypeStruct(buf_t.shape, F32)],
        compiler_params=_cparams("arbitrary"),
        name="rglru_sample",
    )(proj_s, buf_t, h0, *[wts[n] for n in names])


TM_PROMPT = 512
TQ_FLASH = 512
T_HGRN = 256
T_RGLRU = 256
PAGES_PER_BLOCK = 8
SEQS_HGRN_SAMPLE = 4

def kernel(x_prompt, x_sample, cache_ckv, cache_kpe, page_table, state_hgrn, state_conv, state_lru, norm_g, mla_w_in, mla_g_q_lora, mla_g_kv, mla_w_uq, mla_w_ukv, mla_g_q, mla_g_k, mla_w_out, hgrn_w_in, hgrn_lower_bounds, hgrn_g_norm, hgrn_w_out, rglru_w_in, rglru_conv_w, rglru_conv_b, rglru_w_a, rglru_b_a, rglru_w_x, rglru_b_x, rglru_L, rglru_w_out):
    bsz, s_p, d = x_prompt.shape
    db, s_s, _ = x_sample.shape
    assert s_s == 1 and d == D_MODEL
    depth = norm_g.shape[0]
    past_len = page_table.shape[1] * PAGE_SIZE
    xp = x_prompt.reshape(bsz * s_p, d)
    xs = x_sample.reshape(db, d)
    tabs_p = _rope_tables(jnp.arange(s_p))
    tabs_s = _rope_tables(jnp.full((db,), past_len))
    cache_kpe_t = cache_kpe.transpose(0, 1, 3, 2)

    ckv_p, kpe_p, ckv_s, kpe_s = [], [], [], []
    hg_p, hg_s, cv_p, cv_s, lr_p, lr_s = [], [], [], [], [], []
    for li in range(depth):
        kind, j = li % N_MIXERS, li // N_MIXERS
        if kind == 0:
            wts = _mla_weights(mla_w_in[j], mla_g_q_lora[j], mla_g_kv[j], mla_w_uq[j], mla_w_ukv[j],
                               mla_g_q[j], mla_g_k[j])
            w_out = mla_w_out[j].astype(BF16)
            q, k, vt, gate, ckv, kpe_blk = mla_proj(xp, norm_g[li], wts, tabs_p, s_p, TM_PROMPT, BF16, "g_q_base2")
            o = flash_prompt(q, k, vt, bsz, s_p, TQ_FLASH)
            xp = gated_out(o, gate, 0, w_out, xp, TM_PROMPT)
            ckv_p.append(ckv.reshape(bsz, s_p, A_KV_LORA))
            kpe_p.append(kpe_blk[:, A_NOPE:A_QK].reshape(bsz, s_p, A_ROPE))

            q_s, _, _, gate_s, ckv_n, kpe_blk_s = mla_proj(xs, norm_g[li], wts, tabs_s, db, db, F32, "g_q")
            kpe_n = kpe_blk_s[:, A_NOPE:A_QK]
            o_s = mla_sample_attention(q_s, ckv_n, kpe_n, cache_ckv, cache_kpe_t, j, page_table, wts,
                                       PAGES_PER_BLOCK)
            xs = gated_out(o_s, gate_s, 0, w_out, xs, db)
            ckv_s.append(ckv_n.reshape(db, 1, A_KV_LORA))
            kpe_s.append(kpe_n.reshape(db, 1, A_ROPE))
        elif kind == 1:
            w_in = hgrn_w_in[j].astype(BF16)
            w_out = hgrn_w_out[j].astype(BF16)
            proj = norm_matmul(xp, norm_g[li], w_in, TM_PROMPT // 2)
            o, st = hgrn_prompt(proj, hgrn_lower_bounds, li, bsz, s_p, T_HGRN)
            xp = gated_out(o, proj, 3, w_out, xp, TM_PROMPT, hgrn_g_norm[j])
            hg_p.append(st)

            proj_s = norm_matmul(xs, norm_g[li], w_in, db)
            o_s, st_s = hgrn_sample(proj_s, hgrn_lower_bounds, li, state_hgrn[j])
            xs = gated_out(o_s, proj_s, 3, w_out, xs, db, hgrn_g_norm[j])
            hg_s.append(st_s)
        else:
            w_in = rglru_w_in[j].astype(BF16)
            w_out = rglru_w_out[j].astype(BF16)
            wts = _rglru_weights(rglru_conv_w[j], rglru_conv_b[j], rglru_w_a[j], rglru_b_a[j],
                                 rglru_w_x[j], rglru_b_x[j], rglru_L[j])
            proj = norm_matmul(xp, norm_g[li], w_in, TM_PROMPT)
            hs, tail, hl = rglru_prompt(proj, wts, bsz, s_p, T_RGLRU)
            xp = gated_out(hs, proj, 1, w_out, xp, TM_PROMPT)
            cv_p.append(tail[:, SUBLANES - (C_CONV - 1):, :])
            lr_p.append(hl.reshape(bsz, C_WIDTH))

            proj_s = norm_matmul(xs, norm_g[li], w_in, db)
            hs_s, nbuf = rglru_sample(proj_s, state_conv[j].transpose(1, 0, 2), state_lru[j], wts)
            xs = gated_out(hs_s, proj_s, 1, w_out, xs, db)
            cv_s.append(nbuf.transpose(1, 0, 2))
            lr_s.append(hs_s)

    return (xp.reshape(bsz, s_p, d), xs.reshape(db, 1, d),
            jnp.stack(ckv_p), jnp.stack(kpe_p), jnp.stack(ckv_s), jnp.stack(kpe_s),
            jnp.stack(hg_p), jnp.stack(hg_s), jnp.stack(cv_p), jnp.stack(cv_s),
            jnp.stack(lr_p), jnp.stack(lr_s))
```

```python
import functools
import math

import jax
import jax.numpy as jnp
from jax import lax
from jax.experimental import pallas as pl
from jax.experimental.pallas import tpu as pltpu

F32 = jnp.float32
BF16 = jnp.bfloat16

LANES = 128
SUBLANES = 8
VMEM_LIMIT_BYTES = 56 * 1024 * 1024

D_MODEL = 1024
PAGE_SIZE = 128
N_MIXERS = 3
EPS = 1e-6
ROPE_THETA = 10000.0
NEG_BIG = -1e30

A_HEADS = 8
A_NOPE = 64
A_ROPE = 32
A_QK = A_NOPE + A_ROPE
A_V = 64
A_Q_LORA = 384
A_KV_LORA = 256
A_WIDTH = A_HEADS * A_V
A_HALF = A_ROPE // 2
A_PADW = A_HEADS * LANES

B_HEADS = 8
B_DK = 128
B_DV = 128
B_CHUNK = 64

C_WIDTH = 1024
C_HEADS = 8
C_BW = 128
C_CONV = 4
C_GATE_C = 8.0


def _cparams(*sem):
    return pltpu.CompilerParams(dimension_semantics=sem, vmem_limit_bytes=VMEM_LIMIT_BYTES)


def _sigmoid(x):
    return 1.0 / (1.0 + jnp.exp(-x))


def _rms_rows(x, g):
    ms = jnp.mean(x * x, axis=-1, keepdims=True)
    return x * lax.rsqrt(ms + EPS) * g


def _dot_nt(a, b):
    return lax.dot_general(a, b, (((1,), (1,)), ((), ())), preferred_element_type=F32)


def _dot(a, b):
    return jnp.dot(a, b, preferred_element_type=F32)


def _norm_matmul_kernel(x_ref, g_ref, w_ref, o_ref):
    h = _rms_rows(x_ref[...], g_ref[...])
    o_ref[...] = _dot(h.astype(BF16), w_ref[...])


def norm_matmul(x, g, w_bf16, tm):
    m, d = x.shape
    n = w_bf16.shape[1]
    return pl.pallas_call(
        _norm_matmul_kernel,
        grid=(m // tm,),
        in_specs=[pl.BlockSpec((tm, d), lambda i: (i, 0)),
                  pl.BlockSpec((1, d), lambda i: (0, 0)),
                  pl.BlockSpec((d, n), lambda i: (0, 0))],
        out_specs=pl.BlockSpec((tm, n), lambda i: (i, 0)),
        out_shape=jax.ShapeDtypeStruct((m, n), F32),
        compiler_params=_cparams("parallel"),
        name="norm_matmul",
    )(x, g.reshape(1, d), w_bf16)


def _gated_out_kernel(o_ref, gate_ref, w_ref, x_ref, gn_ref, y_ref, *, head_norm):
    o = o_ref[...]
    if head_norm:
        parts = []
        for h in range(o.shape[1] // LANES):
            parts.append(_rms_rows(o[:, h * LANES:(h + 1) * LANES], gn_ref[...]))
        o = jnp.concatenate(parts, axis=1)
    gate = gate_ref[...]
    z = o * (gate * _sigmoid(gate))
    y_ref[...] = x_ref[...] + _dot(z.astype(BF16), w_ref[...])


def gated_out(o, gate_arr, gate_col, w_bf16, x, tm, g_norm=None):
    m, w = o.shape
    d = x.shape[1]
    head_norm = g_norm is not None
    gn = (g_norm if head_norm else jnp.ones((LANES,), F32)).reshape(1, LANES)
    return pl.pallas_call(
        functools.partial(_gated_out_kernel, head_norm=head_norm),
        grid=(m // tm,),
        in_specs=[pl.BlockSpec((tm, w), lambda i: (i, 0)),
                  pl.BlockSpec((tm, w), lambda i: (i, gate_col)),
                  pl.BlockSpec((w, d), lambda i: (0, 0)),
                  pl.BlockSpec((tm, d), lambda i: (i, 0)),
                  pl.BlockSpec((1, LANES), lambda i: (0, 0))],
        out_specs=pl.BlockSpec((tm, d), lambda i: (i, 0)),
        out_shape=jax.ShapeDtypeStruct((m, d), F32),
        compiler_params=_cparams("parallel"),
        name="gated_out",
    )(o, gate_arr, w_bf16, x, gn)


def _rope_lanes(x, c, s1, s2):
    n = x.shape[1]
    return x * c + pltpu.roll(x, n - A_HALF, 1) * s1 + pltpu.roll(x, A_HALF, 1) * s2


def _head_norm_blocks(x, g):
    parts = []
    for h in range(A_HEADS):
        blk = x[:, h * LANES:(h + 1) * LANES]
        ss = jnp.sum(blk * blk, axis=-1, keepdims=True)
        parts.append(blk * lax.rsqrt(ss * (1.0 / A_QK) + EPS) * g[:, h * LANES:(h + 1) * LANES])
    return jnp.concatenate(parts, axis=1)


def _mla_proj_kernel(x_ref, gn_ref, win_ref, gql_ref, gkv_ref, wuq_ref, gq_ref, wuk_ref, wuvt_ref, gk_ref,
                     c_ref, s1_ref, s2_ref,
                     q_ref, k_ref, vt_ref, gate_ref, ckv_ref, kpe_ref):
    h = _rms_rows(x_ref[...], gn_ref[...])
    p = _dot(h.astype(BF16), win_ref[...])
    gate_ref[...] = p[:, :A_WIDTH]
    cq = p[:, A_WIDTH:A_WIDTH + A_Q_LORA]
    ckv = p[:, A_WIDTH + A_Q_LORA:A_WIDTH + A_Q_LORA + A_KV_LORA]
    kpe = p[:, A_WIDTH + A_Q_LORA + A_KV_LORA:]
    c, s1, s2 = c_ref[...], s1_ref[...], s2_ref[...]
    c8 = jnp.concatenate([c] * A_HEADS, axis=1)
    sin8 = jnp.concatenate([s2 - s1] * A_HEADS, axis=1)

    q2 = _dot(_rms_rows(cq, gql_ref[...]).astype(BF16), wuq_ref[...])
    q = q2[:, :A_PADW] * c8 + q2[:, A_PADW:] * sin8
    q_ref[...] = _head_norm_blocks(q, gq_ref[...]).astype(q_ref.dtype)

    ckvn = _rms_rows(ckv, gkv_ref[...])
    ckv_ref[...] = ckvn
    kper = _rope_lanes(kpe, c, s1, s2)
    kpe_ref[...] = kper
    cb = ckvn.astype(BF16)
    kn = _dot(cb, wuk_ref[...]) + jnp.concatenate([kper] * A_HEADS, axis=1)
    k_ref[...] = _head_norm_blocks(kn, gk_ref[...]).astype(k_ref.dtype)
    vt_ref[...] = _dot_nt(wuvt_ref[...], cb).astype(vt_ref.dtype)


def mla_proj(x, g_norm, wts, rope_tabs, rows_per_seq, tm, qkv_dtype, q_gain):
    m, d = x.shape
    c, s1, s2 = rope_tabs
    nblk = rows_per_seq // tm
    full = lambda a: pl.BlockSpec(a.shape, lambda i: (0, 0))
    tab = pl.BlockSpec((tm, LANES), lambda i: (i % nblk, 0))
    row = lambda n: pl.BlockSpec((tm, n), lambda i: (i, 0))
    return pl.pallas_call(
        _mla_proj_kernel,
        grid=(m // tm,),
        in_specs=[row(d), pl.BlockSpec((1, d), lambda i: (0, 0)), full(wts["w_in"]),
                  full(wts["g_ql"]), full(wts["g_kv"]), full(wts["w_uq"]), full(wts[q_gain]),
                  full(wts["w_uk"]), full(wts["w_uvt"]), full(wts["g_k"]), tab, tab, tab],
        out_specs=[row(A_PADW), row(A_PADW), pl.BlockSpec((A_WIDTH, tm), lambda i: (0, i)),
                   row(A_WIDTH), row(A_KV_LORA), row(LANES)],
        out_shape=[jax.ShapeDtypeStruct((m, A_PADW), qkv_dtype),
                   jax.ShapeDtypeStruct((m, A_PADW), qkv_dtype),
                   jax.ShapeDtypeStruct((A_WIDTH, m), qkv_dtype),
                   jax.ShapeDtypeStruct((m, A_WIDTH), F32),
                   jax.ShapeDtypeStruct((m, A_KV_LORA), F32),
                   jax.ShapeDtypeStruct((m, LANES), F32)],
        compiler_params=_cparams("parallel"),
        name="mla_proj",
    )(x, g_norm.reshape(1, d), wts["w_in"], wts["g_ql"], wts["g_kv"], wts["w_uq"], wts[q_gain],
      wts["w_uk"], wts["w_uvt"], wts["g_k"], c, s1, s2)


def _pad_heads(a, used):
    pad = [(0, 0)] * (a.ndim - 1) + [(0, LANES - used)]
    a = jnp.pad(a, pad)
    return a.reshape(a.shape[:-2] + (a.shape[-2] * LANES,))


def _mla_weights(w_in, g_ql, g_kv, w_uq, w_ukv, g_q, g_k):
    cq, ckv, kpe, gate = jnp.split(w_in, [A_Q_LORA, A_Q_LORA + A_KV_LORA, A_Q_LORA + A_KV_LORA + A_ROPE], axis=1)
    kpe_blk = jnp.pad(kpe, ((0, 0), (A_NOPE, LANES - A_QK)))
    scale = A_QK ** -0.5
    ukv = w_ukv.reshape(A_KV_LORA, A_HEADS, A_NOPE + A_V)
    uk, uv = ukv[:, :, :A_NOPE], ukv[:, :, A_NOPE:]
    ukt_pad = _pad_heads(uk, A_NOPE).T
    ukt_hi = ukt_pad.astype(BF16)
    uq = w_uq.reshape(A_Q_LORA, A_HEADS, A_QK)
    x1, x2 = uq[:, :, A_NOPE:A_NOPE + A_HALF], uq[:, :, A_NOPE + A_HALF:]
    uq_rot = jnp.concatenate([jnp.zeros_like(uq[:, :, :A_NOPE]), -x2, x1], axis=2)
    return {
        "w_in": jnp.concatenate([gate, cq, ckv, kpe_blk], axis=1).astype(BF16),
        "g_ql": g_ql.reshape(1, -1), "g_kv": g_kv.reshape(1, -1),
        "w_uq": jnp.concatenate([_pad_heads(uq, A_QK), _pad_heads(uq_rot, A_QK)], axis=1).astype(BF16),
        "g_q": _pad_heads(jnp.broadcast_to(g_q * scale, (A_HEADS, A_QK)), A_QK).reshape(1, -1),
        "g_q_base2": _pad_heads(jnp.broadcast_to(g_q * (scale * math.log2(math.e)), (A_HEADS, A_QK)),
                                A_QK).reshape(1, -1),
        "g_k": _pad_heads(jnp.broadcast_to(g_k, (A_HEADS, A_QK)), A_QK).reshape(1, -1),
        "w_uk": _pad_heads(uk, A_NOPE).astype(BF16),
        "w_uv": uv.reshape(A_KV_LORA, A_WIDTH).astype(BF16),
        "w_uvt": uv.reshape(A_KV_LORA, A_WIDTH).T.astype(BF16),
        "ukt": uk.reshape(A_KV_LORA, A_HEADS * A_NOPE).T.astype(BF16),
        "ukt_hi": ukt_hi,
        "ukt_lo": (ukt_pad - ukt_hi.astype(F32)).astype(BF16),
    }


def _rope_tables(pos):
    inv = 1.0 / (ROPE_THETA ** (jnp.arange(0, A_ROPE, 2, dtype=F32) / A_ROPE))
    ang = pos.astype(F32)[:, None] * inv[None, :]
    cos, sin = jnp.cos(ang), jnp.sin(ang)
    n = pos.shape[0]
    z = lambda w: jnp.zeros((n, w), F32)
    c = jnp.concatenate([jnp.ones((n, A_NOPE), F32), cos, cos, z(LANES - A_QK)], axis=1)
    s1 = jnp.concatenate([z(A_NOPE), -sin, z(A_HALF), z(LANES - A_QK)], axis=1)
    s2 = jnp.concatenate([z(A_NOPE + A_HALF), sin, z(LANES - A_QK)], axis=1)
    return c, s1, s2


def _flash_kernel(qi_ref, kj_ref, q_ref, k_ref, vt_ref, o_ref, m_scr, l_scr, acc_scr):
    p_id = pl.program_id(1)
    qi = qi_ref[p_id]
    kj = kj_ref[p_id]
    tq, tk = q_ref.shape[0], k_ref.shape[0]

    @pl.when(kj == 0)
    def _():
        m_scr[...] = jnp.full(m_scr.shape, NEG_BIG, F32)
        l_scr[...] = jnp.zeros(l_scr.shape, F32)
        acc_scr[...] = jnp.zeros(acc_scr.shape, F32)

    def step(masked):
        if masked:
            key = lax.broadcasted_iota(jnp.int32, (tk, tq), 0)
            qry = lax.broadcasted_iota(jnp.int32, (tk, tq), 1)
            keep = key <= qry
        def scores(h):
            return _dot_nt(k_ref[:, h * LANES:(h + 1) * LANES], q_ref[:, h * LANES:(h + 1) * LANES])

        st_next = scores(0)
        for h in range(A_HEADS):
            st = st_next
            if h + 1 < A_HEADS:
                st_next = scores(h + 1)
            if masked:
                st = jnp.where(keep, st, NEG_BIG)
            m_prev = m_scr[h]
            m_new = jnp.maximum(m_prev, jnp.max(st, axis=0, keepdims=True))
            p = jnp.exp2(st - m_new)
            corr = jnp.exp2(m_prev - m_new)
            l_scr[h] = l_scr[h] * corr + jnp.sum(p, axis=0, keepdims=True)
            rows = slice(h * A_V, (h + 1) * A_V)
            acc_scr[rows, :] = acc_scr[rows, :] * corr + _dot(vt_ref[rows, :], p.astype(BF16))
            m_scr[h] = m_new

    @pl.when(kj < qi)
    def _():
        step(False)

    @pl.when(kj == qi)
    def _():
        step(True)
        ot = jnp.concatenate([acc_scr[h * A_V:(h + 1) * A_V, :] / l_scr[h] for h in range(A_HEADS)], axis=0)
        o_ref[...] = ot.T


def flash_prompt(q, k, vt, batch, seq, tq):
    nq = seq // tq
    pairs = [(i, j) for i in range(nq) for j in range(i + 1)]
    qi = jnp.asarray([p[0] for p in pairs], jnp.int32)
    kj = jnp.asarray([p[1] for p in pairs], jnp.int32)
    grid_spec = pltpu.PrefetchScalarGridSpec(
        num_scalar_prefetch=2,
        grid=(batch, len(pairs)),
        in_specs=[pl.BlockSpec((tq, A_PADW), lambda b, p, qi, kj: (b * nq + qi[p], 0)),
                  pl.BlockSpec((tq, A_PADW), lambda b, p, qi, kj: (b * nq + kj[p], 0)),
                  pl.BlockSpec((A_WIDTH, tq), lambda b, p, qi, kj: (0, b * nq + kj[p]))],
        out_specs=pl.BlockSpec((tq, A_WIDTH), lambda b, p, qi, kj: (b * nq + qi[p], 0)),
        scratch_shapes=[pltpu.VMEM((A_HEADS, 1, tq), F32), pltpu.VMEM((A_HEADS, 1, tq), F32),
                        pltpu.VMEM((A_WIDTH, tq), F32)])
    return pl.pallas_call(
        _flash_kernel, grid_spec=grid_spec,
        out_shape=jax.ShapeDtypeStruct((batch * seq, A_WIDTH), F32),
        compiler_params=_cparams("parallel", "arbitrary"),
        name="flash_prompt",
    )(qi, kj, q, k, vt)


def _absorb_query_kernel(q_ref, gk_ref, ukhi_ref, uklo_ref, qg_ref, qabs_ref):
    qg = q_ref[...] * gk_ref[...]
    qg_ref[...] = qg
    for h in range(A_HEADS):
        blk = qg[:, h * LANES:(h + 1) * LANES]
        q_hi = blk.astype(BF16)
        q_lo = (blk - q_hi.astype(F32)).astype(BF16)
        w_hi = ukhi_ref[h * LANES:(h + 1) * LANES, :]
        qabs_ref[h] = _dot(q_hi, w_hi) + _dot(q_hi, uklo_ref[h * LANES:(h + 1) * LANES, :]) + _dot(q_lo, w_hi)


def absorb_query(q_s, wts):
    db = q_s.shape[0]
    whole = lambda a: pl.BlockSpec(a.shape, lambda i: (0,) * a.ndim)
    return pl.pallas_call(
        _absorb_query_kernel,
        grid=(1,),
        in_specs=[whole(q_s), whole(wts["g_k"]), whole(wts["ukt_hi"]), whole(wts["ukt_lo"])],
        out_specs=[pl.BlockSpec((db, A_PADW), lambda i: (0, 0)),
                   pl.BlockSpec((A_HEADS, db, A_KV_LORA), lambda i: (0, 0, 0))],
        out_shape=[jax.ShapeDtypeStruct((db, A_PADW), F32),
                   jax.ShapeDtypeStruct((A_HEADS, db, A_KV_LORA), F32)],
        compiler_params=_cparams("arbitrary"),
        name="absorb_query",
    )(q_s, wts["g_k"], wts["ukt_hi"], wts["ukt_lo"])


def _sample_attn_kernel(pt_ref, qabs_ref, qpe_ref, ukt_ref, wuv_ref, cnew_ref, knew_ref, ckv_hbm, kpe_hbm, o_ref,
                        lhs_scr, cb_scr, s_scr, ckv_buf, kpe_buf, sems, *, la, n_pages, pages_per_block):
    b = pl.program_id(0)
    last = pl.num_programs(0) - 1
    slot = b % 2
    nxt = jnp.minimum(b + 1, last)
    n_nope = A_HEADS * A_NOPE
    n_past = n_pages * PAGE_SIZE

    def page_copies(seq, to_slot, i):
        page = pt_ref[seq * n_pages + i]
        rows = pl.ds(i * PAGE_SIZE, PAGE_SIZE)
        return (pltpu.make_async_copy(ckv_hbm.at[la, page], ckv_buf.at[to_slot, rows, :], sems.at[0, to_slot]),
                pltpu.make_async_copy(kpe_hbm.at[la, page], kpe_buf.at[to_slot, :, rows], sems.at[1, to_slot]))

    def start_pages(seq, to_slot, pages):
        for i in pages:
            for cp in page_copies(seq, to_slot, i):
                cp.start()

    def wait_pages(to_slot):
        pltpu.make_async_copy(ckv_buf.at[to_slot], ckv_buf.at[to_slot], sems.at[0, to_slot]).wait()
        pltpu.make_async_copy(kpe_buf.at[to_slot], kpe_buf.at[to_slot], sems.at[1, to_slot]).wait()

    @pl.when(b == 0)
    def _():
        start_pages(b, slot, range(n_pages))

    wait_pages(slot)
    lhs_scr[:n_nope, :] = ukt_ref[...]
    lhs_scr[n_nope:, :] = qabs_ref[...]

    def score(c, kpe_t, start):
        n = c.shape[0]
        cb = c.astype(BF16)
        cb_scr[start:start + n, :] = cb
        a = _dot_nt(lhs_scr[...], cb)
        kn = a[:n_nope]
        ss = jnp.sum((kn * kn).reshape(A_HEADS, A_NOPE, n), axis=1)
        s_pe = _dot(qpe_ref[...], kpe_t.astype(BF16))[:A_HEADS]
        pe_ss = jnp.sum(kpe_t * kpe_t, axis=0, keepdims=True)
        rs = lax.rsqrt((ss + pe_ss) * (1.0 / A_QK) + EPS)
        return (a[n_nope:n_nope + A_HEADS] + s_pe) * rs

    for blk in range(0, n_pages, pages_per_block):
        start, n = blk * PAGE_SIZE, pages_per_block * PAGE_SIZE
        s_scr[:, start:start + n] = score(ckv_buf[slot, start:start + n, :], kpe_buf[slot, :, start:start + n], start)
        start_pages(nxt, 1 - slot, range(blk, blk + pages_per_block))
    s_new = score(jnp.broadcast_to(cnew_ref[...], (LANES, A_KV_LORA)),
                  jnp.broadcast_to(knew_ref[...], (A_ROPE, LANES)), n_past)
    s_scr[:, n_past:] = jnp.where(lax.broadcasted_iota(jnp.int32, s_new.shape, 1) < 1, s_new, NEG_BIG)

    s = s_scr[...]
    p = jnp.exp(s - jnp.max(s, axis=-1, keepdims=True))
    lat = _dot(p.astype(BF16), cb_scr[...]) / jnp.sum(p, axis=-1, keepdims=True)
    full = _dot(lat.astype(BF16), wuv_ref[...])
    lane = lax.broadcasted_iota(jnp.int32, full.shape, 1)
    row = lax.broadcasted_iota(jnp.int32, full.shape, 0)
    sel = (lane >= row * A_V) & (lane < (row + 1) * A_V)
    o_ref[...] = jnp.sum(jnp.where(sel, full, 0.0), axis=0, keepdims=True)

    @pl.when(b == last)
    def _():
        wait_pages(1 - slot)


def mla_sample_attention(q_s, ckv_new, kpe_new, cache_ckv, cache_kpe_t, la, page_table, wts, pages_per_block):
    db, n_pages = page_table.shape
    n_past = n_pages * PAGE_SIZE
    n_tok = n_past + LANES
    row3 = lambda r, n: pl.BlockSpec((None, r, n), lambda b, pt: (b, 0, 0))
    full = lambda a: pl.BlockSpec(a.shape, lambda b, pt: (0, 0))
    hbm = pl.BlockSpec(memory_space=pl.ANY)
    q_rows = 2 * SUBLANES
    lhs_rows = A_HEADS * A_NOPE + q_rows
    qg, qabs = absorb_query(q_s, wts)
    pad_rows = lambda a: jnp.pad(a, ((0, 0), (0, q_rows - A_HEADS), (0, 0))).astype(BF16)
    qabs = pad_rows(qabs.transpose(1, 0, 2))
    qpe = pad_rows(qg.reshape(db, A_HEADS, LANES)[:, :, A_NOPE:A_QK])
    grid_spec = pltpu.PrefetchScalarGridSpec(
        num_scalar_prefetch=1,
        grid=(db,),
        in_specs=[row3(q_rows, A_KV_LORA), row3(q_rows, A_ROPE), full(wts["ukt"]),
                  full(wts["w_uv"]), row3(1, A_KV_LORA), row3(A_ROPE, 1), hbm, hbm],
        out_specs=pl.BlockSpec((None, 1, A_WIDTH), lambda b, pt: (b, 0, 0)),
        scratch_shapes=[pltpu.VMEM((lhs_rows, A_KV_LORA), BF16),
                        pltpu.VMEM((n_tok, A_KV_LORA), BF16), pltpu.VMEM((A_HEADS, n_tok), F32),
                        pltpu.VMEM((2, n_past, A_KV_LORA), F32), pltpu.VMEM((2, A_ROPE, n_past), F32),
                        pltpu.SemaphoreType.DMA((2, 2))])
    out = pl.pallas_call(
        functools.partial(_sample_attn_kernel, la=la, n_pages=n_pages, pages_per_block=pages_per_block),
        grid_spec=grid_spec,
        out_shape=jax.ShapeDtypeStruct((db, 1, A_WIDTH), F32),
        compiler_params=_cparams("arbitrary"),
        name="mla_sample_attention",
    )(page_table.reshape(-1), qabs, qpe, wts["ukt"],
      wts["w_uv"], ckv_new.reshape(db, 1, A_KV_LORA), kpe_new.reshape(db, A_ROPE, 1), cache_ckv, cache_kpe_t)
    return out.reshape(db, A_WIDTH)


def _hgrn_lower_bound(lb_all, li):
    e = jnp.exp(lb_all - jnp.max(lb_all, axis=0, keepdims=True))
    smx = e / jnp.sum(e, axis=0, keepdims=True)
    return jnp.sum(smx[1:li + 1], axis=0, keepdims=True) if li > 0 else jnp.zeros_like(smx[:1])


def _hgrn_gates(q, f, lb):
    g = jnp.log(lb + (1.0 - lb) * _sigmoid(f))
    k = (1.0 - lb) * _sigmoid(-f)
    qf = q * _sigmoid(q) * (B_DK ** -0.5)
    return qf, k, g


def _cumsum_rows(x):
    n = x.shape[0]
    row = lax.broadcasted_iota(jnp.int32, x.shape, 0)
    d = 1
    while d < n:
        x = x + jnp.where(row >= d, pltpu.roll(x, d, 0), 0.0)
        d *= 2
    return x


def _bcast_row_in_groups(x, j):
    n, w = x.shape
    x3 = x.reshape(n // SUBLANES, SUBLANES, w)
    return jnp.broadcast_to(x3[:, j:j + 1, :], x3.shape).reshape(n, w)


def _intra_chunk_masks(c):
    tt = lax.broadcasted_iota(jnp.int32, (c, c), 0)
    ss = lax.broadcasted_iota(jnp.int32, (c, c), 1)
    masks = {"diag": tt == ss}
    half = c // 2
    while half >= 1:
        blk = 2 * half
        keep = ((tt % blk) >= half) & ((ss % blk) < half)
        masks[half] = keep & ((tt // blk) == (ss // blk)) if blk < c else keep
        half //= 2
    return masks


def _intra_chunk_att(qf, k, b, masks):
    c = qf.shape[0]
    row = lax.broadcasted_iota(jnp.int32, (c, 1), 0)
    att = jnp.where(masks["diag"], jnp.sum(qf * k, axis=-1, keepdims=True), 0.0)
    half = c // 2
    while half >= 1:
        blk = 2 * half
        if blk >= SUBLANES:
            bref = jnp.concatenate(
                [jnp.broadcast_to(b[m:m + 1], (blk, b.shape[1])) for m in range(half, c, blk)], axis=0)
        else:
            rm = row % SUBLANES
            bref = _bcast_row_in_groups(b, SUBLANES - half)
            for m in range(SUBLANES - half - blk, 0, -blk):
                bref = jnp.where(rm < m + half, _bcast_row_in_groups(b, m), bref)
        e = jnp.exp(-jnp.abs(b - bref))
        p = _dot_nt((qf * e).astype(BF16), (k * e).astype(BF16))
        att = att + jnp.where(masks[half], p, 0.0)
        half //= 2
    return att


def _hgrn_prompt_kernel(q_ref, f_ref, v_ref, lb_ref, o_ref, st_ref, state_scr, *, li):
    t = pl.program_id(2)

    @pl.when(t == 0)
    def _():
        state_scr[...] = jnp.zeros(state_scr.shape, F32)

    lb = _hgrn_lower_bound(lb_ref[...], li)
    masks = _intra_chunk_masks(B_CHUNK)
    for ci in range(q_ref.shape[0] // B_CHUNK):
        sl = slice(ci * B_CHUNK, (ci + 1) * B_CHUNK)
        qf, k, g = _hgrn_gates(q_ref[sl, :], f_ref[sl, :], lb)
        v = v_ref[sl, :]
        b = _cumsum_rows(g)
        vb = v.astype(BF16)
        intra = _dot(_intra_chunk_att(qf, k, b, masks).astype(BF16), vb)
        blast = b[B_CHUNK - 1:B_CHUNK]
        kd = (k * jnp.exp(blast - b)).astype(BF16)
        upd = lax.dot_general(vb, kd, (((0,), (0,)), ((), ())), preferred_element_type=F32)
        st = state_scr[...]
        o_ref[sl, :] = intra + _dot_nt((qf * jnp.exp(b)).astype(BF16), st.astype(BF16))
        state_scr[...] = jnp.exp(blast) * st + upd

    @pl.when(t == pl.num_programs(2) - 1)
    def _():
        st_ref[...] = state_scr[...].T


def hgrn_prompt(proj, lower_bounds, li, batch, seq, tblk):
    nt = seq // tblk
    col = lambda off: pl.BlockSpec((tblk, LANES), lambda b, h, t: (b * nt + t, off + h))
    return pl.pallas_call(
        functools.partial(_hgrn_prompt_kernel, li=li),
        grid=(batch, B_HEADS, nt),
        in_specs=[col(0), col(B_HEADS), col(2 * B_HEADS),
                  pl.BlockSpec((lower_bounds.shape[0], LANES), lambda b, h, t: (0, h))],
        out_specs=[pl.BlockSpec((tblk, LANES), lambda b, h, t: (b * nt + t, h)),
                   pl.BlockSpec((None, None, B_DK, B_DV), lambda b, h, t: (b, h, 0, 0))],
        out_shape=[jax.ShapeDtypeStruct((batch * seq, B_HEADS * B_DV), F32),
                   jax.ShapeDtypeStruct((batch, B_HEADS, B_DK, B_DV), F32)],
        scratch_shapes=[pltpu.VMEM((B_DV, B_DK), F32)],
        compiler_params=_cparams("parallel", "parallel", "arbitrary"),
        name="hgrn_prompt",
    )(proj, proj, proj, lower_bounds)


def _hgrn_sample_gates_kernel(p_ref, lb_ref, qe_ref, eg_ref, k_ref, av_ref, *, li):
    w = B_HEADS * B_DK
    lb = _hgrn_lower_bound(lb_ref[...], li)
    qf, k, g = _hgrn_gates(p_ref[:, :w], p_ref[:, w:2 * w], lb)
    v = p_ref[:, 2 * w:3 * w]
    eg = jnp.exp(g)
    qe_ref[...] = qf * eg
    eg_ref[...] = eg
    k_ref[...] = k
    qk = qf * k
    av_ref[...] = jnp.concatenate(
        [jnp.sum(qk[:, h * LANES:(h + 1) * LANES], axis=-1, keepdims=True) * v[:, h * LANES:(h + 1) * LANES]
         for h in range(B_HEADS)], axis=1)


def _hgrn_sample_state_kernel(st_ref, qe_ref, egt_ref, kt_ref, v_ref, av_ref, o_ref, ns_ref):
    for s in range(st_ref.shape[0]):
        for h in range(B_HEADS):
            st = st_ref[s, h]
            ns_ref[s, h] = egt_ref[s, :, h:h + 1] * st + kt_ref[s, :, h:h + 1] * v_ref[s, h:h + 1, :]
            qe = jnp.broadcast_to(qe_ref[s, h:h + 1, :], (2 * SUBLANES, B_DK)).astype(BF16)
            o_ref[s, h:h + 1, :] = _dot(qe, st.astype(BF16))[:1] + av_ref[s, h:h + 1, :]


def hgrn_sample(proj_s, lower_bounds, li, state):
    db = proj_s.shape[0]
    w = B_HEADS * B_DK
    whole = lambda a: pl.BlockSpec(a.shape, lambda i: (0,) * a.ndim)
    qe, eg, k, av = pl.pallas_call(
        functools.partial(_hgrn_sample_gates_kernel, li=li),
        grid=(1,),
        in_specs=[whole(proj_s), whole(lower_bounds)],
        out_specs=[pl.BlockSpec((db, w), lambda i: (0, 0))] * 4,
        out_shape=[jax.ShapeDtypeStruct((db, w), F32)] * 4,
        compiler_params=_cparams("arbitrary"),
        name="hgrn_sample_gates",
    )(proj_s, lower_bounds)
    heads = lambda a: a.reshape(db, B_HEADS, B_DK)
    cols = lambda a: heads(a).transpose(0, 2, 1)
    sb = SEQS_HGRN_SAMPLE
    hrow = pl.BlockSpec((sb, B_HEADS, B_DK), lambda b: (b, 0, 0))
    hcol = pl.BlockSpec((sb, B_DK, B_HEADS), lambda b: (b, 0, 0))
    stspec = pl.BlockSpec((sb, B_HEADS, B_DK, B_DV), lambda b: (b, 0, 0, 0))
    o, new_state = pl.pallas_call(
        _hgrn_sample_state_kernel,
        grid=(db // sb,),
        in_specs=[stspec, hrow, hcol, hcol, hrow, hrow],
        out_specs=[hrow, stspec],
        out_shape=[jax.ShapeDtypeStruct((db, B_HEADS, B_DV), F32),
                   jax.ShapeDtypeStruct(state.shape, F32)],
        compiler_params=_cparams("parallel"),
        name="hgrn_sample_state",
    )(state, heads(qe), cols(eg), cols(k), heads(proj_s[:, 2 * w:3 * w]), heads(av))
    return o.reshape(db, w), new_state


def _rglru_gates(y, wax_ref, ba, bx, lam):
    yb = y.astype(BF16)
    rs, is_ = [], []
    for n in range(C_HEADS):
        ax = _dot(yb[:, n * C_BW:(n + 1) * C_BW], wax_ref[n])
        rs.append(ax[:, :C_BW])
        is_.append(ax[:, C_BW:])
    r = _sigmoid(jnp.concatenate(rs, axis=1) + ba)
    i = _sigmoid(jnp.concatenate(is_, axis=1) + bx)
    softplus = jnp.maximum(-lam, 0.0) + jnp.log1p(jnp.exp(-jnp.abs(lam)))
    log_a = -C_GATE_C * r * softplus
    a = jnp.exp(log_a)
    th = jnp.tanh(log_a)
    u = jnp.sqrt(-2.0 * th / (1.0 - th)) * (i * y)
    return a, u


def _rglru_prompt_kernel(xb_ref, cw_ref, cb_ref, wax_ref, ba_ref, bx_ref, lam_ref,
                         hs_ref, tail_ref, hl_ref, prev_scr, h_scr, a_scr, u_scr):
    t = pl.program_id(1)
    tb = xb_ref.shape[0]

    @pl.when(t == 0)
    def _():
        prev_scr[...] = jnp.zeros(prev_scr.shape, F32)
        h_scr[...] = jnp.zeros(h_scr.shape, F32)

    xb = xb_ref[...]
    prev = prev_scr[...]
    row = lax.broadcasted_iota(jnp.int32, (tb, 1), 0)
    y = cb_ref[...] + cw_ref[C_CONV - 1:C_CONV, :] * xb
    for d in range(1, C_CONV):
        sh = pltpu.roll(xb, d, 0)
        for r in range(d):
            sh = jnp.where(row == r, prev[SUBLANES - d + r:SUBLANES - d + r + 1, :], sh)
        y = y + cw_ref[C_CONV - 1 - d:C_CONV - d, :] * sh
    prev_scr[...] = xb[tb - SUBLANES:, :]
    a, u = _rglru_gates(y, wax_ref, ba_ref[...], bx_ref[...], lam_ref[...])
    a_scr[...] = a
    u_scr[...] = u

    def body(i, h):
        base = pl.multiple_of(i * SUBLANES, SUBLANES)
        for r in range(SUBLANES):
            h = a_scr[pl.ds(base + r, 1), :] * h + u_scr[pl.ds(base + r, 1), :]
            hs_ref[pl.ds(base + r, 1), :] = h
        return h

    h = lax.fori_loop(0, tb // SUBLANES, body, h_scr[...])
    h_scr[...] = h

    @pl.when(t == pl.num_programs(1) - 1)
    def _():
        tail_ref[...] = xb[tb - SUBLANES:, :]
        hl_ref[...] = h


def _rglru_weights(conv_w, conv_b, w_a, b_a, w_x, b_x, lam):
    r1 = lambda a: a.reshape(1, -1)
    return {"cw": conv_w, "cb": r1(conv_b), "wax": jnp.concatenate([w_a, w_x], axis=2).astype(BF16),
            "ba": r1(b_a), "bx": r1(b_x), "lam": r1(lam)}


def rglru_prompt(proj, wts, batch, seq, tblk):
    nt = seq // tblk
    w = C_WIDTH
    full = lambda a: pl.BlockSpec(a.shape, lambda b, t: (0,) * a.ndim)
    names = ("cw", "cb", "wax", "ba", "bx", "lam")
    return pl.pallas_call(
        _rglru_prompt_kernel,
        grid=(batch, nt),
        in_specs=[pl.BlockSpec((tblk, w), lambda b, t: (b * nt + t, 0))] + [full(wts[n]) for n in names],
        out_specs=[pl.BlockSpec((tblk, w), lambda b, t: (b * nt + t, 0)),
                   pl.BlockSpec((None, SUBLANES, w), lambda b, t: (b, 0, 0)),
                   pl.BlockSpec((None, 1, w), lambda b, t: (b, 0, 0))],
        out_shape=[jax.ShapeDtypeStruct((batch * seq, w), F32),
                   jax.ShapeDtypeStruct((batch, SUBLANES, w), F32),
                   jax.ShapeDtypeStruct((batch, 1, w), F32)],
        scratch_shapes=[pltpu.VMEM((SUBLANES, w), F32), pltpu.VMEM((1, w), F32),
                        pltpu.VMEM((tblk, w), F32), pltpu.VMEM((tblk, w), F32)],
        compiler_params=_cparams("parallel", "arbitrary"),
        name="rglru_prompt",
    )(proj, *[wts[n] for n in names])


def _rglru_sample_kernel(xb_ref, buf_ref, h0_ref, cw_ref, cb_ref, wax_ref, ba_ref, bx_ref, lam_ref,
                         hs_ref, nbuf_ref):
    xb = xb_ref[...]
    y = cb_ref[...] + cw_ref[C_CONV - 1:C_CONV, :] * xb
    for j in range(C_CONV - 1):
        y = y + cw_ref[j:j + 1, :] * buf_ref[j]
    a, u = _rglru_gates(y, wax_ref, ba_ref[...], bx_ref[...], lam_ref[...])
    hs_ref[...] = a * h0_ref[...] + u
    for j in range(C_CONV - 2):
        nbuf_ref[j] = buf_ref[j + 1]
    nbuf_ref[C_CONV - 2] = xb


def rglru_sample(proj_s, buf_t, h0, wts):
    db = proj_s.shape[0]
    w = C_WIDTH
    whole = lambda a: pl.BlockSpec(a.shape, lambda i: (0,) * a.ndim)
    names = ("cw", "cb", "wax", "ba", "bx", "lam")
    return pl.pallas_call(
        _rglru_sample_kernel,
        grid=(1,),
        in_specs=[pl.BlockSpec((db, w), lambda i: (0, 0)), whole(buf_t), whole(h0)] + [whole(wts[n]) for n in names],
        out_specs=[pl.BlockSpec((db, w), lambda i: (0, 0)), whole(buf_t)],
        out_shape=[jax.ShapeDtypeStruct((db, w), F32), jax.ShapeDtypeStruct(buf_t.shape, F32)],
        compiler_params=_cparams("arbitrary"),
        name="rglru_sample",
    )(proj_s, buf_t, h0, *[wts[n] for n in names])


TM_PROMPT = 512
TQ_FLASH = 512
T_HGRN = 1024
T_RGLRU = 256
PAGES_PER_BLOCK = 16
SEQS_HGRN_SAMPLE = 4

def kernel(x_prompt, x_sample, cache_ckv, cache_kpe, page_table, state_hgrn, state_conv, state_lru, norm_g, mla_w_in, mla_g_q_lora, mla_g_kv, mla_w_uq, mla_w_ukv, mla_g_q, mla_g_k, mla_w_out, hgrn_w_in, hgrn_lower_bounds, hgrn_g_norm, hgrn_w_out, rglru_w_in, rglru_conv_w, rglru_conv_b, rglru_w_a, rglru_b_a, rglru_w_x, rglru_b_x, rglru_L, rglru_w_out):
    bsz, s_p, d = x_prompt.shape
    db, s_s, _ = x_sample.shape
    assert s_s == 1 and d == D_MODEL
    depth = norm_g.shape[0]
    past_len = page_table.shape[1] * PAGE_SIZE
    xp = x_prompt.reshape(bsz * s_p, d)
    xs = x_sample.reshape(db, d)
    tabs_p = _rope_tables(jnp.arange(s_p))
    tabs_s = _rope_tables(jnp.full((db,), past_len))
    cache_kpe_t = cache_kpe.transpose(0, 1, 3, 2)

    ckv_p, kpe_p, ckv_s, kpe_s = [], [], [], []
    hg_p, hg_s, cv_p, cv_s, lr_p, lr_s = [], [], [], [], [], []
    for li in range(depth):
        kind, j = li % N_MIXERS, li // N_MIXERS
        if kind == 0:
            wts = _mla_weights(mla_w_in[j], mla_g_q_lora[j], mla_g_kv[j], mla_w_uq[j], mla_w_ukv[j],
                               mla_g_q[j], mla_g_k[j])
            w_out = mla_w_out[j].astype(BF16)
            q, k, vt, gate, ckv, kpe_blk = mla_proj(xp, norm_g[li], wts, tabs_p, s_p, TM_PROMPT, BF16, "g_q_base2")
            o = flash_prompt(q, k, vt, bsz, s_p, TQ_FLASH)
            xp = gated_out(o, gate, 0, w_out, xp, TM_PROMPT)
            ckv_p.append(ckv.reshape(bsz, s_p, A_KV_LORA))
            kpe_p.append(kpe_blk[:, A_NOPE:A_QK].reshape(bsz, s_p, A_ROPE))

            q_s, _, _, gate_s, ckv_n, kpe_blk_s = mla_proj(xs, norm_g[li], wts, tabs_s, db, db, F32, "g_q")
            kpe_n = kpe_blk_s[:, A_NOPE:A_QK]
            o_s = mla_sample_attention(q_s, ckv_n, kpe_n, cache_ckv, cache_kpe_t, j, page_table, wts,
                                       PAGES_PER_BLOCK)
            xs = gated_out(o_s, gate_s, 0, w_out, xs, db)
            ckv_s.append(ckv_n.reshape(db, 1, A_KV_LORA))
            kpe_s.append(kpe_n.reshape(db, 1, A_ROPE))
        elif kind == 1:
            w_in = hgrn_w_in[j].astype(BF16)
            w_out = hgrn_w_out[j].astype(BF16)
            proj = norm_matmul(xp, norm_g[li], w_in, TM_PROMPT // 2)
            o, st = hgrn_prompt(proj, hgrn_lower_bounds, li, bsz, s_p, T_HGRN)
            xp = gated_out(o, proj, 3, w_out, xp, TM_PROMPT, hgrn_g_norm[j])
            hg_p.append(st)

            proj_s = norm_matmul(xs, norm_g[li], w_in, db)
            o_s, st_s = hgrn_sample(proj_s, hgrn_lower_bounds, li, state_hgrn[j])
            xs = gated_out(o_s, proj_s, 3, w_out, xs, db, hgrn_g_norm[j])
            hg_s.append(st_s)
        else:
            w_in = rglru_w_in[j].astype(BF16)
            w_out = rglru_w_out[j].astype(BF16)
            wts = _rglru_weights(rglru_conv_w[j], rglru_conv_b[j], rglru_w_a[j], rglru_b_a[j],
                                 rglru_w_x[j], rglru_b_x[j], rglru_L[j])
            proj = norm_matmul(xp, norm_g[li], w_in, TM_PROMPT)
            hs, tail, hl = rglru_prompt(proj, wts, bsz, s_p, T_RGLRU)
            xp = gated_out(hs, proj, 1, w_out, xp, TM_PROMPT)
            cv_p.append(tail[:, SUBLANES - (C_CONV - 1):, :])
            lr_p.append(hl.reshape(bsz, C_WIDTH))

            proj_s = norm_matmul(xs, norm_g[li], w_in, db)
            hs_s, nbuf = rglru_sample(proj_s, state_conv[j].transpose(1, 0, 2), state_lru[j], wts)
            xs = gated_out(hs_s, proj_s, 1, w_out, xs, db)
            cv_s.append(nbuf.transpose(1, 0, 2))
            lr_s.append(hs_s)

    return (xp.reshape(bsz, s_p, d), xs.reshape(db, 1, d),
            jnp.stack(ckv_p), jnp.stack(kpe_p), jnp.stack(ckv_s), jnp.stack(kpe_s),
            jnp.stack(hg_p), jnp.stack(hg_s), jnp.stack(cv_p), jnp.stack(cv_s),
            jnp.stack(lr_p), jnp.stack(lr_s))
```

```python
import functools
import math

import jax
import jax.numpy as jnp
from jax import lax
from jax.experimental import pallas as pl
from jax.experimental.pallas import tpu as pltpu

F32 = jnp.float32
BF16 = jnp.bfloat16

LANES = 128
SUBLANES = 8
VMEM_LIMIT_BYTES = 56 * 1024 * 1024

D_MODEL = 1024
PAGE_SIZE = 128
N_MIXERS = 3
EPS = 1e-6
ROPE_THETA = 10000.0
NEG_BIG = -1e30

A_HEADS = 8
A_NOPE = 64
A_ROPE = 32
A_QK = A_NOPE + A_ROPE
A_V = 64
A_Q_LORA = 384
A_KV_LORA = 256
A_WIDTH = A_HEADS * A_V
A_HALF = A_ROPE // 2
A_PADW = A_HEADS * LANES
A_VL = A_V + 2 * SUBLANES

B_HEADS = 8
B_DK = 128
B_DV = 128
B_CHUNK = 64

C_WIDTH = 1024
C_HEADS = 8
C_BW = 128
C_CONV = 4
C_GATE_C = 8.0


def _cparams(*sem):
    return pltpu.CompilerParams(dimension_semantics=sem, vmem_limit_bytes=VMEM_LIMIT_BYTES)


def _sigmoid(x):
    return 1.0 / (1.0 + jnp.exp(-x))


def _rms_rows(x, g):
    ms = jnp.mean(x * x, axis=-1, keepdims=True)
    return x * lax.rsqrt(ms + EPS) * g


def _dot_nt(a, b):
    return lax.dot_general(a, b, (((1,), (1,)), ((), ())), preferred_element_type=F32)


def _dot(a, b):
    return jnp.dot(a, b, preferred_element_type=F32)


def _norm_matmul_kernel(x_ref, g_ref, w_ref, o_ref):
    h = _rms_rows(x_ref[...], g_ref[...])
    o_ref[...] = _dot(h.astype(BF16), w_ref[...])


def norm_matmul(x, g, w_bf16, tm):
    m, d = x.shape
    n = w_bf16.shape[1]
    return pl.pallas_call(
        _norm_matmul_kernel,
        grid=(m // tm,),
        in_specs=[pl.BlockSpec((tm, d), lambda i: (i, 0)),
                  pl.BlockSpec((1, d), lambda i: (0, 0)),
                  pl.BlockSpec((d, n), lambda i: (0, 0))],
        out_specs=pl.BlockSpec((tm, n), lambda i: (i, 0)),
        out_shape=jax.ShapeDtypeStruct((m, n), F32),
        compiler_params=_cparams("parallel"),
        name="norm_matmul",
    )(x, g.reshape(1, d), w_bf16)


def _gated_out_kernel(o_ref, gate_ref, w_ref, x_ref, gn_ref, y_ref, *, head_norm):
    o = o_ref[...]
    if head_norm:
        parts = []
        for h in range(o.shape[1] // LANES):
            parts.append(_rms_rows(o[:, h * LANES:(h + 1) * LANES], gn_ref[...]))
        o = jnp.concatenate(parts, axis=1)
    gate = gate_ref[...]
    z = o * (gate * _sigmoid(gate))
    y_ref[...] = x_ref[...] + _dot(z.astype(BF16), w_ref[...])


def gated_out(o, gate_arr, gate_col, w_bf16, x, tm, g_norm=None):
    m, w = o.shape
    d = x.shape[1]
    head_norm = g_norm is not None
    gn = (g_norm if head_norm else jnp.ones((LANES,), F32)).reshape(1, LANES)
    return pl.pallas_call(
        functools.partial(_gated_out_kernel, head_norm=head_norm),
        grid=(m // tm,),
        in_specs=[pl.BlockSpec((tm, w), lambda i: (i, 0)),
                  pl.BlockSpec((tm, w), lambda i: (i, gate_col)),
                  pl.BlockSpec((w, d), lambda i: (0, 0)),
                  pl.BlockSpec((tm, d), lambda i: (i, 0)),
                  pl.BlockSpec((1, LANES), lambda i: (0, 0))],
        out_specs=pl.BlockSpec((tm, d), lambda i: (i, 0)),
        out_shape=jax.ShapeDtypeStruct((m, d), F32),
        compiler_params=_cparams("parallel"),
        name="gated_out",
    )(o, gate_arr, w_bf16, x, gn)


def _rope_lanes(x, c, s1, s2):
    n = x.shape[1]
    return x * c + pltpu.roll(x, n - A_HALF, 1) * s1 + pltpu.roll(x, A_HALF, 1) * s2


def _head_norm_blocks(x, g):
    parts = []
    for h in range(A_HEADS):
        blk = x[:, h * LANES:(h + 1) * LANES]
        ss = jnp.sum(blk * blk, axis=-1, keepdims=True)
        parts.append(blk * lax.rsqrt(ss * (1.0 / A_QK) + EPS) * g[:, h * LANES:(h + 1) * LANES])
    return jnp.concatenate(parts, axis=1)


def _mla_proj_kernel(x_ref, gn_ref, win_ref, gql_ref, gkv_ref, wuq_ref, gq_ref, wuk_ref, wuvt_ref, gk_ref,
                     c_ref, s1_ref, s2_ref,
                     q_ref, k_ref, vt_ref, gate_ref, ckv_ref, kpe_ref):
    h = _rms_rows(x_ref[...], gn_ref[...])
    p = _dot(h.astype(BF16), win_ref[...])
    gate_ref[...] = p[:, :A_WIDTH]
    cq = p[:, A_WIDTH:A_WIDTH + A_Q_LORA]
    ckv = p[:, A_WIDTH + A_Q_LORA:A_WIDTH + A_Q_LORA + A_KV_LORA]
    kpe = p[:, A_WIDTH + A_Q_LORA + A_KV_LORA:]
    c, s1, s2 = c_ref[...], s1_ref[...], s2_ref[...]
    c8 = jnp.concatenate([c] * A_HEADS, axis=1)
    sin8 = jnp.concatenate([s2 - s1] * A_HEADS, axis=1)

    q2 = _dot(_rms_rows(cq, gql_ref[...]).astype(BF16), wuq_ref[...])
    q = q2[:, :A_PADW] * c8 + q2[:, A_PADW:] * sin8
    q_ref[...] = _head_norm_blocks(q, gq_ref[...]).astype(q_ref.dtype)

    ckvn = _rms_rows(ckv, gkv_ref[...])
    ckv_ref[...] = ckvn
    kper = _rope_lanes(kpe, c, s1, s2)
    kpe_ref[...] = kper
    cb = ckvn.astype(BF16)
    kn = _dot(cb, wuk_ref[...]) + jnp.concatenate([kper] * A_HEADS, axis=1)
    k_ref[...] = _head_norm_blocks(kn, gk_ref[...]).astype(k_ref.dtype)
    vt = _dot_nt(wuvt_ref[...], cb)
    ones = jnp.ones((A_VL - A_V, vt.shape[1]), F32)
    vt_ref[...] = jnp.concatenate(
        [blk for h in range(A_HEADS) for blk in (vt[h * A_V:(h + 1) * A_V, :], ones)], axis=0).astype(vt_ref.dtype)


def mla_proj(x, g_norm, wts, rope_tabs, rows_per_seq, tm, qkv_dtype, q_gain):
    m, d = x.shape
    c, s1, s2 = rope_tabs
    nblk = rows_per_seq // tm
    full = lambda a: pl.BlockSpec(a.shape, lambda i: (0, 0))
    tab = pl.BlockSpec((tm, LANES), lambda i: (i % nblk, 0))
    row = lambda n: pl.BlockSpec((tm, n), lambda i: (i, 0))
    return pl.pallas_call(
        _mla_proj_kernel,
        grid=(m // tm,),
        in_specs=[row(d), pl.BlockSpec((1, d), lambda i: (0, 0)), full(wts["w_in"]),
                  full(wts["g_ql"]), full(wts["g_kv"]), full(wts["w_uq"]), full(wts[q_gain]),
                  full(wts["w_uk"]), full(wts["w_uvt"]), full(wts["g_k"]), tab, tab, tab],
        out_specs=[row(A_PADW), row(A_PADW), pl.BlockSpec((A_HEADS * A_VL, tm), lambda i: (0, i)),
                   row(A_WIDTH), row(A_KV_LORA), row(LANES)],
        out_shape=[jax.ShapeDtypeStruct((m, A_PADW), qkv_dtype),
                   jax.ShapeDtypeStruct((m, A_PADW), qkv_dtype),
                   jax.ShapeDtypeStruct((A_HEADS * A_VL, m), qkv_dtype),
                   jax.ShapeDtypeStruct((m, A_WIDTH), F32),
                   jax.ShapeDtypeStruct((m, A_KV_LORA), F32),
                   jax.ShapeDtypeStruct((m, LANES), F32)],
        compiler_params=_cparams("parallel"),
        name="mla_proj",
    )(x, g_norm.reshape(1, d), wts["w_in"], wts["g_ql"], wts["g_kv"], wts["w_uq"], wts[q_gain],
      wts["w_uk"], wts["w_uvt"], wts["g_k"], c, s1, s2)


def _pad_heads(a, used):
    pad = [(0, 0)] * (a.ndim - 1) + [(0, LANES - used)]
    a = jnp.pad(a, pad)
    return a.reshape(a.shape[:-2] + (a.shape[-2] * LANES,))


def _mla_weights(w_in, g_ql, g_kv, w_uq, w_ukv, g_q, g_k):
    cq, ckv, kpe, gate = jnp.split(w_in, [A_Q_LORA, A_Q_LORA + A_KV_LORA, A_Q_LORA + A_KV_LORA + A_ROPE], axis=1)
    kpe_blk = jnp.pad(kpe, ((0, 0), (A_NOPE, LANES - A_QK)))
    scale = A_QK ** -0.5
    ukv = w_ukv.reshape(A_KV_LORA, A_HEADS, A_NOPE + A_V)
    uk, uv = ukv[:, :, :A_NOPE], ukv[:, :, A_NOPE:]
    ukt_pad = _pad_heads(uk, A_NOPE).T
    ukt_hi = ukt_pad.astype(BF16)
    uq = w_uq.reshape(A_Q_LORA, A_HEADS, A_QK)
    x1, x2 = uq[:, :, A_NOPE:A_NOPE + A_HALF], uq[:, :, A_NOPE + A_HALF:]
    uq_rot = jnp.concatenate([jnp.zeros_like(uq[:, :, :A_NOPE]), -x2, x1], axis=2)
    return {
        "w_in": jnp.concatenate([gate, cq, ckv, kpe_blk], axis=1).astype(BF16),
        "g_ql": g_ql.reshape(1, -1), "g_kv": g_kv.reshape(1, -1),
        "w_uq": jnp.concatenate([_pad_heads(uq, A_QK), _pad_heads(uq_rot, A_QK)], axis=1).astype(BF16),
        "g_q": _pad_heads(jnp.broadcast_to(g_q * scale, (A_HEADS, A_QK)), A_QK).reshape(1, -1),
        "g_q_base2": _pad_heads(jnp.broadcast_to(g_q * (scale * math.log2(math.e)), (A_HEADS, A_QK)),
                                A_QK).reshape(1, -1),
        "g_k": _pad_heads(jnp.broadcast_to(g_k, (A_HEADS, A_QK)), A_QK).reshape(1, -1),
        "w_uk": _pad_heads(uk, A_NOPE).astype(BF16),
        "w_uv": uv.reshape(A_KV_LORA, A_WIDTH).astype(BF16),
        "w_uvt": uv.reshape(A_KV_LORA, A_WIDTH).T.astype(BF16),
        "ukt": uk.reshape(A_KV_LORA, A_HEADS * A_NOPE).T.astype(BF16),
        "ukt_hi": ukt_hi,
        "ukt_lo": (ukt_pad - ukt_hi.astype(F32)).astype(BF16),
    }


def _rope_tables(pos):
    inv = 1.0 / (ROPE_THETA ** (jnp.arange(0, A_ROPE, 2, dtype=F32) / A_ROPE))
    ang = pos.astype(F32)[:, None] * inv[None, :]
    cos, sin = jnp.cos(ang), jnp.sin(ang)
    n = pos.shape[0]
    z = lambda w: jnp.zeros((n, w), F32)
    c = jnp.concatenate([jnp.ones((n, A_NOPE), F32), cos, cos, z(LANES - A_QK)], axis=1)
    s1 = jnp.concatenate([z(A_NOPE), -sin, z(A_HALF), z(LANES - A_QK)], axis=1)
    s2 = jnp.concatenate([z(A_NOPE + A_HALF), sin, z(LANES - A_QK)], axis=1)
    return c, s1, s2


def _flash_kernel(qi_ref, kj_ref, q_ref, k_ref, vt_ref, gate_ref, x_ref, wout_ref, y_ref, m_scr, acc_scr):
    p_id = pl.program_id(1)
    qi = qi_ref[p_id]
    kj = kj_ref[p_id]
    tq, tk = q_ref.shape[0], k_ref.shape[0]

    @pl.when(kj == 0)
    def _():
        m_scr[...] = jnp.full(m_scr.shape, NEG_BIG, F32)
        acc_scr[...] = jnp.zeros(acc_scr.shape, F32)

    def step(masked):
        if masked:
            key = lax.broadcasted_iota(jnp.int32, (tk, tq), 0)
            qry = lax.broadcasted_iota(jnp.int32, (tk, tq), 1)
            keep = key <= qry

        def scores(h):
            return _dot_nt(k_ref[:, h * LANES:(h + 1) * LANES], q_ref[:, h * LANES:(h + 1) * LANES])

        st_next = scores(0)
        for h in range(A_HEADS):
            st = st_next
            if h + 1 < A_HEADS:
                st_next = scores(h + 1)
            if masked:
                st = jnp.where(keep, st, NEG_BIG)
            m_prev = m_scr[h]
            m_new = jnp.maximum(m_prev, jnp.max(st, axis=0, keepdims=True))
            p = jnp.exp2(st - m_new)
            corr = jnp.exp2(m_prev - m_new)
            rows = slice(h * A_VL, (h + 1) * A_VL)
            acc_scr[rows, :] = acc_scr[rows, :] * corr + _dot(vt_ref[rows, :], p.astype(BF16))
            m_scr[h] = m_new

    @pl.when(kj < qi)
    def _():
        step(False)

    @pl.when(kj == qi)
    def _():
        step(True)
        ot = jnp.concatenate([acc_scr[h * A_VL:h * A_VL + A_V, :] / acc_scr[h * A_VL + A_V:h * A_VL + A_V + 1, :]
                              for h in range(A_HEADS)], axis=0)
        gate = gate_ref[...]
        z = ot.T * (gate * _sigmoid(gate))
        y_ref[...] = x_ref[...] + _dot(z.astype(BF16), wout_ref[...])


def flash_prompt(q, k, vt, gate, x, w_out_bf16, batch, seq, tq):
    nq = seq // tq
    d = x.shape[1]
    pairs = [(i, j) for i in range(nq) for j in range(i + 1)]
    qi = jnp.asarray([p[0] for p in pairs], jnp.int32)
    kj = jnp.asarray([p[1] for p in pairs], jnp.int32)
    qrow = lambda n: pl.BlockSpec((tq, n), lambda b, p, qi, kj: (b * nq + qi[p], 0))
    grid_spec = pltpu.PrefetchScalarGridSpec(
        num_scalar_prefetch=2,
        grid=(batch, len(pairs)),
        in_specs=[qrow(A_PADW),
                  pl.BlockSpec((tq, A_PADW), lambda b, p, qi, kj: (b * nq + kj[p], 0)),
                  pl.BlockSpec((A_HEADS * A_VL, tq), lambda b, p, qi, kj: (0, b * nq + kj[p])),
                  qrow(A_WIDTH), qrow(d), pl.BlockSpec(w_out_bf16.shape, lambda b, p, qi, kj: (0, 0))],
        out_specs=qrow(d),
        scratch_shapes=[pltpu.VMEM((A_HEADS, 1, tq), F32), pltpu.VMEM((A_HEADS * A_VL, tq), F32)])
    return pl.pallas_call(
        _flash_kernel, grid_spec=grid_spec,
        out_shape=jax.ShapeDtypeStruct((batch * seq, d), F32),
        compiler_params=_cparams("parallel", "arbitrary"),
        name="flash_prompt",
    )(qi, kj, q, k, vt, gate, x, w_out_bf16)


def _absorb_query_kernel(q_ref, gk_ref, ukhi_ref, uklo_ref, qg_ref, qabs_ref):
    qg = q_ref[...] * gk_ref[...]
    qg_ref[...] = qg
    for h in range(A_HEADS):
        blk = qg[:, h * LANES:(h + 1) * LANES]
        q_hi = blk.astype(BF16)
        q_lo = (blk - q_hi.astype(F32)).astype(BF16)
        w_hi = ukhi_ref[h * LANES:(h + 1) * LANES, :]
        qabs_ref[h] = _dot(q_hi, w_hi) + _dot(q_hi, uklo_ref[h * LANES:(h + 1) * LANES, :]) + _dot(q_lo, w_hi)


def absorb_query(q_s, wts):
    db = q_s.shape[0]
    whole = lambda a: pl.BlockSpec(a.shape, lambda i: (0,) * a.ndim)
    return pl.pallas_call(
        _absorb_query_kernel,
        grid=(1,),
        in_specs=[whole(q_s), whole(wts["g_k"]), whole(wts["ukt_hi"]), whole(wts["ukt_lo"])],
        out_specs=[pl.BlockSpec((db, A_PADW), lambda i: (0, 0)),
                   pl.BlockSpec((A_HEADS, db, A_KV_LORA), lambda i: (0, 0, 0))],
        out_shape=[jax.ShapeDtypeStruct((db, A_PADW), F32),
                   jax.ShapeDtypeStruct((A_HEADS, db, A_KV_LORA), F32)],
        compiler_params=_cparams("arbitrary"),
        name="absorb_query",
    )(q_s, wts["g_k"], wts["ukt_hi"], wts["ukt_lo"])


def _sample_attn_kernel(pt_ref, qabs_ref, qpe_ref, ukt_ref, wuv_ref, cnew_ref, knew_ref, ckv_hbm, kpe_hbm, o_ref,
                        lhs_scr, cb_scr, s_scr, ckv_buf, kpe_buf, sems, *, la, n_pages, pages_per_block):
    b = pl.program_id(0)
    last = pl.num_programs(0) - 1
    slot = b % 2
    nxt = jnp.minimum(b + 1, last)
    n_nope = A_HEADS * A_NOPE
    n_past = n_pages * PAGE_SIZE

    def page_copies(seq, to_slot, i):
        page = pt_ref[seq * n_pages + i]
        rows = pl.ds(i * PAGE_SIZE, PAGE_SIZE)
        return (pltpu.make_async_copy(ckv_hbm.at[la, page], ckv_buf.at[to_slot, rows, :], sems.at[0, to_slot]),
                pltpu.make_async_copy(kpe_hbm.at[la, page], kpe_buf.at[to_slot, :, rows], sems.at[1, to_slot]))

    def start_pages(seq, to_slot, pages):
        for i in pages:
            for cp in page_copies(seq, to_slot, i):
                cp.start()

    def wait_pages(to_slot):
        pltpu.make_async_copy(ckv_buf.at[to_slot], ckv_buf.at[to_slot], sems.at[0, to_slot]).wait()
        pltpu.make_async_copy(kpe_buf.at[to_slot], kpe_buf.at[to_slot], sems.at[1, to_slot]).wait()

    @pl.when(b == 0)
    def _():
        start_pages(b, slot, range(n_pages))

    wait_pages(slot)
    lhs_scr[:n_nope, :] = ukt_ref[...]
    lhs_scr[n_nope:, :] = qabs_ref[...]

    def score(c, kpe_t, start):
        n = c.shape[0]
        cb = c.astype(BF16)
        cb_scr[start:start + n, :] = cb
        a = _dot_nt(lhs_scr[...], cb)
        kn = a[:n_nope]
        ss = jnp.sum((kn * kn).reshape(A_HEADS, A_NOPE, n), axis=1)
        s_pe = _dot(qpe_ref[...], kpe_t.astype(BF16))[:A_HEADS]
        pe_ss = jnp.sum(kpe_t * kpe_t, axis=0, keepdims=True)
        rs = lax.rsqrt((ss + pe_ss) * (1.0 / A_QK) + EPS)
        return (a[n_nope:n_nope + A_HEADS] + s_pe) * rs

    for blk in range(0, n_pages, pages_per_block):
        start, n = blk * PAGE_SIZE, pages_per_block * PAGE_SIZE
        s_scr[:, start:start + n] = score(ckv_buf[slot, start:start + n, :], kpe_buf[slot, :, start:start + n], start)
        start_pages(nxt, 1 - slot, range(blk, blk + pages_per_block))
    s_new = score(jnp.broadcast_to(cnew_ref[...], (LANES, A_KV_LORA)),
                  jnp.broadcast_to(knew_ref[...], (A_ROPE, LANES)), n_past)
    s_scr[:, n_past:] = jnp.where(lax.broadcasted_iota(jnp.int32, s_new.shape, 1) < 1, s_new, NEG_BIG)

    s = s_scr[...]
    p = jnp.exp(s - jnp.max(s, axis=-1, keepdims=True))
    pb = p.astype(BF16)
    acc = jnp.concatenate([_dot(pb, cb_scr[:, :LANES]), _dot(pb, cb_scr[:, LANES:])], axis=1)
    lat = acc / jnp.sum(p, axis=-1, keepdims=True)
    full = _dot(lat.astype(BF16), wuv_ref[...])
    lane = lax.broadcasted_iota(jnp.int32, full.shape, 1)
    row = lax.broadcasted_iota(jnp.int32, full.shape, 0)
    sel = (lane >= row * A_V) & (lane < (row + 1) * A_V)
    o_ref[...] = jnp.sum(jnp.where(sel, full, 0.0), axis=0, keepdims=True)

    @pl.when(b == last)
    def _():
        wait_pages(1 - slot)


def mla_sample_attention(q_s, ckv_new, kpe_new, cache_ckv, cache_kpe_t, la, page_table, wts, pages_per_block):
    db, n_pages = page_table.shape
    assert n_pages % pages_per_block == 0
    n_past = n_pages * PAGE_SIZE
    n_tok = n_past + LANES
    row3 = lambda r, n: pl.BlockSpec((None, r, n), lambda b, pt: (b, 0, 0))
    full = lambda a: pl.BlockSpec(a.shape, lambda b, pt: (0, 0))
    hbm = pl.BlockSpec(memory_space=pl.ANY)
    q_rows = 2 * SUBLANES
    lhs_rows = A_HEADS * A_NOPE + q_rows
    qg, qabs = absorb_query(q_s, wts)
    pad_rows = lambda a: jnp.pad(a, ((0, 0), (0, q_rows - A_HEADS), (0, 0))).astype(BF16)
    qabs = pad_rows(qabs.transpose(1, 0, 2))
    qpe = pad_rows(qg.reshape(db, A_HEADS, LANES)[:, :, A_NOPE:A_QK])
    grid_spec = pltpu.PrefetchScalarGridSpec(
        num_scalar_prefetch=1,
        grid=(db,),
        in_specs=[row3(q_rows, A_KV_LORA), row3(q_rows, A_ROPE), full(wts["ukt"]),
                  full(wts["w_uv"]), row3(1, A_KV_LORA), row3(A_ROPE, 1), hbm, hbm],
        out_specs=pl.BlockSpec((None, 1, A_WIDTH), lambda b, pt: (b, 0, 0)),
        scratch_shapes=[pltpu.VMEM((lhs_rows, A_KV_LORA), BF16),
                        pltpu.VMEM((n_tok, A_KV_LORA), BF16), pltpu.VMEM((A_HEADS, n_tok), F32),
                        pltpu.VMEM((2, n_past, A_KV_LORA), F32), pltpu.VMEM((2, A_ROPE, n_past), F32),
                        pltpu.SemaphoreType.DMA((2, 2))])
    out = pl.pallas_call(
        functools.partial(_sample_attn_kernel, la=la, n_pages=n_pages, pages_per_block=pages_per_block),
        grid_spec=grid_spec,
        out_shape=jax.ShapeDtypeStruct((db, 1, A_WIDTH), F32),
        compiler_params=_cparams("arbitrary"),
        name="mla_sample_attention",
    )(page_table.reshape(-1), qabs, qpe, wts["ukt"],
      wts["w_uv"], ckv_new.reshape(db, 1, A_KV_LORA), kpe_new.reshape(db, A_ROPE, 1), cache_ckv, cache_kpe_t)
    return out.reshape(db, A_WIDTH)


def _hgrn_lower_bound(lb_all, li):
    e = jnp.exp(lb_all - jnp.max(lb_all, axis=0, keepdims=True))
    smx = e / jnp.sum(e, axis=0, keepdims=True)
    return jnp.sum(smx[1:li + 1], axis=0, keepdims=True) if li > 0 else jnp.zeros_like(smx[:1])


def _hgrn_gates(q, f, lb):
    g = jnp.log(lb + (1.0 - lb) * _sigmoid(f))
    k = (1.0 - lb) * _sigmoid(-f)
    qf = q * _sigmoid(q) * (B_DK ** -0.5)
    return qf, k, g


def _cumsum_rows(x):
    n = x.shape[0]
    row = lax.broadcasted_iota(jnp.int32, x.shape, 0)
    d = 1
    while d < n:
        x = x + jnp.where(row >= d, pltpu.roll(x, d, 0), 0.0)
        d *= 2
    return x


def _bcast_row_in_groups(x, j):
    n, w = x.shape
    x3 = x.reshape(n // SUBLANES, SUBLANES, w)
    return jnp.broadcast_to(x3[:, j:j + 1, :], x3.shape).reshape(n, w)


def _intra_chunk_masks(c):
    tt = lax.broadcasted_iota(jnp.int32, (c, c), 0)
    ss = lax.broadcasted_iota(jnp.int32, (c, c), 1)
    masks = {"diag": tt == ss}
    half = c // 2
    while half >= 1:
        blk = 2 * half
        keep = ((tt % blk) >= half) & ((ss % blk) < half)
        masks[half] = keep & ((tt // blk) == (ss // blk)) if blk < c else keep
        half //= 2
    return masks


def _intra_chunk_att(qf, k, b, masks):
    c = qf.shape[0]
    row = lax.broadcasted_iota(jnp.int32, (c, 1), 0)
    att = jnp.where(masks["diag"], jnp.sum(qf * k, axis=-1, keepdims=True), 0.0)
    half = c // 2
    while half >= 1:
        blk = 2 * half
        if blk >= SUBLANES:
            bref = jnp.concatenate(
                [jnp.broadcast_to(b[m:m + 1], (blk, b.shape[1])) for m in range(half, c, blk)], axis=0)
        else:
            rm = row % SUBLANES
            bref = _bcast_row_in_groups(b, SUBLANES - half)
            for m in range(SUBLANES - half - blk, 0, -blk):
                bref = jnp.where(rm < m + half, _bcast_row_in_groups(b, m), bref)
        e = jnp.exp(-jnp.abs(b - bref))
        p = _dot_nt((qf * e).astype(BF16), (k * e).astype(BF16))
        att = att + jnp.where(masks[half], p, 0.0)
        half //= 2
    return att


def _hgrn_prompt_kernel(q_ref, f_ref, v_ref, lb_ref, o_ref, st_ref, state_scr, *, li):
    t = pl.program_id(2)

    @pl.when(t == 0)
    def _():
        state_scr[...] = jnp.zeros(state_scr.shape, F32)

    lb = _hgrn_lower_bound(lb_ref[...], li)
    masks = _intra_chunk_masks(B_CHUNK)
    for ci in range(q_ref.shape[0] // B_CHUNK):
        sl = slice(ci * B_CHUNK, (ci + 1) * B_CHUNK)
        qf, k, g = _hgrn_gates(q_ref[sl, :], f_ref[sl, :], lb)
        v = v_ref[sl, :]
        b = _cumsum_rows(g)
        vb = v.astype(BF16)
        intra = _dot(_intra_chunk_att(qf, k, b, masks).astype(BF16), vb)
        blast = b[B_CHUNK - 1:B_CHUNK]
        kd = (k * jnp.exp(blast - b)).astype(BF16)
        upd = lax.dot_general(vb, kd, (((0,), (0,)), ((), ())), preferred_element_type=F32)
        st = state_scr[...]
        o_ref[sl, :] = intra + _dot_nt((qf * jnp.exp(b)).astype(BF16), st.astype(BF16))
        state_scr[...] = jnp.exp(blast) * st + upd

    @pl.when(t == pl.num_programs(2) - 1)
    def _():
        st_ref[...] = state_scr[...].T


def hgrn_prompt(proj, lower_bounds, li, batch, seq, tblk):
    nt = seq // tblk
    col = lambda off: pl.BlockSpec((tblk, LANES), lambda b, h, t: (b * nt + t, off + h))
    return pl.pallas_call(
        functools.partial(_hgrn_prompt_kernel, li=li),
        grid=(batch, B_HEADS, nt),
        in_specs=[col(0), col(B_HEADS), col(2 * B_HEADS),
                  pl.BlockSpec((lower_bounds.shape[0], LANES), lambda b, h, t: (0, h))],
        out_specs=[pl.BlockSpec((tblk, LANES), lambda b, h, t: (b * nt + t, h)),
                   pl.BlockSpec((None, None, B_DK, B_DV), lambda b, h, t: (b, h, 0, 0))],
        out_shape=[jax.ShapeDtypeStruct((batch * seq, B_HEADS * B_DV), F32),
                   jax.ShapeDtypeStruct((batch, B_HEADS, B_DK, B_DV), F32)],
        scratch_shapes=[pltpu.VMEM((B_DV, B_DK), F32)],
        compiler_params=_cparams("parallel", "parallel", "arbitrary"),
        name="hgrn_prompt",
    )(proj, proj, proj, lower_bounds)


def _hgrn_sample_gates_kernel(p_ref, lb_ref, qe_ref, eg_ref, k_ref, av_ref, *, li):
    w = B_HEADS * B_DK
    lb = _hgrn_lower_bound(lb_ref[...], li)
    qf, k, g = _hgrn_gates(p_ref[:, :w], p_ref[:, w:2 * w], lb)
    v = p_ref[:, 2 * w:3 * w]
    eg = jnp.exp(g)
    qe_ref[...] = qf * eg
    eg_ref[...] = eg
    k_ref[...] = k
    qk = qf * k
    av_ref[...] = jnp.concatenate(
        [jnp.sum(qk[:, h * LANES:(h + 1) * LANES], axis=-1, keepdims=True) * v[:, h * LANES:(h + 1) * LANES]
         for h in range(B_HEADS)], axis=1)


def _hgrn_sample_state_kernel(st_ref, qe_ref, egt_ref, kt_ref, v_ref, av_ref, o_ref, ns_ref):
    for s in range(st_ref.shape[0]):
        for h in range(B_HEADS):
            st = st_ref[s, h]
            ns_ref[s, h] = egt_ref[s, :, h:h + 1] * st + kt_ref[s, :, h:h + 1] * v_ref[s, h:h + 1, :]
            qe = jnp.broadcast_to(qe_ref[s, h:h + 1, :], (2 * SUBLANES, B_DK)).astype(BF16)
            o_ref[s, h:h + 1, :] = _dot(qe, st.astype(BF16))[:1] + av_ref[s, h:h + 1, :]


def hgrn_sample(proj_s, lower_bounds, li, state):
    db = proj_s.shape[0]
    w = B_HEADS * B_DK
    whole = lambda a: pl.BlockSpec(a.shape, lambda i: (0,) * a.ndim)
    qe, eg, k, av = pl.pallas_call(
        functools.partial(_hgrn_sample_gates_kernel, li=li),
        grid=(1,),
        in_specs=[whole(proj_s), whole(lower_bounds)],
        out_specs=[pl.BlockSpec((db, w), lambda i: (0, 0))] * 4,
        out_shape=[jax.ShapeDtypeStruct((db, w), F32)] * 4,
        compiler_params=_cparams("arbitrary"),
        name="hgrn_sample_gates",
    )(proj_s, lower_bounds)
    heads = lambda a: a.reshape(db, B_HEADS, B_DK)
    cols = lambda a: heads(a).transpose(0, 2, 1)
    sb = SEQS_HGRN_SAMPLE
    hrow = pl.BlockSpec((sb, B_HEADS, B_DK), lambda b: (b, 0, 0))
    hcol = pl.BlockSpec((sb, B_DK, B_HEADS), lambda b: (b, 0, 0))
    stspec = pl.BlockSpec((sb, B_HEADS, B_DK, B_DV), lambda b: (b, 0, 0, 0))
    o, new_state = pl.pallas_call(
        _hgrn_sample_state_kernel,
        grid=(db // sb,),
        in_specs=[stspec, hrow, hcol, hcol, hrow, hrow],
        out_specs=[hrow, stspec],
        out_shape=[jax.ShapeDtypeStruct((db, B_HEADS, B_DV), F32),
                   jax.ShapeDtypeStruct(state.shape, F32)],
        compiler_params=_cparams("parallel"),
        name="hgrn_sample_state",
    )(state, heads(qe), cols(eg), cols(k), heads(proj_s[:, 2 * w:3 * w]), heads(av))
    return o.reshape(db, w), new_state


def _rglru_gates(y, wax_ref, ba, bx, lam):
    yb = y.astype(BF16)
    rs, is_ = [], []
    for n in range(C_HEADS):
        ax = _dot(yb[:, n * C_BW:(n + 1) * C_BW], wax_ref[n])
        rs.append(ax[:, :C_BW])
        is_.append(ax[:, C_BW:])
    r = _sigmoid(jnp.concatenate(rs, axis=1) + ba)
    i = _sigmoid(jnp.concatenate(is_, axis=1) + bx)
    softplus = jnp.maximum(-lam, 0.0) + jnp.log1p(jnp.exp(-jnp.abs(lam)))
    log_a = -C_GATE_C * r * softplus
    a = jnp.exp(log_a)
    th = jnp.tanh(log_a)
    u = jnp.sqrt(-2.0 * th / (1.0 - th)) * (i * y)
    return a, u


def _rglru_prompt_kernel(xb_ref, cw_ref, cb_ref, wax_ref, ba_ref, bx_ref, lam_ref,
                         hs_ref, tail_ref, hl_ref, prev_scr, h_scr, a_scr, u_scr):
    t = pl.program_id(1)
    tb = xb_ref.shape[0]

    @pl.when(t == 0)
    def _():
        prev_scr[...] = jnp.zeros(prev_scr.shape, F32)
        h_scr[...] = jnp.zeros(h_scr.shape, F32)

    xb = xb_ref[...]
    prev = prev_scr[...]
    row = lax.broadcasted_iota(jnp.int32, (tb, 1), 0)
    y = cb_ref[...] + cw_ref[C_CONV - 1:C_CONV, :] * xb
    for d in range(1, C_CONV):
        sh = pltpu.roll(xb, d, 0)
        for r in range(d):
            sh = jnp.where(row == r, prev[SUBLANES - d + r:SUBLANES - d + r + 1, :], sh)
        y = y + cw_ref[C_CONV - 1 - d:C_CONV - d, :] * sh
    prev_scr[...] = xb[tb - SUBLANES:, :]
    a, u = _rglru_gates(y, wax_ref, ba_ref[...], bx_ref[...], lam_ref[...])
    a_scr[...] = a
    u_scr[...] = u

    def body(i, h):
        base = pl.multiple_of(i * SUBLANES, SUBLANES)
        for r in range(SUBLANES):
            h = a_scr[pl.ds(base + r, 1), :] * h + u_scr[pl.ds(base + r, 1), :]
            hs_ref[pl.ds(base + r, 1), :] = h
        return h

    h = lax.fori_loop(0, tb // SUBLANES, body, h_scr[...])
    h_scr[...] = h

    @pl.when(t == pl.num_programs(1) - 1)
    def _():
        tail_ref[...] = xb[tb - SUBLANES:, :]
        hl_ref[...] = h


def _rglru_weights(conv_w, conv_b, w_a, b_a, w_x, b_x, lam):
    r1 = lambda a: a.reshape(1, -1)
    return {"cw": conv_w, "cb": r1(conv_b), "wax": jnp.concatenate([w_a, w_x], axis=2).astype(BF16),
            "ba": r1(b_a), "bx": r1(b_x), "lam": r1(lam)}


def rglru_prompt(proj, wts, batch, seq, tblk):
    nt = seq // tblk
    w = C_WIDTH
    full = lambda a: pl.BlockSpec(a.shape, lambda b, t: (0,) * a.ndim)
    names = ("cw", "cb", "wax", "ba", "bx", "lam")
    return pl.pallas_call(
        _rglru_prompt_kernel,
        grid=(batch, nt),
        in_specs=[pl.BlockSpec((tblk, w), lambda b, t: (b * nt + t, 0))] + [full(wts[n]) for n in names],
        out_specs=[pl.BlockSpec((tblk, w), lambda b, t: (b * nt + t, 0)),
                   pl.BlockSpec((None, SUBLANES, w), lambda b, t: (b, 0, 0)),
                   pl.BlockSpec((None, 1, w), lambda b, t: (b, 0, 0))],
        out_shape=[jax.ShapeDtypeStruct((batch * seq, w), F32),
                   jax.ShapeDtypeStruct((batch, SUBLANES, w), F32),
                   jax.ShapeDtypeStruct((batch, 1, w), F32)],
        scratch_shapes=[pltpu.VMEM((SUBLANES, w), F32), pltpu.VMEM((1, w), F32),
                        pltpu.VMEM((tblk, w), F32), pltpu.VMEM((tblk, w), F32)],
        compiler_params=_cparams("parallel", "arbitrary"),
        name="rglru_prompt",
    )(proj, *[wts[n] for n in names])


def _rglru_sample_kernel(xb_ref, buf_ref, h0_ref, cw_ref, cb_ref, wax_ref, ba_ref, bx_ref, lam_ref,
                         hs_ref, nbuf_ref):
    xb = xb_ref[...]
    y = cb_ref[...] + cw_ref[C_CONV - 1:C_CONV, :] * xb
    for j in range(C_CONV - 1):
        y = y + cw_ref[j:j + 1, :] * buf_ref[j]
    a, u = _rglru_gates(y, wax_ref, ba_ref[...], bx_ref[...], lam_ref[...])
    hs_ref[...] = a * h0_ref[...] + u
    for j in range(C_CONV - 2):
        nbuf_ref[j] = buf_ref[j + 1]
    nbuf_ref[C_CONV - 2] = xb


def rglru_sample(proj_s, buf_t, h0, wts):
    db = proj_s.shape[0]
    w = C_WIDTH
    whole = lambda a: pl.BlockSpec(a.shape, lambda i: (0,) * a.ndim)
    names = ("cw", "cb", "wax", "ba", "bx", "lam")
    return pl.pallas_call(
        _rglru_sample_kernel,
        grid=(1,),
        in_specs=[pl.BlockSpec((db, w), lambda i: (0, 0)), whole(buf_t), whole(h0)] + [whole(wts[n]) for n in names],
        out_specs=[pl.BlockSpec((db, w), lambda i: (0, 0)), whole(buf_t)],
        out_shape=[jax.ShapeDtypeStruct((db, w), F32), jax.ShapeDtypeStruct(buf_t.shape, F32)],
        compiler_params=_cparams("arbitrary"),
        name="rglru_sample",
    )(proj_s, buf_t, h0, *[wts[n] for n in names])


TM_PROMPT = 512
TQ_FLASH = 512
T_HGRN = 1024
T_RGLRU = 256
PAGES_PER_BLOCK = 16
SEQS_HGRN_SAMPLE = 4

def kernel(x_prompt, x_sample, cache_ckv, cache_kpe, page_table, state_hgrn, state_conv, state_lru, norm_g, mla_w_in, mla_g_q_lora, mla_g_kv, mla_w_uq, mla_w_ukv, mla_g_q, mla_g_k, mla_w_out, hgrn_w_in, hgrn_lower_bounds, hgrn_g_norm, hgrn_w_out, rglru_w_in, rglru_conv_w, rglru_conv_b, rglru_w_a, rglru_b_a, rglru_w_x, rglru_b_x, rglru_L, rglru_w_out):
    bsz, s_p, d = x_prompt.shape
    db, s_s, _ = x_sample.shape
    assert s_s == 1 and d == D_MODEL
    depth = norm_g.shape[0]
    past_len = page_table.shape[1] * PAGE_SIZE
    xp = x_prompt.reshape(bsz * s_p, d)
    xs = x_sample.reshape(db, d)
    tabs_p = _rope_tables(jnp.arange(s_p))
    tabs_s = _rope_tables(jnp.full((db,), past_len))
    cache_kpe_t = cache_kpe.transpose(0, 1, 3, 2)

    ckv_p, kpe_p, ckv_s, kpe_s = [], [], [], []
    hg_p, hg_s, cv_p, cv_s, lr_p, lr_s = [], [], [], [], [], []
    for li in range(depth):
        kind, j = li % N_MIXERS, li // N_MIXERS
        if kind == 0:
            wts = _mla_weights(mla_w_in[j], mla_g_q_lora[j], mla_g_kv[j], mla_w_uq[j], mla_w_ukv[j],
                               mla_g_q[j], mla_g_k[j])
            w_out = mla_w_out[j].astype(BF16)
            q, k, vt, gate, ckv, kpe_blk = mla_proj(xp, norm_g[li], wts, tabs_p, s_p, TM_PROMPT, BF16, "g_q_base2")
            xp = flash_prompt(q, k, vt, gate, xp, w_out, bsz, s_p, TQ_FLASH)
            ckv_p.append(ckv.reshape(bsz, s_p, A_KV_LORA))
            kpe_p.append(kpe_blk[:, A_NOPE:A_QK].reshape(bsz, s_p, A_ROPE))

            q_s, _, _, gate_s, ckv_n, kpe_blk_s = mla_proj(xs, norm_g[li], wts, tabs_s, db, db, F32, "g_q")
            kpe_n = kpe_blk_s[:, A_NOPE:A_QK]
            o_s = mla_sample_attention(q_s, ckv_n, kpe_n, cache_ckv, cache_kpe_t, j, page_table, wts,
                                       PAGES_PER_BLOCK)
            xs = gated_out(o_s, gate_s, 0, w_out, xs, db)
            ckv_s.append(ckv_n.reshape(db, 1, A_KV_LORA))
            kpe_s.append(kpe_n.reshape(db, 1, A_ROPE))
        elif kind == 1:
            w_in = hgrn_w_in[j].astype(BF16)
            w_out = hgrn_w_out[j].astype(BF16)
            proj = norm_matmul(xp, norm_g[li], w_in, TM_PROMPT // 2)
            o, st = hgrn_prompt(proj, hgrn_lower_bounds, li, bsz, s_p, T_HGRN)
            xp = gated_out(o, proj, 3, w_out, xp, TM_PROMPT, hgrn_g_norm[j])
            hg_p.append(st)

            proj_s = norm_matmul(xs, norm_g[li], w_in, db)
            o_s, st_s = hgrn_sample(proj_s, hgrn_lower_bounds, li, state_hgrn[j])
            xs = gated_out(o_s, proj_s, 3, w_out, xs, db, hgrn_g_norm[j])
            hg_s.append(st_s)
        else:
            w_in = rglru_w_in[j].astype(BF16)
            w_out = rglru_w_out[j].astype(BF16)
            wts = _rglru_weights(rglru_conv_w[j], rglru_conv_b[j], rglru_w_a[j], rglru_b_a[j],
                                 rglru_w_x[j], rglru_b_x[j], rglru_L[j])
            proj = norm_matmul(xp, norm_g[li], w_in, TM_PROMPT)
            hs, tail, hl = rglru_prompt(proj, wts, bsz, s_p, T_RGLRU)
            xp = gated_out(hs, proj, 1, w_out, xp, TM_PROMPT)
            cv_p.append(tail[:, SUBLANES - (C_CONV - 1):, :])
            lr_p.append(hl.reshape(bsz, C_WIDTH))

            proj_s = norm_matmul(xs, norm_g[li], w_in, db)
            hs_s, nbuf = rglru_sample(proj_s, state_conv[j].transpose(1, 0, 2), state_lru[j], wts)
            xs = gated_out(hs_s, proj_s, 1, w_out, xs, db)
            cv_s.append(nbuf.transpose(1, 0, 2))
            lr_s.append(hs_s)

    return (xp.reshape(bsz, s_p, d), xs.reshape(db, 1, d),
            jnp.stack(ckv_p), jnp.stack(kpe_p), jnp.stack(ckv_s), jnp.stack(kpe_s),
            jnp.stack(hg_p), jnp.stack(hg_s), jnp.stack(cv_p), jnp.stack(cv_s),
            jnp.stack(lr_p), jnp.stack(lr_s))
```

```python
import functools
import math

import jax
import jax.numpy as jnp
from jax import lax
from jax.experimental import pallas as pl
from jax.experimental.pallas import tpu as pltpu

F32 = jnp.float32
BF16 = jnp.bfloat16

LANES = 128
SUBLANES = 8
VMEM_LIMIT_BYTES = 56 * 1024 * 1024

D_MODEL = 1024
PAGE_SIZE = 128
N_MIXERS = 3
EPS = 1e-6
ROPE_THETA = 10000.0
NEG_BIG = -1e30

A_HEADS = 8
A_NOPE = 64
A_ROPE = 32
A_QK = A_NOPE + A_ROPE
A_V = 64
A_Q_LORA = 384
A_KV_LORA = 256
A_WIDTH = A_HEADS * A_V
A_HALF = A_ROPE // 2
A_PADW = A_HEADS * LANES
A_VL = A_V + 2 * SUBLANES

B_HEADS = 8
B_DK = 128
B_DV = 128
B_CHUNK = 64

C_WIDTH = 1024
C_HEADS = 8
C_BW = 128
C_CONV = 4
C_GATE_C = 8.0


def _cparams(*sem):
    return pltpu.CompilerParams(dimension_semantics=sem, vmem_limit_bytes=VMEM_LIMIT_BYTES)


def _sigmoid(x):
    return 1.0 / (1.0 + jnp.exp(-x))


def _rms_rows(x, g):
    ms = jnp.mean(x * x, axis=-1, keepdims=True)
    return x * lax.rsqrt(ms + EPS) * g


def _dot_nt(a, b):
    return lax.dot_general(a, b, (((1,), (1,)), ((), ())), preferred_element_type=F32)


def _dot(a, b):
    return jnp.dot(a, b, preferred_element_type=F32)


def _norm_matmul_kernel(x_ref, g_ref, w_ref, o_ref):
    h = _rms_rows(x_ref[...], g_ref[...])
    o_ref[...] = _dot(h.astype(BF16), w_ref[...])


def norm_matmul(x, g, w_bf16, tm):
    m, d = x.shape
    n = w_bf16.shape[1]
    return pl.pallas_call(
        _norm_matmul_kernel,
        grid=(m // tm,),
        in_specs=[pl.BlockSpec((tm, d), lambda i: (i, 0)),
                  pl.BlockSpec((1, d), lambda i: (0, 0)),
                  pl.BlockSpec((d, n), lambda i: (0, 0))],
        out_specs=pl.BlockSpec((tm, n), lambda i: (i, 0)),
        out_shape=jax.ShapeDtypeStruct((m, n), F32),
        compiler_params=_cparams("parallel"),
        name="norm_matmul",
    )(x, g.reshape(1, d), w_bf16)


def _gated_out_kernel(o_ref, gate_ref, w_ref, x_ref, gn_ref, y_ref, *, head_norm):
    o = o_ref[...]
    if head_norm:
        parts = []
        for h in range(o.shape[1] // LANES):
            parts.append(_rms_rows(o[:, h * LANES:(h + 1) * LANES], gn_ref[...]))
        o = jnp.concatenate(parts, axis=1)
    gate = gate_ref[...]
    z = o * (gate * _sigmoid(gate))
    y_ref[...] = x_ref[...] + _dot(z.astype(BF16), w_ref[...])


def gated_out(o, gate_arr, gate_col, w_bf16, x, tm, g_norm=None):
    m, w = o.shape
    d = x.shape[1]
    head_norm = g_norm is not None
    gn = (g_norm if head_norm else jnp.ones((LANES,), F32)).reshape(1, LANES)
    return pl.pallas_call(
        functools.partial(_gated_out_kernel, head_norm=head_norm),
        grid=(m // tm,),
        in_specs=[pl.BlockSpec((tm, w), lambda i: (i, 0)),
                  pl.BlockSpec((tm, w), lambda i: (i, gate_col)),
                  pl.BlockSpec((w, d), lambda i: (0, 0)),
                  pl.BlockSpec((tm, d), lambda i: (i, 0)),
                  pl.BlockSpec((1, LANES), lambda i: (0, 0))],
        out_specs=pl.BlockSpec((tm, d), lambda i: (i, 0)),
        out_shape=jax.ShapeDtypeStruct((m, d), F32),
        compiler_params=_cparams("parallel"),
        name="gated_out",
    )(o, gate_arr, w_bf16, x, gn)


def _rope_lanes(x, c, s1, s2):
    n = x.shape[1]
    return x * c + pltpu.roll(x, n - A_HALF, 1) * s1 + pltpu.roll(x, A_HALF, 1) * s2


def _head_norm_blocks(x, g):
    parts = []
    for h in range(A_HEADS):
        blk = x[:, h * LANES:(h + 1) * LANES]
        ss = jnp.sum(blk * blk, axis=-1, keepdims=True)
        parts.append(blk * lax.rsqrt(ss * (1.0 / A_QK) + EPS) * g[:, h * LANES:(h + 1) * LANES])
    return jnp.concatenate(parts, axis=1)


def _mla_proj_kernel(x_ref, gn_ref, win_ref, gql_ref, gkv_ref, wuq_ref, gq_ref, wuk_ref, wuvt_ref, gk_ref,
                     c_ref, s1_ref, s2_ref,
                     q_ref, k_ref, vt_ref, gate_ref, ckv_ref, kpe_ref):
    h = _rms_rows(x_ref[...], gn_ref[...])
    p = _dot(h.astype(BF16), win_ref[...])
    gate_ref[...] = p[:, :A_WIDTH]
    cq = p[:, A_WIDTH:A_WIDTH + A_Q_LORA]
    ckv = p[:, A_WIDTH + A_Q_LORA:A_WIDTH + A_Q_LORA + A_KV_LORA]
    kpe = p[:, A_WIDTH + A_Q_LORA + A_KV_LORA:]
    c, s1, s2 = c_ref[...], s1_ref[...], s2_ref[...]
    c8 = jnp.concatenate([c] * A_HEADS, axis=1)
    sin8 = jnp.concatenate([s2 - s1] * A_HEADS, axis=1)

    q2 = _dot(_rms_rows(cq, gql_ref[...]).astype(BF16), wuq_ref[...])
    q = q2[:, :A_PADW] * c8 + q2[:, A_PADW:] * sin8
    q_ref[...] = _head_norm_blocks(q, gq_ref[...]).astype(q_ref.dtype)

    ckvn = _rms_rows(ckv, gkv_ref[...])
    ckv_ref[...] = ckvn
    kper = _rope_lanes(kpe, c, s1, s2)
    kpe_ref[...] = kper
    cb = ckvn.astype(BF16)
    kn = _dot(cb, wuk_ref[...]) + jnp.concatenate([kper] * A_HEADS, axis=1)
    k_ref[...] = _head_norm_blocks(kn, gk_ref[...]).astype(k_ref.dtype)
    vt = _dot_nt(wuvt_ref[...], cb)
    ones = jnp.ones((A_VL - A_V, vt.shape[1]), F32)
    vt_ref[...] = jnp.concatenate(
        [blk for h in range(A_HEADS) for blk in (vt[h * A_V:(h + 1) * A_V, :], ones)], axis=0).astype(vt_ref.dtype)


def mla_proj(x, g_norm, wts, rope_tabs, rows_per_seq, tm, qkv_dtype, q_gain):
    m, d = x.shape
    c, s1, s2 = rope_tabs
    nblk = rows_per_seq // tm
    full = lambda a: pl.BlockSpec(a.shape, lambda i: (0, 0))
    tab = pl.BlockSpec((tm, LANES), lambda i: (i % nblk, 0))
    row = lambda n: pl.BlockSpec((tm, n), lambda i: (i, 0))
    return pl.pallas_call(
        _mla_proj_kernel,
        grid=(m // tm,),
        in_specs=[row(d), pl.BlockSpec((1, d), lambda i: (0, 0)), full(wts["w_in"]),
                  full(wts["g_ql"]), full(wts["g_kv"]), full(wts["w_uq"]), full(wts[q_gain]),
                  full(wts["w_uk"]), full(wts["w_uvt"]), full(wts["g_k"]), tab, tab, tab],
        out_specs=[row(A_PADW), row(A_PADW), pl.BlockSpec((A_HEADS * A_VL, tm), lambda i: (0, i)),
                   row(A_WIDTH), row(A_KV_LORA), row(LANES)],
        out_shape=[jax.ShapeDtypeStruct((m, A_PADW), qkv_dtype),
                   jax.ShapeDtypeStruct((m, A_PADW), qkv_dtype),
                   jax.ShapeDtypeStruct((A_HEADS * A_VL, m), qkv_dtype),
                   jax.ShapeDtypeStruct((m, A_WIDTH), F32),
                   jax.ShapeDtypeStruct((m, A_KV_LORA), F32),
                   jax.ShapeDtypeStruct((m, LANES), F32)],
        compiler_params=_cparams("parallel"),
        name="mla_proj",
    )(x, g_norm.reshape(1, d), wts["w_in"], wts["g_ql"], wts["g_kv"], wts["w_uq"], wts[q_gain],
      wts["w_uk"], wts["w_uvt"], wts["g_k"], c, s1, s2)


def _pad_heads(a, used):
    pad = [(0, 0)] * (a.ndim - 1) + [(0, LANES - used)]
    a = jnp.pad(a, pad)
    return a.reshape(a.shape[:-2] + (a.shape[-2] * LANES,))


def _mla_weights(w_in, g_ql, g_kv, w_uq, w_ukv, g_q, g_k):
    cq, ckv, kpe, gate = jnp.split(w_in, [A_Q_LORA, A_Q_LORA + A_KV_LORA, A_Q_LORA + A_KV_LORA + A_ROPE], axis=1)
    kpe_blk = jnp.pad(kpe, ((0, 0), (A_NOPE, LANES - A_QK)))
    scale = A_QK ** -0.5
    ukv = w_ukv.reshape(A_KV_LORA, A_HEADS, A_NOPE + A_V)
    uk, uv = ukv[:, :, :A_NOPE], ukv[:, :, A_NOPE:]
    ukt_pad = _pad_heads(uk, A_NOPE).T
    ukt_hi = ukt_pad.astype(BF16)
    uq = w_uq.reshape(A_Q_LORA, A_HEADS, A_QK)
    x1, x2 = uq[:, :, A_NOPE:A_NOPE + A_HALF], uq[:, :, A_NOPE + A_HALF:]
    uq_rot = jnp.concatenate([jnp.zeros_like(uq[:, :, :A_NOPE]), -x2, x1], axis=2)
    return {
        "w_in": jnp.concatenate([gate, cq, ckv, kpe_blk], axis=1).astype(BF16),
        "g_ql": g_ql.reshape(1, -1), "g_kv": g_kv.reshape(1, -1),
        "w_uq": jnp.concatenate([_pad_heads(uq, A_QK), _pad_heads(uq_rot, A_QK)], axis=1).astype(BF16),
        "g_q": _pad_heads(jnp.broadcast_to(g_q * scale, (A_HEADS, A_QK)), A_QK).reshape(1, -1),
        "g_q_base2": _pad_heads(jnp.broadcast_to(g_q * (scale * math.log2(math.e)), (A_HEADS, A_QK)),
                                A_QK).reshape(1, -1),
        "g_k": _pad_heads(jnp.broadcast_to(g_k, (A_HEADS, A_QK)), A_QK).reshape(1, -1),
        "w_uk": _pad_heads(uk, A_NOPE).astype(BF16),
        "w_uv": uv.reshape(A_KV_LORA, A_WIDTH).astype(BF16),
        "w_uvt": uv.reshape(A_KV_LORA, A_WIDTH).T.astype(BF16),
        "ukt": uk.reshape(A_KV_LORA, A_HEADS * A_NOPE).T.astype(BF16),
        "ukt_hi": ukt_hi,
        "ukt_lo": (ukt_pad - ukt_hi.astype(F32)).astype(BF16),
    }


def _rope_tables(pos):
    inv = 1.0 / (ROPE_THETA ** (jnp.arange(0, A_ROPE, 2, dtype=F32) / A_ROPE))
    ang = pos.astype(F32)[:, None] * inv[None, :]
    cos, sin = jnp.cos(ang), jnp.sin(ang)
    n = pos.shape[0]
    z = lambda w: jnp.zeros((n, w), F32)
    c = jnp.concatenate([jnp.ones((n, A_NOPE), F32), cos, cos, z(LANES - A_QK)], axis=1)
    s1 = jnp.concatenate([z(A_NOPE), -sin, z(A_HALF), z(LANES - A_QK)], axis=1)
    s2 = jnp.concatenate([z(A_NOPE + A_HALF), sin, z(LANES - A_QK)], axis=1)
    return c, s1, s2


def _flash_kernel(qi_ref, kj_ref, q_ref, k_ref, vt_ref, gate_ref, x_ref, wout_ref, y_ref, m_scr, acc_scr):
    p_id = pl.program_id(1)
    qi = qi_ref[p_id]
    kj = kj_ref[p_id]
    tq, tk = q_ref.shape[0], k_ref.shape[0]

    @pl.when(kj == 0)
    def _():
        m_scr[...] = jnp.full(m_scr.shape, NEG_BIG, F32)
        acc_scr[...] = jnp.zeros(acc_scr.shape, F32)

    def step(masked):
        if masked:
            key = lax.broadcasted_iota(jnp.int32, (tk, tq), 0)
            qry = lax.broadcasted_iota(jnp.int32, (tk, tq), 1)
            keep = key <= qry

        def scores(h):
            return _dot_nt(k_ref[:, h * LANES:(h + 1) * LANES], q_ref[:, h * LANES:(h + 1) * LANES])

        st_next = scores(0)
        for h in range(A_HEADS):
            st = st_next
            if h + 1 < A_HEADS:
                st_next = scores(h + 1)
            if masked:
                st = jnp.where(keep, st, NEG_BIG)
            m_prev = m_scr[h]
            m_new = jnp.maximum(m_prev, jnp.max(st, axis=0, keepdims=True))
            p = jnp.exp2(st - m_new)
            corr = jnp.exp2(m_prev - m_new)
            rows = slice(h * A_VL, (h + 1) * A_VL)
            acc_scr[rows, :] = acc_scr[rows, :] * corr + _dot(vt_ref[rows, :], p.astype(BF16))
            m_scr[h] = m_new

    @pl.when(kj < qi)
    def _():
        step(False)

    @pl.when(kj == qi)
    def _():
        step(True)
        ot = jnp.concatenate([acc_scr[h * A_VL:h * A_VL + A_V, :] / acc_scr[h * A_VL + A_V:h * A_VL + A_V + 1, :]
                              for h in range(A_HEADS)], axis=0)
        gate = gate_ref[...]
        z = ot.T * (gate * _sigmoid(gate))
        y_ref[...] = x_ref[...] + _dot(z.astype(BF16), wout_ref[...])


def flash_prompt(q, k, vt, gate, x, w_out_bf16, batch, seq, tq):
    nq = seq // tq
    d = x.shape[1]
    pairs = [(i, j) for i in range(nq) for j in range(i + 1)]
    qi = jnp.asarray([p[0] for p in pairs], jnp.int32)
    kj = jnp.asarray([p[1] for p in pairs], jnp.int32)
    qrow = lambda n: pl.BlockSpec((tq, n), lambda b, p, qi, kj: (b * nq + qi[p], 0))
    grid_spec = pltpu.PrefetchScalarGridSpec(
        num_scalar_prefetch=2,
        grid=(batch, len(pairs)),
        in_specs=[qrow(A_PADW),
                  pl.BlockSpec((tq, A_PADW), lambda b, p, qi, kj: (b * nq + kj[p], 0)),
                  pl.BlockSpec((A_HEADS * A_VL, tq), lambda b, p, qi, kj: (0, b * nq + kj[p])),
                  qrow(A_WIDTH), qrow(d), pl.BlockSpec(w_out_bf16.shape, lambda b, p, qi, kj: (0, 0))],
        out_specs=qrow(d),
        scratch_shapes=[pltpu.VMEM((A_HEADS, 1, tq), F32), pltpu.VMEM((A_HEADS * A_VL, tq), F32)])
    return pl.pallas_call(
        _flash_kernel, grid_spec=grid_spec,
        out_shape=jax.ShapeDtypeStruct((batch * seq, d), F32),
        compiler_params=_cparams("parallel", "arbitrary"),
        name="flash_prompt",
    )(qi, kj, q, k, vt, gate, x, w_out_bf16)


def _absorb_query_kernel(q_ref, gk_ref, ukhi_ref, uklo_ref, qg_ref, qabs_ref):
    qg = q_ref[...] * gk_ref[...]
    qg_ref[...] = qg
    for h in range(A_HEADS):
        blk = qg[:, h * LANES:(h + 1) * LANES]
        q_hi = blk.astype(BF16)
        q_lo = (blk - q_hi.astype(F32)).astype(BF16)
        w_hi = ukhi_ref[h * LANES:(h + 1) * LANES, :]
        qabs_ref[h] = _dot(q_hi, w_hi) + _dot(q_hi, uklo_ref[h * LANES:(h + 1) * LANES, :]) + _dot(q_lo, w_hi)


def absorb_query(q_s, wts):
    db = q_s.shape[0]
    whole = lambda a: pl.BlockSpec(a.shape, lambda i: (0,) * a.ndim)
    return pl.pallas_call(
        _absorb_query_kernel,
        grid=(1,),
        in_specs=[whole(q_s), whole(wts["g_k"]), whole(wts["ukt_hi"]), whole(wts["ukt_lo"])],
        out_specs=[pl.BlockSpec((db, A_PADW), lambda i: (0, 0)),
                   pl.BlockSpec((A_HEADS, db, A_KV_LORA), lambda i: (0, 0, 0))],
        out_shape=[jax.ShapeDtypeStruct((db, A_PADW), F32),
                   jax.ShapeDtypeStruct((A_HEADS, db, A_KV_LORA), F32)],
        compiler_params=_cparams("arbitrary"),
        name="absorb_query",
    )(q_s, wts["g_k"], wts["ukt_hi"], wts["ukt_lo"])


def _sample_attn_kernel(pt_ref, qabs_ref, qpe_ref, ukt_ref, wuv_ref, cnew_ref, knew_ref, ckv_hbm, kpe_hbm, o_ref,
                        lhs_scr, cb_scr, s_scr, ckv_buf, kpe_buf, sems, *, la, n_pages, pages_per_block):
    b = pl.program_id(0)
    last = pl.num_programs(0) - 1
    slot = b % 2
    nxt = jnp.minimum(b + 1, last)
    n_nope = A_HEADS * A_NOPE
    n_past = n_pages * PAGE_SIZE

    def page_copies(seq, to_slot, i):
        page = pt_ref[seq * n_pages + i]
        rows = pl.ds(i * PAGE_SIZE, PAGE_SIZE)
        return (pltpu.make_async_copy(ckv_hbm.at[la, page], ckv_buf.at[to_slot, rows, :], sems.at[0, to_slot]),
                pltpu.make_async_copy(kpe_hbm.at[la, page], kpe_buf.at[to_slot, :, rows], sems.at[1, to_slot]))

    def start_pages(seq, to_slot, pages):
        for i in pages:
            for cp in page_copies(seq, to_slot, i):
                cp.start()

    def wait_pages(to_slot):
        pltpu.make_async_copy(ckv_buf.at[to_slot], ckv_buf.at[to_slot], sems.at[0, to_slot]).wait()
        pltpu.make_async_copy(kpe_buf.at[to_slot], kpe_buf.at[to_slot], sems.at[1, to_slot]).wait()

    @pl.when(b == 0)
    def _():
        start_pages(b, slot, range(n_pages))

    wait_pages(slot)
    lhs_scr[:n_nope, :] = ukt_ref[...]
    lhs_scr[n_nope:, :] = qabs_ref[...]

    def score(c, kpe_t, start):
        n = c.shape[0]
        cb = c.astype(BF16)
        cb_scr[start:start + n, :] = cb
        a = _dot_nt(lhs_scr[...], cb)
        kn = a[:n_nope]
        ss = jnp.sum((kn * kn).reshape(A_HEADS, A_NOPE, n), axis=1)
        s_pe = _dot(qpe_ref[...], kpe_t.astype(BF16))[:A_HEADS]
        pe_ss = jnp.sum(kpe_t * kpe_t, axis=0, keepdims=True)
        rs = lax.rsqrt((ss + pe_ss) * (1.0 / A_QK) + EPS)
        return (a[n_nope:n_nope + A_HEADS] + s_pe) * rs

    for blk in range(0, n_pages, pages_per_block):
        start, n = blk * PAGE_SIZE, pages_per_block * PAGE_SIZE
        s_scr[:, start:start + n] = score(ckv_buf[slot, start:start + n, :], kpe_buf[slot, :, start:start + n], start)
        start_pages(nxt, 1 - slot, range(blk, blk + pages_per_block))
    s_new = score(jnp.broadcast_to(cnew_ref[...], (LANES, A_KV_LORA)),
                  jnp.broadcast_to(knew_ref[...], (A_ROPE, LANES)), n_past)
    s_scr[:, n_past:] = jnp.where(lax.broadcasted_iota(jnp.int32, s_new.shape, 1) < 1, s_new, NEG_BIG)

    s = s_scr[...]
    p = jnp.exp(s - jnp.max(s, axis=-1, keepdims=True))
    pb = p.astype(BF16)
    acc = jnp.concatenate([_dot(pb, cb_scr[:, :LANES]), _dot(pb, cb_scr[:, LANES:])], axis=1)
    lat = acc / jnp.sum(p, axis=-1, keepdims=True)
    full = _dot(lat.astype(BF16), wuv_ref[...])
    lane = lax.broadcasted_iota(jnp.int32, full.shape, 1)
    row = lax.broadcasted_iota(jnp.int32, full.shape, 0)
    sel = (lane >= row * A_V) & (lane < (row + 1) * A_V)
    o_ref[...] = jnp.sum(jnp.where(sel, full, 0.0), axis=0, keepdims=True)

    @pl.when(b == last)
    def _():
        wait_pages(1 - slot)


def mla_sample_attention(q_s, ckv_new, kpe_new, cache_ckv, cache_kpe_t, la, page_table, wts, pages_per_block):
    db, n_pages = page_table.shape
    assert n_pages % pages_per_block == 0
    n_past = n_pages * PAGE_SIZE
    n_tok = n_past + LANES
    row3 = lambda r, n: pl.BlockSpec((None, r, n), lambda b, pt: (b, 0, 0))
    full = lambda a: pl.BlockSpec(a.shape, lambda b, pt: (0, 0))
    hbm = pl.BlockSpec(memory_space=pl.ANY)
    q_rows = 2 * SUBLANES
    lhs_rows = A_HEADS * A_NOPE + q_rows
    qg, qabs = absorb_query(q_s, wts)
    pad_rows = lambda a: jnp.pad(a, ((0, 0), (0, q_rows - A_HEADS), (0, 0))).astype(BF16)
    qabs = pad_rows(qabs.transpose(1, 0, 2))
    qpe = pad_rows(qg.reshape(db, A_HEADS, LANES)[:, :, A_NOPE:A_QK])
    grid_spec = pltpu.PrefetchScalarGridSpec(
        num_scalar_prefetch=1,
        grid=(db,),
        in_specs=[row3(q_rows, A_KV_LORA), row3(q_rows, A_ROPE), full(wts["ukt"]),
                  full(wts["w_uv"]), row3(1, A_KV_LORA), row3(A_ROPE, 1), hbm, hbm],
        out_specs=pl.BlockSpec((None, 1, A_WIDTH), lambda b, pt: (b, 0, 0)),
        scratch_shapes=[pltpu.VMEM((lhs_rows, A_KV_LORA), BF16),
                        pltpu.VMEM((n_tok, A_KV_LORA), BF16), pltpu.VMEM((A_HEADS, n_tok), F32),
                        pltpu.VMEM((2, n_past, A_KV_LORA), F32), pltpu.VMEM((2, A_ROPE, n_past), F32),
                        pltpu.SemaphoreType.DMA((2, 2))])
    out = pl.pallas_call(
        functools.partial(_sample_attn_kernel, la=la, n_pages=n_pages, pages_per_block=pages_per_block),
        grid_spec=grid_spec,
        out_shape=jax.ShapeDtypeStruct((db, 1, A_WIDTH), F32),
        compiler_params=_cparams("arbitrary"),
        name="mla_sample_attention",
    )(page_table.reshape(-1), qabs, qpe, wts["ukt"],
      wts["w_uv"], ckv_new.reshape(db, 1, A_KV_LORA), kpe_new.reshape(db, A_ROPE, 1), cache_ckv, cache_kpe_t)
    return out.reshape(db, A_WIDTH)


def _hgrn_lower_bound(lb_all, li):
    e = jnp.exp(lb_all - jnp.max(lb_all, axis=0, keepdims=True))
    smx = e / jnp.sum(e, axis=0, keepdims=True)
    return jnp.sum(smx[1:li + 1], axis=0, keepdims=True) if li > 0 else jnp.zeros_like(smx[:1])


def _hgrn_gates(q, f, lb):
    g = jnp.log(lb + (1.0 - lb) * _sigmoid(f))
    k = (1.0 - lb) * _sigmoid(-f)
    qf = q * _sigmoid(q) * (B_DK ** -0.5)
    return qf, k, g


def _cumsum_rows(x):
    n = x.shape[0]
    row = lax.broadcasted_iota(jnp.int32, x.shape, 0)
    d = 1
    while d < n:
        x = x + jnp.where(row >= d, pltpu.roll(x, d, 0), 0.0)
        d *= 2
    return x


def _bcast_row_in_groups(x, j):
    n, w = x.shape
    x3 = x.reshape(n // SUBLANES, SUBLANES, w)
    return jnp.broadcast_to(x3[:, j:j + 1, :], x3.shape).reshape(n, w)


def _intra_chunk_masks(c):
    tt = lax.broadcasted_iota(jnp.int32, (c, c), 0)
    ss = lax.broadcasted_iota(jnp.int32, (c, c), 1)
    masks = {"diag": tt == ss}
    half = c // 2
    while half >= 1:
        blk = 2 * half
        keep = ((tt % blk) >= half) & ((ss % blk) < half)
        masks[half] = keep & ((tt // blk) == (ss // blk)) if blk < c else keep
        half //= 2
    return masks


def _intra_chunk_att(qf, k, b, masks):
    c = qf.shape[0]
    row = lax.broadcasted_iota(jnp.int32, (c, 1), 0)
    att = jnp.where(masks["diag"], jnp.sum(qf * k, axis=-1, keepdims=True), 0.0)
    half = c // 2
    while half >= 1:
        blk = 2 * half
        if blk >= SUBLANES:
            bref = jnp.concatenate(
                [jnp.broadcast_to(b[m:m + 1], (blk, b.shape[1])) for m in range(half, c, blk)], axis=0)
        else:
            rm = row % SUBLANES
            bref = _bcast_row_in_groups(b, SUBLANES - half)
            for m in range(SUBLANES - half - blk, 0, -blk):
                bref = jnp.where(rm < m + half, _bcast_row_in_groups(b, m), bref)
        e = jnp.exp(-jnp.abs(b - bref))
        p = _dot_nt((qf * e).astype(BF16), (k * e).astype(BF16))
        att = att + jnp.where(masks[half], p, 0.0)
        half //= 2
    return att


def _hgrn_prompt_kernel(q_ref, f_ref, v_ref, gate_ref, x_ref, wout_ref, gn_ref, lb_ref, y_ref, st_ref,
                        state_scr, o_scr, *, li):
    t = pl.program_id(1)

    @pl.when(t == 0)
    def _():
        state_scr[...] = jnp.zeros(state_scr.shape, F32)

    lb_all = _hgrn_lower_bound(lb_ref[...], li)
    masks = _intra_chunk_masks(B_CHUNK)
    for ci in range(q_ref.shape[0] // B_CHUNK):
        sl = slice(ci * B_CHUNK, (ci + 1) * B_CHUNK)
        for h in range(B_HEADS):
            hl = slice(h * LANES, (h + 1) * LANES)
            qf, k, g = _hgrn_gates(q_ref[sl, hl], f_ref[sl, hl], lb_all[:, hl])
            b = _cumsum_rows(g)
            vb = v_ref[sl, hl].astype(BF16)
            intra = _dot(_intra_chunk_att(qf, k, b, masks).astype(BF16), vb)
            blast = b[B_CHUNK - 1:B_CHUNK]
            kd = (k * jnp.exp(blast - b)).astype(BF16)
            upd = lax.dot_general(vb, kd, (((0,), (0,)), ((), ())), preferred_element_type=F32)
            st = state_scr[h]
            o_scr[sl, hl] = intra + _dot_nt((qf * jnp.exp(b)).astype(BF16), st.astype(BF16))
            state_scr[h] = jnp.exp(blast) * st + upd

    o = jnp.concatenate([_rms_rows(o_scr[:, h * LANES:(h + 1) * LANES], gn_ref[...]) for h in range(B_HEADS)],
                        axis=1)
    gate = gate_ref[...]
    z = o * (gate * _sigmoid(gate))
    y_ref[...] = x_ref[...] + _dot(z.astype(BF16), wout_ref[...])

    @pl.when(t == pl.num_programs(1) - 1)
    def _():
        for h in range(B_HEADS):
            st_ref[h] = state_scr[h].T


def hgrn_prompt(proj, x, w_out_bf16, g_norm, lower_bounds, li, batch, seq, tblk):
    nt = seq // tblk
    w = B_HEADS * B_DK
    d = x.shape[1]
    rows = lambda n, col: pl.BlockSpec((tblk, n), lambda b, t: (b * nt + t, col))
    full = lambda a: pl.BlockSpec(a.shape, lambda b, t: (0,) * a.ndim)
    gn = g_norm.reshape(1, B_DV)
    return pl.pallas_call(
        functools.partial(_hgrn_prompt_kernel, li=li),
        grid=(batch, nt),
        in_specs=[rows(w, 0), rows(w, 1), rows(w, 2), rows(w, 3), rows(d, 0), full(w_out_bf16), full(gn),
                  full(lower_bounds)],
        out_specs=[rows(d, 0), pl.BlockSpec((None, B_HEADS, B_DK, B_DV), lambda b, t: (b, 0, 0, 0))],
        out_shape=[jax.ShapeDtypeStruct((batch * seq, d), F32),
                   jax.ShapeDtypeStruct((batch, B_HEADS, B_DK, B_DV), F32)],
        scratch_shapes=[pltpu.VMEM((B_HEADS, B_DV, B_DK), F32), pltpu.VMEM((tblk, w), F32)],
        compiler_params=_cparams("parallel", "arbitrary"),
        name="hgrn_prompt",
    )(proj, proj, proj, proj, x, w_out_bf16, gn, lower_bounds)


def _hgrn_sample_gates_kernel(p_ref, lb_ref, qe_ref, eg_ref, k_ref, av_ref, *, li):
    w = B_HEADS * B_DK
    lb = _hgrn_lower_bound(lb_ref[...], li)
    qf, k, g = _hgrn_gates(p_ref[:, :w], p_ref[:, w:2 * w], lb)
    v = p_ref[:, 2 * w:3 * w]
    eg = jnp.exp(g)
    qe_ref[...] = qf * eg
    eg_ref[...] = eg
    k_ref[...] = k
    qk = qf * k
    av_ref[...] = jnp.concatenate(
        [jnp.sum(qk[:, h * LANES:(h + 1) * LANES], axis=-1, keepdims=True) * v[:, h * LANES:(h + 1) * LANES]
         for h in range(B_HEADS)], axis=1)


def _hgrn_sample_state_kernel(st_ref, qe_ref, egt_ref, kt_ref, v_ref, av_ref, o_ref, ns_ref):
    for s in range(st_ref.shape[0]):
        for h in range(B_HEADS):
            st = st_ref[s, h]
            ns_ref[s, h] = egt_ref[s, :, h:h + 1] * st + kt_ref[s, :, h:h + 1] * v_ref[s, h:h + 1, :]
            qe = jnp.broadcast_to(qe_ref[s, h:h + 1, :], (2 * SUBLANES, B_DK)).astype(BF16)
            o_ref[s, h:h + 1, :] = _dot(qe, st.astype(BF16))[:1] + av_ref[s, h:h + 1, :]


def hgrn_sample(proj_s, lower_bounds, li, state):
    db = proj_s.shape[0]
    w = B_HEADS * B_DK
    whole = lambda a: pl.BlockSpec(a.shape, lambda i: (0,) * a.ndim)
    qe, eg, k, av = pl.pallas_call(
        functools.partial(_hgrn_sample_gates_kernel, li=li),
        grid=(1,),
        in_specs=[whole(proj_s), whole(lower_bounds)],
        out_specs=[pl.BlockSpec((db, w), lambda i: (0, 0))] * 4,
        out_shape=[jax.ShapeDtypeStruct((db, w), F32)] * 4,
        compiler_params=_cparams("arbitrary"),
        name="hgrn_sample_gates",
    )(proj_s, lower_bounds)
    heads = lambda a: a.reshape(db, B_HEADS, B_DK)
    cols = lambda a: heads(a).transpose(0, 2, 1)
    sb = SEQS_HGRN_SAMPLE
    hrow = pl.BlockSpec((sb, B_HEADS, B_DK), lambda b: (b, 0, 0))
    hcol = pl.BlockSpec((sb, B_DK, B_HEADS), lambda b: (b, 0, 0))
    stspec = pl.BlockSpec((sb, B_HEADS, B_DK, B_DV), lambda b: (b, 0, 0, 0))
    o, new_state = pl.pallas_call(
        _hgrn_sample_state_kernel,
        grid=(db // sb,),
        in_specs=[stspec, hrow, hcol, hcol, hrow, hrow],
        out_specs=[hrow, stspec],
        out_shape=[jax.ShapeDtypeStruct((db, B_HEADS, B_DV), F32),
                   jax.ShapeDtypeStruct(state.shape, F32)],
        compiler_params=_cparams("parallel"),
        name="hgrn_sample_state",
    )(state, heads(qe), cols(eg), cols(k), heads(proj_s[:, 2 * w:3 * w]), heads(av))
    return o.reshape(db, w), new_state


def _rglru_gates(y, wax_ref, ba, bx, lam):
    yb = y.astype(BF16)
    rs, is_ = [], []
    for n in range(C_HEADS):
        ax = _dot(yb[:, n * C_BW:(n + 1) * C_BW], wax_ref[n])
        rs.append(ax[:, :C_BW])
        is_.append(ax[:, C_BW:])
    r = _sigmoid(jnp.concatenate(rs, axis=1) + ba)
    i = _sigmoid(jnp.concatenate(is_, axis=1) + bx)
    softplus = jnp.maximum(-lam, 0.0) + jnp.log1p(jnp.exp(-jnp.abs(lam)))
    log_a = -C_GATE_C * r * softplus
    a = jnp.exp(log_a)
    th = jnp.tanh(log_a)
    u = jnp.sqrt(-2.0 * th / (1.0 - th)) * (i * y)
    return a, u


def _rglru_prompt_kernel(xb_ref, gate_ref, x_ref, wout_ref, cw_ref, cb_ref, wax_ref, ba_ref, bx_ref, lam_ref,
                         y_ref, tail_ref, hl_ref, prev_scr, h_scr, a_scr, u_scr, hs_scr):
    t = pl.program_id(1)
    tb = xb_ref.shape[0]

    @pl.when(t == 0)
    def _():
        prev_scr[...] = jnp.zeros(prev_scr.shape, F32)
        h_scr[...] = jnp.zeros(h_scr.shape, F32)

    xb = xb_ref[...]
    prev = prev_scr[...]
    row = lax.broadcasted_iota(jnp.int32, (tb, 1), 0)
    y = cb_ref[...] + cw_ref[C_CONV - 1:C_CONV, :] * xb
    for d in range(1, C_CONV):
        sh = pltpu.roll(xb, d, 0)
        for r in range(d):
            sh = jnp.where(row == r, prev[SUBLANES - d + r:SUBLANES - d + r + 1, :], sh)
        y = y + cw_ref[C_CONV - 1 - d:C_CONV - d, :] * sh
    prev_scr[...] = xb[tb - SUBLANES:, :]
    a, u = _rglru_gates(y, wax_ref, ba_ref[...], bx_ref[...], lam_ref[...])
    a_scr[...] = a
    u_scr[...] = u

    def body(i, h):
        base = pl.multiple_of(i * SUBLANES, SUBLANES)
        for r in range(SUBLANES):
            h = a_scr[pl.ds(base + r, 1), :] * h + u_scr[pl.ds(base + r, 1), :]
            hs_scr[pl.ds(base + r, 1), :] = h
        return h

    h = lax.fori_loop(0, tb // SUBLANES, body, h_scr[...])
    h_scr[...] = h
    gate = gate_ref[...]
    z = hs_scr[...] * (gate * _sigmoid(gate))
    y_ref[...] = x_ref[...] + _dot(z.astype(BF16), wout_ref[...])

    @pl.when(t == pl.num_programs(1) - 1)
    def _():
        tail_ref[...] = xb[tb - SUBLANES:, :]
        hl_ref[...] = h


def _rglru_weights(conv_w, conv_b, w_a, b_a, w_x, b_x, lam):
    r1 = lambda a: a.reshape(1, -1)
    return {"cw": conv_w, "cb": r1(conv_b), "wax": jnp.concatenate([w_a, w_x], axis=2).astype(BF16),
            "ba": r1(b_a), "bx": r1(b_x), "lam": r1(lam)}


def rglru_prompt(proj, x, w_out_bf16, wts, batch, seq, tblk):
    nt = seq // tblk
    w = C_WIDTH
    d = x.shape[1]
    full = lambda a: pl.BlockSpec(a.shape, lambda b, t: (0,) * a.ndim)
    rows = lambda n, col: pl.BlockSpec((tblk, n), lambda b, t: (b * nt + t, col))
    names = ("cw", "cb", "wax", "ba", "bx", "lam")
    return pl.pallas_call(
        _rglru_prompt_kernel,
        grid=(batch, nt),
        in_specs=[rows(w, 0), rows(w, 1), rows(d, 0), full(w_out_bf16)] + [full(wts[n]) for n in names],
        out_specs=[rows(d, 0),
                   pl.BlockSpec((None, SUBLANES, w), lambda b, t: (b, 0, 0)),
                   pl.BlockSpec((None, 1, w), lambda b, t: (b, 0, 0))],
        out_shape=[jax.ShapeDtypeStruct((batch * seq, d), F32),
                   jax.ShapeDtypeStruct((batch, SUBLANES, w), F32),
                   jax.ShapeDtypeStruct((batch, 1, w), F32)],
        scratch_shapes=[pltpu.VMEM((SUBLANES, w), F32), pltpu.VMEM((1, w), F32),
                        pltpu.VMEM((tblk, w), F32), pltpu.VMEM((tblk, w), F32), pltpu.VMEM((tblk, w), F32)],
        compiler_params=_cparams("parallel", "arbitrary"),
        name="rglru_prompt",
    )(proj, proj, x, w_out_bf16, *[wts[n] for n in names])


def _rglru_sample_kernel(xb_ref, buf_ref, h0_ref, cw_ref, cb_ref, wax_ref, ba_ref, bx_ref, lam_ref,
                         hs_ref, nbuf_ref):
    xb = xb_ref[...]
    y = cb_ref[...] + cw_ref[C_CONV - 1:C_CONV, :] * xb
    for j in range(C_CONV - 1):
        y = y + cw_ref[j:j + 1, :] * buf_ref[j]
    a, u = _rglru_gates(y, wax_ref, ba_ref[...], bx_ref[...], lam_ref[...])
    hs_ref[...] = a * h0_ref[...] + u
    for j in range(C_CONV - 2):
        nbuf_ref[j] = buf_ref[j + 1]
    nbuf_ref[C_CONV - 2] = xb


def rglru_sample(proj_s, buf_t, h0, wts):
    db = proj_s.shape[0]
    w = C_WIDTH
    whole = lambda a: pl.BlockSpec(a.shape, lambda i: (0,) * a.ndim)
    names = ("cw", "cb", "wax", "ba", "bx", "lam")
    return pl.pallas_call(
        _rglru_sample_kernel,
        grid=(1,),
        in_specs=[pl.BlockSpec((db, w), lambda i: (0, 0)), whole(buf_t), whole(h0)] + [whole(wts[n]) for n in names],
        out_specs=[pl.BlockSpec((db, w), lambda i: (0, 0)), whole(buf_t)],
        out_shape=[jax.ShapeDtypeStruct((db, w), F32), jax.ShapeDtypeStruct(buf_t.shape, F32)],
        compiler_params=_cparams("arbitrary"),
        name="rglru_sample",
    )(proj_s, buf_t, h0, *[wts[n] for n in names])


TM_PROMPT = 512
TQ_FLASH = 512
T_HGRN = 256
T_RGLRU = 256
PAGES_PER_BLOCK = 16
SEQS_HGRN_SAMPLE = 4

def kernel(x_prompt, x_sample, cache_ckv, cache_kpe, page_table, state_hgrn, state_conv, state_lru, norm_g, mla_w_in, mla_g_q_lora, mla_g_kv, mla_w_uq, mla_w_ukv, mla_g_q, mla_g_k, mla_w_out, hgrn_w_in, hgrn_lower_bounds, hgrn_g_norm, hgrn_w_out, rglru_w_in, rglru_conv_w, rglru_conv_b, rglru_w_a, rglru_b_a, rglru_w_x, rglru_b_x, rglru_L, rglru_w_out):
    bsz, s_p, d = x_prompt.shape
    db, s_s, _ = x_sample.shape
    assert s_s == 1 and d == D_MODEL
    depth = norm_g.shape[0]
    past_len = page_table.shape[1] * PAGE_SIZE
    xp = x_prompt.reshape(bsz * s_p, d)
    xs = x_sample.reshape(db, d)
    tabs_p = _rope_tables(jnp.arange(s_p))
    tabs_s = _rope_tables(jnp.full((db,), past_len))
    cache_kpe_t = cache_kpe.transpose(0, 1, 3, 2)

    ckv_p, kpe_p, ckv_s, kpe_s = [], [], [], []
    hg_p, hg_s, cv_p, cv_s, lr_p, lr_s = [], [], [], [], [], []
    for li in range(depth):
        kind, j = li % N_MIXERS, li // N_MIXERS
        if kind == 0:
            wts = _mla_weights(mla_w_in[j], mla_g_q_lora[j], mla_g_kv[j], mla_w_uq[j], mla_w_ukv[j],
                               mla_g_q[j], mla_g_k[j])
            w_out = mla_w_out[j].astype(BF16)
            q, k, vt, gate, ckv, kpe_blk = mla_proj(xp, norm_g[li], wts, tabs_p, s_p, TM_PROMPT, BF16, "g_q_base2")
            xp = flash_prompt(q, k, vt, gate, xp, w_out, bsz, s_p, TQ_FLASH)
            ckv_p.append(ckv.reshape(bsz, s_p, A_KV_LORA))
            kpe_p.append(kpe_blk[:, A_NOPE:A_QK].reshape(bsz, s_p, A_ROPE))

            q_s, _, _, gate_s, ckv_n, kpe_blk_s = mla_proj(xs, norm_g[li], wts, tabs_s, db, db, F32, "g_q")
            kpe_n = kpe_blk_s[:, A_NOPE:A_QK]
            o_s = mla_sample_attention(q_s, ckv_n, kpe_n, cache_ckv, cache_kpe_t, j, page_table, wts,
                                       PAGES_PER_BLOCK)
            xs = gated_out(o_s, gate_s, 0, w_out, xs, db)
            ckv_s.append(ckv_n.reshape(db, 1, A_KV_LORA))
            kpe_s.append(kpe_n.reshape(db, 1, A_ROPE))
        elif kind == 1:
            w_in = hgrn_w_in[j].astype(BF16)
            w_out = hgrn_w_out[j].astype(BF16)
            proj = norm_matmul(xp, norm_g[li], w_in, TM_PROMPT // 2)
            xp, st = hgrn_prompt(proj, xp, w_out, hgrn_g_norm[j], hgrn_lower_bounds, li, bsz, s_p, T_HGRN)
            hg_p.append(st)

            proj_s = norm_matmul(xs, norm_g[li], w_in, db)
            o_s, st_s = hgrn_sample(proj_s, hgrn_lower_bounds, li, state_hgrn[j])
            xs = gated_out(o_s, proj_s, 3, w_out, xs, db, hgrn_g_norm[j])
            hg_s.append(st_s)
        else:
            w_in = rglru_w_in[j].astype(BF16)
            w_out = rglru_w_out[j].astype(BF16)
            wts = _rglru_weights(rglru_conv_w[j], rglru_conv_b[j], rglru_w_a[j], rglru_b_a[j],
                                 rglru_w_x[j], rglru_b_x[j], rglru_L[j])
            proj = norm_matmul(xp, norm_g[li], w_in, TM_PROMPT)
            xp, tail, hl = rglru_prompt(proj, xp, w_out, wts, bsz, s_p, T_RGLRU)
            cv_p.append(tail[:, SUBLANES - (C_CONV - 1):, :])
            lr_p.append(hl.reshape(bsz, C_WIDTH))

            proj_s = norm_matmul(xs, norm_g[li], w_in, db)
            hs_s, nbuf = rglru_sample(proj_s, state_conv[j].transpose(1, 0, 2), state_lru[j], wts)
            xs = gated_out(hs_s, proj_s, 1, w_out, xs, db)
            cv_s.append(nbuf.transpose(1, 0, 2))
            lr_s.append(hs_s)

    return (xp.reshape(bsz, s_p, d), xs.reshape(db, 1, d),
            jnp.stack(ckv_p), jnp.stack(kpe_p), jnp.stack(ckv_s), jnp.stack(kpe_s),
            jnp.stack(hg_p), jnp.stack(hg_s), jnp.stack(cv_p), jnp.stack(cv_s),
            jnp.stack(lr_p), jnp.stack(lr_s))
```

```python
import functools
import math

import jax
import jax.numpy as jnp
from jax import lax
from jax.experimental import pallas as pl
from jax.experimental.pallas import tpu as pltpu

F32 = jnp.float32
BF16 = jnp.bfloat16

LANES = 128
SUBLANES = 8
VMEM_LIMIT_BYTES = 56 * 1024 * 1024

D_MODEL = 1024
PAGE_SIZE = 128
N_MIXERS = 3
EPS = 1e-6
ROPE_THETA = 10000.0
NEG_BIG = -1e30

A_HEADS = 8
A_NOPE = 64
A_ROPE = 32
A_QK = A_NOPE + A_ROPE
A_V = 64
A_Q_LORA = 384
A_KV_LORA = 256
A_WIDTH = A_HEADS * A_V
A_HALF = A_ROPE // 2
A_PADW = A_HEADS * LANES
A_VL = A_V + 2 * SUBLANES

B_HEADS = 8
B_DK = 128
B_DV = 128
B_CHUNK = 64

C_WIDTH = 1024
C_HEADS = 8
C_BW = 128
C_CONV = 4
C_GATE_C = 8.0


def _cparams(*sem):
    return pltpu.CompilerParams(dimension_semantics=sem, vmem_limit_bytes=VMEM_LIMIT_BYTES)


def _sigmoid(x):
    return 1.0 / (1.0 + jnp.exp(-x))


def _rms_rows(x, g):
    ms = jnp.mean(x * x, axis=-1, keepdims=True)
    return x * lax.rsqrt(ms + EPS) * g


def _dot_nt(a, b):
    return lax.dot_general(a, b, (((1,), (1,)), ((), ())), preferred_element_type=F32)


def _dot(a, b):
    return jnp.dot(a, b, preferred_element_type=F32)


def _norm_matmul_kernel(x_ref, g_ref, w_ref, o_ref):
    h = _rms_rows(x_ref[...], g_ref[...])
    o_ref[...] = _dot(h.astype(BF16), w_ref[...])


def norm_matmul(x, g, w_bf16, tm):
    m, d = x.shape
    n = w_bf16.shape[1]
    return pl.pallas_call(
        _norm_matmul_kernel,
        grid=(m // tm,),
        in_specs=[pl.BlockSpec((tm, d), lambda i: (i, 0)),
                  pl.BlockSpec((1, d), lambda i: (0, 0)),
                  pl.BlockSpec((d, n), lambda i: (0, 0))],
        out_specs=pl.BlockSpec((tm, n), lambda i: (i, 0)),
        out_shape=jax.ShapeDtypeStruct((m, n), F32),
        compiler_params=_cparams("parallel"),
        name="norm_matmul",
    )(x, g.reshape(1, d), w_bf16)


def _gated_out_kernel(o_ref, gate_ref, w_ref, x_ref, gn_ref, y_ref, *, head_norm):
    o = o_ref[...]
    if head_norm:
        parts = []
        for h in range(o.shape[1] // LANES):
            parts.append(_rms_rows(o[:, h * LANES:(h + 1) * LANES], gn_ref[...]))
        o = jnp.concatenate(parts, axis=1)
    gate = gate_ref[...]
    z = o * (gate * _sigmoid(gate))
    y_ref[...] = x_ref[...] + _dot(z.astype(BF16), w_ref[...])


def gated_out(o, gate_arr, gate_col, w_bf16, x, tm, g_norm=None):
    m, w = o.shape
    d = x.shape[1]
    head_norm = g_norm is not None
    gn = (g_norm if head_norm else jnp.ones((LANES,), F32)).reshape(1, LANES)
    return pl.pallas_call(
        functools.partial(_gated_out_kernel, head_norm=head_norm),
        grid=(m // tm,),
        in_specs=[pl.BlockSpec((tm, w), lambda i: (i, 0)),
                  pl.BlockSpec((tm, w), lambda i: (i, gate_col)),
                  pl.BlockSpec((w, d), lambda i: (0, 0)),
                  pl.BlockSpec((tm, d), lambda i: (i, 0)),
                  pl.BlockSpec((1, LANES), lambda i: (0, 0))],
        out_specs=pl.BlockSpec((tm, d), lambda i: (i, 0)),
        out_shape=jax.ShapeDtypeStruct((m, d), F32),
        compiler_params=_cparams("parallel"),
        name="gated_out",
    )(o, gate_arr, w_bf16, x, gn)


def _rope_lanes(x, c, s1, s2):
    n = x.shape[1]
    return x * c + pltpu.roll(x, n - A_HALF, 1) * s1 + pltpu.roll(x, A_HALF, 1) * s2


def _head_norm_blocks(x, g):
    parts = []
    for h in range(A_HEADS):
        blk = x[:, h * LANES:(h + 1) * LANES]
        ss = jnp.sum(blk * blk, axis=-1, keepdims=True)
        parts.append(blk * lax.rsqrt(ss * (1.0 / A_QK) + EPS) * g[:, h * LANES:(h + 1) * LANES])
    return jnp.concatenate(parts, axis=1)


def _mla_proj_kernel(x_ref, gn_ref, win_ref, gql_ref, gkv_ref, wuq_ref, gq_ref, wuk_ref, wuvt_ref, gk_ref,
                     c_ref, s1_ref, s2_ref,
                     q_ref, k_ref, vt_ref, gate_ref, ckv_ref, kpe_ref):
    h = _rms_rows(x_ref[...], gn_ref[...])
    p = _dot(h.astype(BF16), win_ref[...])
    gate_ref[...] = p[:, :A_WIDTH]
    cq = p[:, A_WIDTH:A_WIDTH + A_Q_LORA]
    ckv = p[:, A_WIDTH + A_Q_LORA:A_WIDTH + A_Q_LORA + A_KV_LORA]
    kpe = p[:, A_WIDTH + A_Q_LORA + A_KV_LORA:]
    c, s1, s2 = c_ref[...], s1_ref[...], s2_ref[...]
    c8 = jnp.concatenate([c] * A_HEADS, axis=1)
    sin8 = jnp.concatenate([s2 - s1] * A_HEADS, axis=1)

    q2 = _dot(_rms_rows(cq, gql_ref[...]).astype(BF16), wuq_ref[...])
    q = q2[:, :A_PADW] * c8 + q2[:, A_PADW:] * sin8
    q_ref[...] = _head_norm_blocks(q, gq_ref[...]).astype(q_ref.dtype)

    ckvn = _rms_rows(ckv, gkv_ref[...])
    ckv_ref[...] = ckvn
    kper = _rope_lanes(kpe, c, s1, s2)
    kpe_ref[...] = kper
    cb = ckvn.astype(BF16)
    kn = _dot(cb, wuk_ref[...]) + jnp.concatenate([kper] * A_HEADS, axis=1)
    k_ref[...] = _head_norm_blocks(kn, gk_ref[...]).astype(k_ref.dtype)
    vt = _dot_nt(wuvt_ref[...], cb)
    ones = jnp.ones((A_VL - A_V, vt.shape[1]), F32)
    vt_ref[...] = jnp.concatenate(
        [blk for h in range(A_HEADS) for blk in (vt[h * A_V:(h + 1) * A_V, :], ones)], axis=0).astype(vt_ref.dtype)


def mla_proj(x, g_norm, wts, rope_tabs, rows_per_seq, tm, qkv_dtype, q_gain):
    m, d = x.shape
    c, s1, s2 = rope_tabs
    nblk = rows_per_seq // tm
    full = lambda a: pl.BlockSpec(a.shape, lambda i: (0, 0))
    tab = pl.BlockSpec((tm, LANES), lambda i: (i % nblk, 0))
    row = lambda n: pl.BlockSpec((tm, n), lambda i: (i, 0))
    return pl.pallas_call(
        _mla_proj_kernel,
        grid=(m // tm,),
        in_specs=[row(d), pl.BlockSpec((1, d), lambda i: (0, 0)), full(wts["w_in"]),
                  full(wts["g_ql"]), full(wts["g_kv"]), full(wts["w_uq"]), full(wts[q_gain]),
                  full(wts["w_uk"]), full(wts["w_uvt"]), full(wts["g_k"]), tab, tab, tab],
        out_specs=[row(A_PADW), row(A_PADW), pl.BlockSpec((A_HEADS * A_VL, tm), lambda i: (0, i)),
                   row(A_WIDTH), row(A_KV_LORA), row(LANES)],
        out_shape=[jax.ShapeDtypeStruct((m, A_PADW), qkv_dtype),
                   jax.ShapeDtypeStruct((m, A_PADW), qkv_dtype),
                   jax.ShapeDtypeStruct((A_HEADS * A_VL, m), qkv_dtype),
                   jax.ShapeDtypeStruct((m, A_WIDTH), F32),
                   jax.ShapeDtypeStruct((m, A_KV_LORA), F32),
                   jax.ShapeDtypeStruct((m, LANES), F32)],
        compiler_params=_cparams("parallel"),
        name="mla_proj",
    )(x, g_norm.reshape(1, d), wts["w_in"], wts["g_ql"], wts["g_kv"], wts["w_uq"], wts[q_gain],
      wts["w_uk"], wts["w_uvt"], wts["g_k"], c, s1, s2)


def _pad_heads(a, used):
    pad = [(0, 0)] * (a.ndim - 1) + [(0, LANES - used)]
    a = jnp.pad(a, pad)
    return a.reshape(a.shape[:-2] + (a.shape[-2] * LANES,))


def _mla_weights(w_in, g_ql, g_kv, w_uq, w_ukv, g_q, g_k):
    cq, ckv, kpe, gate = jnp.split(w_in, [A_Q_LORA, A_Q_LORA + A_KV_LORA, A_Q_LORA + A_KV_LORA + A_ROPE], axis=1)
    kpe_blk = jnp.pad(kpe, ((0, 0), (A_NOPE, LANES - A_QK)))
    scale = A_QK ** -0.5
    ukv = w_ukv.reshape(A_KV_LORA, A_HEADS, A_NOPE + A_V)
    uk, uv = ukv[:, :, :A_NOPE], ukv[:, :, A_NOPE:]
    ukt_pad = _pad_heads(uk, A_NOPE).T
    ukt_hi = ukt_pad.astype(BF16)
    uq = w_uq.reshape(A_Q_LORA, A_HEADS, A_QK)
    x1, x2 = uq[:, :, A_NOPE:A_NOPE + A_HALF], uq[:, :, A_NOPE + A_HALF:]
    uq_rot = jnp.concatenate([jnp.zeros_like(uq[:, :, :A_NOPE]), -x2, x1], axis=2)
    return {
        "w_in": jnp.concatenate([gate, cq, ckv, kpe_blk], axis=1).astype(BF16),
        "g_ql": g_ql.reshape(1, -1), "g_kv": g_kv.reshape(1, -1),
        "w_uq": jnp.concatenate([_pad_heads(uq, A_QK), _pad_heads(uq_rot, A_QK)], axis=1).astype(BF16),
        "g_q": _pad_heads(jnp.broadcast_to(g_q * scale, (A_HEADS, A_QK)), A_QK).reshape(1, -1),
        "g_q_base2": _pad_heads(jnp.broadcast_to(g_q * (scale * math.log2(math.e)), (A_HEADS, A_QK)),
                                A_QK).reshape(1, -1),
        "g_k": _pad_heads(jnp.broadcast_to(g_k, (A_HEADS, A_QK)), A_QK).reshape(1, -1),
        "w_uk": _pad_heads(uk, A_NOPE).astype(BF16),
        "w_uv": uv.reshape(A_KV_LORA, A_WIDTH).astype(BF16),
        "w_uvt": uv.reshape(A_KV_LORA, A_WIDTH).T.astype(BF16),
        "ukt": uk.reshape(A_KV_LORA, A_HEADS * A_NOPE).T.astype(BF16),
        "ukt_hi": ukt_hi,
        "ukt_lo": (ukt_pad - ukt_hi.astype(F32)).astype(BF16),
    }


def _rope_tables(pos):
    inv = 1.0 / (ROPE_THETA ** (jnp.arange(0, A_ROPE, 2, dtype=F32) / A_ROPE))
    ang = pos.astype(F32)[:, None] * inv[None, :]
    cos, sin = jnp.cos(ang), jnp.sin(ang)
    n = pos.shape[0]
    z = lambda w: jnp.zeros((n, w), F32)
    c = jnp.concatenate([jnp.ones((n, A_NOPE), F32), cos, cos, z(LANES - A_QK)], axis=1)
    s1 = jnp.concatenate([z(A_NOPE), -sin, z(A_HALF), z(LANES - A_QK)], axis=1)
    s2 = jnp.concatenate([z(A_NOPE + A_HALF), sin, z(LANES - A_QK)], axis=1)
    return c, s1, s2


def _flash_kernel(qi_ref, kj_ref, q_ref, k_ref, vt_ref, gate_ref, x_ref, wout_ref, y_ref, m_scr, acc_scr):
    p_id = pl.program_id(1)
    qi = qi_ref[p_id]
    kj = kj_ref[p_id]
    tq, tk = q_ref.shape[0], k_ref.shape[0]

    @pl.when(kj == 0)
    def _():
        m_scr[...] = jnp.full(m_scr.shape, NEG_BIG, F32)
        acc_scr[...] = jnp.zeros(acc_scr.shape, F32)

    def step(masked):
        if masked:
            key = lax.broadcasted_iota(jnp.int32, (tk, tq), 0)
            qry = lax.broadcasted_iota(jnp.int32, (tk, tq), 1)
            keep = key <= qry

        def scores(h):
            return _dot_nt(k_ref[:, h * LANES:(h + 1) * LANES], q_ref[:, h * LANES:(h + 1) * LANES])

        ahead = [scores(h) for h in range(FLASH_LOOKAHEAD)]
        for h in range(A_HEADS):
            st = ahead.pop(0)
            if h + FLASH_LOOKAHEAD < A_HEADS:
                ahead.append(scores(h + FLASH_LOOKAHEAD))
            if masked:
                st = jnp.where(keep, st, NEG_BIG)
            m_prev = m_scr[h]
            m_new = jnp.maximum(m_prev, jnp.max(st, axis=0, keepdims=True))
            p = jnp.exp2(st - m_new)
            corr = jnp.exp2(m_prev - m_new)
            rows = slice(h * A_VL, (h + 1) * A_VL)
            acc_scr[rows, :] = acc_scr[rows, :] * corr + _dot(vt_ref[rows, :], p.astype(BF16))
            m_scr[h] = m_new

    @pl.when(kj < qi)
    def _():
        step(False)

    @pl.when(kj == qi)
    def _():
        step(True)
        ot = jnp.concatenate([acc_scr[h * A_VL:h * A_VL + A_V, :] / acc_scr[h * A_VL + A_V:h * A_VL + A_V + 1, :]
                              for h in range(A_HEADS)], axis=0)
        gate = gate_ref[...]
        z = ot.T * (gate * _sigmoid(gate))
        y_ref[...] = x_ref[...] + _dot(z.astype(BF16), wout_ref[...])


def flash_prompt(q, k, vt, gate, x, w_out_bf16, batch, seq, tq):
    nq = seq // tq
    d = x.shape[1]
    pairs = [(i, j) for i in range(nq) for j in range(i + 1)]
    qi = jnp.asarray([p[0] for p in pairs], jnp.int32)
    kj = jnp.asarray([p[1] for p in pairs], jnp.int32)
    qrow = lambda n: pl.BlockSpec((tq, n), lambda b, p, qi, kj: (b * nq + qi[p], 0))
    grid_spec = pltpu.PrefetchScalarGridSpec(
        num_scalar_prefetch=2,
        grid=(batch, len(pairs)),
        in_specs=[qrow(A_PADW),
                  pl.BlockSpec((tq, A_PADW), lambda b, p, qi, kj: (b * nq + kj[p], 0)),
                  pl.BlockSpec((A_HEADS * A_VL, tq), lambda b, p, qi, kj: (0, b * nq + kj[p])),
                  qrow(A_WIDTH), qrow(d), pl.BlockSpec(w_out_bf16.shape, lambda b, p, qi, kj: (0, 0))],
        out_specs=qrow(d),
        scratch_shapes=[pltpu.VMEM((A_HEADS, 1, tq), F32), pltpu.VMEM((A_HEADS * A_VL, tq), F32)])
    return pl.pallas_call(
        _flash_kernel, grid_spec=grid_spec,
        out_shape=jax.ShapeDtypeStruct((batch * seq, d), F32),
        compiler_params=_cparams("parallel", "arbitrary"),
        name="flash_prompt",
    )(qi, kj, q, k, vt, gate, x, w_out_bf16)


def _absorb_query_kernel(q_ref, gk_ref, ukhi_ref, uklo_ref, qg_ref, qabs_ref):
    qg = q_ref[...] * gk_ref[...]
    qg_ref[...] = qg
    for h in range(A_HEADS):
        blk = qg[:, h * LANES:(h + 1) * LANES]
        q_hi = blk.astype(BF16)
        q_lo = (blk - q_hi.astype(F32)).astype(BF16)
        w_hi = ukhi_ref[h * LANES:(h + 1) * LANES, :]
        qabs_ref[h] = _dot(q_hi, w_hi) + _dot(q_hi, uklo_ref[h * LANES:(h + 1) * LANES, :]) + _dot(q_lo, w_hi)


def absorb_query(q_s, wts):
    db = q_s.shape[0]
    whole = lambda a: pl.BlockSpec(a.shape, lambda i: (0,) * a.ndim)
    return pl.pallas_call(
        _absorb_query_kernel,
        grid=(1,),
        in_specs=[whole(q_s), whole(wts["g_k"]), whole(wts["ukt_hi"]), whole(wts["ukt_lo"])],
        out_specs=[pl.BlockSpec((db, A_PADW), lambda i: (0, 0)),
                   pl.BlockSpec((A_HEADS, db, A_KV_LORA), lambda i: (0, 0, 0))],
        out_shape=[jax.ShapeDtypeStruct((db, A_PADW), F32),
                   jax.ShapeDtypeStruct((A_HEADS, db, A_KV_LORA), F32)],
        compiler_params=_cparams("arbitrary"),
        name="absorb_query",
    )(q_s, wts["g_k"], wts["ukt_hi"], wts["ukt_lo"])


def _sample_attn_kernel(pt_ref, qabs_ref, qpe_ref, ukt_ref, wuv_ref, cnew_ref, knew_ref, ckv_hbm, kpe_hbm, o_ref,
                        lhs_scr, cb_scr, s_scr, ckv_buf, kpe_buf, sems, *, la, n_pages, pages_per_block):
    b = pl.program_id(0)
    last = pl.num_programs(0) - 1
    slot = b % 2
    nxt = jnp.minimum(b + 1, last)
    n_nope = A_HEADS * A_NOPE
    n_past = n_pages * PAGE_SIZE

    def page_copies(seq, to_slot, i):
        page = pt_ref[seq * n_pages + i]
        rows = pl.ds(i * PAGE_SIZE, PAGE_SIZE)
        return (pltpu.make_async_copy(ckv_hbm.at[la, page], ckv_buf.at[to_slot, rows, :], sems.at[0, to_slot]),
                pltpu.make_async_copy(kpe_hbm.at[la, page], kpe_buf.at[to_slot, :, rows], sems.at[1, to_slot]))

    def start_pages(seq, to_slot, pages):
        for i in pages:
            for cp in page_copies(seq, to_slot, i):
                cp.start()

    def wait_pages(to_slot):
        pltpu.make_async_copy(ckv_buf.at[to_slot], ckv_buf.at[to_slot], sems.at[0, to_slot]).wait()
        pltpu.make_async_copy(kpe_buf.at[to_slot], kpe_buf.at[to_slot], sems.at[1, to_slot]).wait()

    @pl.when(b == 0)
    def _():
        start_pages(b, slot, range(n_pages))

    wait_pages(slot)
    lhs_scr[:n_nope, :] = ukt_ref[...]
    lhs_scr[n_nope:, :] = qabs_ref[...]

    def score(c, kpe_t, start):
        n = c.shape[0]
        cb = c.astype(BF16)
        cb_scr[start:start + n, :] = cb
        a = _dot_nt(lhs_scr[...], cb)
        kn = a[:n_nope]
        ss = jnp.sum((kn * kn).reshape(A_HEADS, A_NOPE, n), axis=1)
        s_pe = _dot(qpe_ref[...], kpe_t.astype(BF16))[:A_HEADS]
        pe_ss = jnp.sum(kpe_t * kpe_t, axis=0, keepdims=True)
        rs = lax.rsqrt((ss + pe_ss) * (1.0 / A_QK) + EPS)
        return (a[n_nope:n_nope + A_HEADS] + s_pe) * rs

    for blk in range(0, n_pages, pages_per_block):
        start, n = blk * PAGE_SIZE, pages_per_block * PAGE_SIZE
        s_scr[:, start:start + n] = score(ckv_buf[slot, start:start + n, :], kpe_buf[slot, :, start:start + n], start)
        start_pages(nxt, 1 - slot, range(blk, blk + pages_per_block))
    s_new = score(jnp.broadcast_to(cnew_ref[...], (LANES, A_KV_LORA)),
                  jnp.broadcast_to(knew_ref[...], (A_ROPE, LANES)), n_past)
    s_scr[:, n_past:] = jnp.where(lax.broadcasted_iota(jnp.int32, s_new.shape, 1) < 1, s_new, NEG_BIG)

    s = s_scr[...]
    p = jnp.exp(s - jnp.max(s, axis=-1, keepdims=True))
    pb = p.astype(BF16)
    acc = jnp.concatenate([_dot(pb, cb_scr[:, :LANES]), _dot(pb, cb_scr[:, LANES:])], axis=1)
    lat = acc / jnp.sum(p, axis=-1, keepdims=True)
    full = _dot(lat.astype(BF16), wuv_ref[...])
    lane = lax.broadcasted_iota(jnp.int32, full.shape, 1)
    row = lax.broadcasted_iota(jnp.int32, full.shape, 0)
    sel = (lane >= row * A_V) & (lane < (row + 1) * A_V)
    o_ref[...] = jnp.sum(jnp.where(sel, full, 0.0), axis=0, keepdims=True)

    @pl.when(b == last)
    def _():
        wait_pages(1 - slot)


def mla_sample_attention(q_s, ckv_new, kpe_new, cache_ckv, cache_kpe_t, la, page_table, wts, pages_per_block):
    db, n_pages = page_table.shape
    assert n_pages % pages_per_block == 0
    n_past = n_pages * PAGE_SIZE
    n_tok = n_past + LANES
    row3 = lambda r, n: pl.BlockSpec((None, r, n), lambda b, pt: (b, 0, 0))
    full = lambda a: pl.BlockSpec(a.shape, lambda b, pt: (0, 0))
    hbm = pl.BlockSpec(memory_space=pl.ANY)
    q_rows = 2 * SUBLANES
    lhs_rows = A_HEADS * A_NOPE + q_rows
    qg, qabs = absorb_query(q_s, wts)
    pad_rows = lambda a: jnp.pad(a, ((0, 0), (0, q_rows - A_HEADS), (0, 0))).astype(BF16)
    qabs = pad_rows(qabs.transpose(1, 0, 2))
    qpe = pad_rows(qg.reshape(db, A_HEADS, LANES)[:, :, A_NOPE:A_QK])
    grid_spec = pltpu.PrefetchScalarGridSpec(
        num_scalar_prefetch=1,
        grid=(db,),
        in_specs=[row3(q_rows, A_KV_LORA), row3(q_rows, A_ROPE), full(wts["ukt"]),
                  full(wts["w_uv"]), row3(1, A_KV_LORA), row3(A_ROPE, 1), hbm, hbm],
        out_specs=pl.BlockSpec((None, 1, A_WIDTH), lambda b, pt: (b, 0, 0)),
        scratch_shapes=[pltpu.VMEM((lhs_rows, A_KV_LORA), BF16),
                        pltpu.VMEM((n_tok, A_KV_LORA), BF16), pltpu.VMEM((A_HEADS, n_tok), F32),
                        pltpu.VMEM((2, n_past, A_KV_LORA), F32), pltpu.VMEM((2, A_ROPE, n_past), F32),
                        pltpu.SemaphoreType.DMA((2, 2))])
    out = pl.pallas_call(
        functools.partial(_sample_attn_kernel, la=la, n_pages=n_pages, pages_per_block=pages_per_block),
        grid_spec=grid_spec,
        out_shape=jax.ShapeDtypeStruct((db, 1, A_WIDTH), F32),
        compiler_params=_cparams("arbitrary"),
        name="mla_sample_attention",
    )(page_table.reshape(-1), qabs, qpe, wts["ukt"],
      wts["w_uv"], ckv_new.reshape(db, 1, A_KV_LORA), kpe_new.reshape(db, A_ROPE, 1), cache_ckv, cache_kpe_t)
    return out.reshape(db, A_WIDTH)


def _hgrn_lower_bound(lb_all, li):
    e = jnp.exp(lb_all - jnp.max(lb_all, axis=0, keepdims=True))
    smx = e / jnp.sum(e, axis=0, keepdims=True)
    return jnp.sum(smx[1:li + 1], axis=0, keepdims=True) if li > 0 else jnp.zeros_like(smx[:1])


def _hgrn_gates(q, f, lb):
    g = jnp.log(lb + (1.0 - lb) * _sigmoid(f))
    k = (1.0 - lb) * _sigmoid(-f)
    qf = q * _sigmoid(q) * (B_DK ** -0.5)
    return qf, k, g


def _cumsum_rows(x):
    n = x.shape[0]
    row = lax.broadcasted_iota(jnp.int32, x.shape, 0)
    d = 1
    while d < n:
        x = x + jnp.where(row >= d, pltpu.roll(x, d, 0), 0.0)
        d *= 2
    return x


def _bcast_row_in_groups(x, j):
    n, w = x.shape
    x3 = x.reshape(n // SUBLANES, SUBLANES, w)
    return jnp.broadcast_to(x3[:, j:j + 1, :], x3.shape).reshape(n, w)


def _intra_chunk_masks(c):
    tt = lax.broadcasted_iota(jnp.int32, (c, c), 0)
    ss = lax.broadcasted_iota(jnp.int32, (c, c), 1)
    masks = {"diag": tt == ss}
    half = c // 2
    while half >= 1:
        blk = 2 * half
        keep = ((tt % blk) >= half) & ((ss % blk) < half)
        masks[half] = keep & ((tt // blk) == (ss // blk)) if blk < c else keep
        half //= 2
    return masks


def _intra_chunk_att(qf, k, b, masks):
    c = qf.shape[0]
    row = lax.broadcasted_iota(jnp.int32, (c, 1), 0)
    att = jnp.where(masks["diag"], jnp.sum(qf * k, axis=-1, keepdims=True), 0.0)
    half = c // 2
    while half >= 1:
        blk = 2 * half
        if blk >= SUBLANES:
            bref = jnp.concatenate(
                [jnp.broadcast_to(b[m:m + 1], (blk, b.shape[1])) for m in range(half, c, blk)], axis=0)
        else:
            rm = row % SUBLANES
            bref = _bcast_row_in_groups(b, SUBLANES - half)
            for m in range(SUBLANES - half - blk, 0, -blk):
                bref = jnp.where(rm < m + half, _bcast_row_in_groups(b, m), bref)
        e = jnp.exp(-jnp.abs(b - bref))
        p = _dot_nt((qf * e).astype(BF16), (k * e).astype(BF16))
        att = att + jnp.where(masks[half], p, 0.0)
        half //= 2
    return att


def _hgrn_prompt_kernel(q_ref, f_ref, v_ref, gate_ref, x_ref, wout_ref, gn_ref, lb_ref, y_ref, st_ref,
                        state_scr, o_scr, *, li):
    t = pl.program_id(1)

    @pl.when(t == 0)
    def _():
        state_scr[...] = jnp.zeros(state_scr.shape, F32)

    lb_all = _hgrn_lower_bound(lb_ref[...], li)
    masks = _intra_chunk_masks(B_CHUNK)
    for ci in range(q_ref.shape[0] // B_CHUNK):
        sl = slice(ci * B_CHUNK, (ci + 1) * B_CHUNK)
        for h in range(B_HEADS):
            hl = slice(h * LANES, (h + 1) * LANES)
            qf, k, g = _hgrn_gates(q_ref[sl, hl], f_ref[sl, hl], lb_all[:, hl])
            b = _cumsum_rows(g)
            vb = v_ref[sl, hl].astype(BF16)
            intra = _dot(_intra_chunk_att(qf, k, b, masks).astype(BF16), vb)
            blast = b[B_CHUNK - 1:B_CHUNK]
            kd = (k * jnp.exp(blast - b)).astype(BF16)
            upd = lax.dot_general(vb, kd, (((0,), (0,)), ((), ())), preferred_element_type=F32)
            st = state_scr[h]
            o_scr[sl, hl] = intra + _dot_nt((qf * jnp.exp(b)).astype(BF16), st.astype(BF16))
            state_scr[h] = jnp.exp(blast) * st + upd

    o = jnp.concatenate([_rms_rows(o_scr[:, h * LANES:(h + 1) * LANES], gn_ref[...]) for h in range(B_HEADS)],
                        axis=1)
    gate = gate_ref[...]
    z = o * (gate * _sigmoid(gate))
    y_ref[...] = x_ref[...] + _dot(z.astype(BF16), wout_ref[...])

    @pl.when(t == pl.num_programs(1) - 1)
    def _():
        for h in range(B_HEADS):
            st_ref[h] = state_scr[h].T


def hgrn_prompt(proj, x, w_out_bf16, g_norm, lower_bounds, li, batch, seq, tblk):
    nt = seq // tblk
    w = B_HEADS * B_DK
    d = x.shape[1]
    rows = lambda n, col: pl.BlockSpec((tblk, n), lambda b, t: (b * nt + t, col))
    full = lambda a: pl.BlockSpec(a.shape, lambda b, t: (0,) * a.ndim)
    gn = g_norm.reshape(1, B_DV)
    return pl.pallas_call(
        functools.partial(_hgrn_prompt_kernel, li=li),
        grid=(batch, nt),
        in_specs=[rows(w, 0), rows(w, 1), rows(w, 2), rows(w, 3), rows(d, 0), full(w_out_bf16), full(gn),
                  full(lower_bounds)],
        out_specs=[rows(d, 0), pl.BlockSpec((None, B_HEADS, B_DK, B_DV), lambda b, t: (b, 0, 0, 0))],
        out_shape=[jax.ShapeDtypeStruct((batch * seq, d), F32),
                   jax.ShapeDtypeStruct((batch, B_HEADS, B_DK, B_DV), F32)],
        scratch_shapes=[pltpu.VMEM((B_HEADS, B_DV, B_DK), F32), pltpu.VMEM((tblk, w), F32)],
        compiler_params=_cparams("parallel", "arbitrary"),
        name="hgrn_prompt",
    )(proj, proj, proj, proj, x, w_out_bf16, gn, lower_bounds)


def _hgrn_sample_gates_kernel(p_ref, lb_ref, qe_ref, eg_ref, k_ref, av_ref, *, li):
    w = B_HEADS * B_DK
    lb = _hgrn_lower_bound(lb_ref[...], li)
    qf, k, g = _hgrn_gates(p_ref[:, :w], p_ref[:, w:2 * w], lb)
    v = p_ref[:, 2 * w:3 * w]
    eg = jnp.exp(g)
    qe_ref[...] = qf * eg
    eg_ref[...] = eg
    k_ref[...] = k
    qk = qf * k
    av_ref[...] = jnp.concatenate(
        [jnp.sum(qk[:, h * LANES:(h + 1) * LANES], axis=-1, keepdims=True) * v[:, h * LANES:(h + 1) * LANES]
         for h in range(B_HEADS)], axis=1)


def _hgrn_sample_state_kernel(st_ref, qe_ref, egt_ref, kt_ref, v_ref, av_ref, o_ref, ns_ref):
    for s in range(st_ref.shape[0]):
        for h in range(B_HEADS):
            st = st_ref[s, h]
            ns_ref[s, h] = egt_ref[s, :, h:h + 1] * st + kt_ref[s, :, h:h + 1] * v_ref[s, h:h + 1, :]
            qe = jnp.broadcast_to(qe_ref[s, h:h + 1, :], (2 * SUBLANES, B_DK)).astype(BF16)
            o_ref[s, h:h + 1, :] = _dot(qe, st.astype(BF16))[:1] + av_ref[s, h:h + 1, :]


def hgrn_sample(proj_s, lower_bounds, li, state):
    db = proj_s.shape[0]
    w = B_HEADS * B_DK
    whole = lambda a: pl.BlockSpec(a.shape, lambda i: (0,) * a.ndim)
    qe, eg, k, av = pl.pallas_call(
        functools.partial(_hgrn_sample_gates_kernel, li=li),
        grid=(1,),
        in_specs=[whole(proj_s), whole(lower_bounds)],
        out_specs=[pl.BlockSpec((db, w), lambda i: (0, 0))] * 4,
        out_shape=[jax.ShapeDtypeStruct((db, w), F32)] * 4,
        compiler_params=_cparams("arbitrary"),
        name="hgrn_sample_gates",
    )(proj_s, lower_bounds)
    heads = lambda a: a.reshape(db, B_HEADS, B_DK)
    cols = lambda a: heads(a).transpose(0, 2, 1)
    sb = SEQS_HGRN_SAMPLE
    hrow = pl.BlockSpec((sb, B_HEADS, B_DK), lambda b: (b, 0, 0))
    hcol = pl.BlockSpec((sb, B_DK, B_HEADS), lambda b: (b, 0, 0))
    stspec = pl.BlockSpec((sb, B_HEADS, B_DK, B_DV), lambda b: (b, 0, 0, 0))
    o, new_state = pl.pallas_call(
        _hgrn_sample_state_kernel,
        grid=(db // sb,),
        in_specs=[stspec, hrow, hcol, hcol, hrow, hrow],
        out_specs=[hrow, stspec],
        out_shape=[jax.ShapeDtypeStruct((db, B_HEADS, B_DV), F32),
                   jax.ShapeDtypeStruct(state.shape, F32)],
        compiler_params=_cparams("parallel"),
        name="hgrn_sample_state",
    )(state, heads(qe), cols(eg), cols(k), heads(proj_s[:, 2 * w:3 * w]), heads(av))
    return o.reshape(db, w), new_state


def _rglru_gates(y, wax_ref, ba, bx, lam):
    yb = y.astype(BF16)
    rs, is_ = [], []
    for n in range(C_HEADS):
        ax = _dot(yb[:, n * C_BW:(n + 1) * C_BW], wax_ref[n])
        rs.append(ax[:, :C_BW])
        is_.append(ax[:, C_BW:])
    r = _sigmoid(jnp.concatenate(rs, axis=1) + ba)
    i = _sigmoid(jnp.concatenate(is_, axis=1) + bx)
    softplus = jnp.maximum(-lam, 0.0) + jnp.log1p(jnp.exp(-jnp.abs(lam)))
    log_a = -C_GATE_C * r * softplus
    a = jnp.exp(log_a)
    th = jnp.tanh(log_a)
    u = jnp.sqrt(-2.0 * th / (1.0 - th)) * (i * y)
    return a, u


def _rglru_prompt_kernel(xb_ref, gate_ref, x_ref, wout_ref, cw_ref, cb_ref, wax_ref, ba_ref, bx_ref, lam_ref,
                         y_ref, tail_ref, hl_ref, prev_scr, h_scr, a_scr, u_scr, hs_scr):
    t = pl.program_id(1)
    tb = xb_ref.shape[0]

    @pl.when(t == 0)
    def _():
        prev_scr[...] = jnp.zeros(prev_scr.shape, F32)
        h_scr[...] = jnp.zeros(h_scr.shape, F32)

    xb = xb_ref[...]
    prev = prev_scr[...]
    row = lax.broadcasted_iota(jnp.int32, (tb, 1), 0)
    y = cb_ref[...] + cw_ref[C_CONV - 1:C_CONV, :] * xb
    for d in range(1, C_CONV):
        sh = pltpu.roll(xb, d, 0)
        for r in range(d):
            sh = jnp.where(row == r, prev[SUBLANES - d + r:SUBLANES - d + r + 1, :], sh)
        y = y + cw_ref[C_CONV - 1 - d:C_CONV - d, :] * sh
    prev_scr[...] = xb[tb - SUBLANES:, :]
    a, u = _rglru_gates(y, wax_ref, ba_ref[...], bx_ref[...], lam_ref[...])
    a_scr[...] = a
    u_scr[...] = u

    def body(i, h):
        base = pl.multiple_of(i * SUBLANES, SUBLANES)
        for r in range(SUBLANES):
            h = a_scr[pl.ds(base + r, 1), :] * h + u_scr[pl.ds(base + r, 1), :]
            hs_scr[pl.ds(base + r, 1), :] = h
        return h

    h = lax.fori_loop(0, tb // SUBLANES, body, h_scr[...])
    h_scr[...] = h
    gate = gate_ref[...]
    z = hs_scr[...] * (gate * _sigmoid(gate))
    y_ref[...] = x_ref[...] + _dot(z.astype(BF16), wout_ref[...])

    @pl.when(t == pl.num_programs(1) - 1)
    def _():
        tail_ref[...] = xb[tb - SUBLANES:, :]
        hl_ref[...] = h


def _rglru_weights(conv_w, conv_b, w_a, b_a, w_x, b_x, lam):
    r1 = lambda a: a.reshape(1, -1)
    return {"cw": conv_w, "cb": r1(conv_b), "wax": jnp.concatenate([w_a, w_x], axis=2).astype(BF16),
            "ba": r1(b_a), "bx": r1(b_x), "lam": r1(lam)}


def rglru_prompt(proj, x, w_out_bf16, wts, batch, seq, tblk):
    nt = seq // tblk
    w = C_WIDTH
    d = x.shape[1]
    full = lambda a: pl.BlockSpec(a.shape, lambda b, t: (0,) * a.ndim)
    rows = lambda n, col: pl.BlockSpec((tblk, n), lambda b, t: (b * nt + t, col))
    names = ("cw", "cb", "wax", "ba", "bx", "lam")
    return pl.pallas_call(
        _rglru_prompt_kernel,
        grid=(batch, nt),
        in_specs=[rows(w, 0), rows(w, 1), rows(d, 0), full(w_out_bf16)] + [full(wts[n]) for n in names],
        out_specs=[rows(d, 0),
                   pl.BlockSpec((None, SUBLANES, w), lambda b, t: (b, 0, 0)),
                   pl.BlockSpec((None, 1, w), lambda b, t: (b, 0, 0))],
        out_shape=[jax.ShapeDtypeStruct((batch * seq, d), F32),
                   jax.ShapeDtypeStruct((batch, SUBLANES, w), F32),
                   jax.ShapeDtypeStruct((batch, 1, w), F32)],
        scratch_shapes=[pltpu.VMEM((SUBLANES, w), F32), pltpu.VMEM((1, w), F32),
                        pltpu.VMEM((tblk, w), F32), pltpu.VMEM((tblk, w), F32), pltpu.VMEM((tblk, w), F32)],
        compiler_params=_cparams("parallel", "arbitrary"),
        name="rglru_prompt",
    )(proj, proj, x, w_out_bf16, *[wts[n] for n in names])


def _rglru_sample_kernel(xb_ref, buf_ref, h0_ref, cw_ref, cb_ref, wax_ref, ba_ref, bx_ref, lam_ref,
                         hs_ref, nbuf_ref):
    xb = xb_ref[...]
    y = cb_ref[...] + cw_ref[C_CONV - 1:C_CONV, :] * xb
    for j in range(C_CONV - 1):
        y = y + cw_ref[j:j + 1, :] * buf_ref[j]
    a, u = _rglru_gates(y, wax_ref, ba_ref[...], bx_ref[...], lam_ref[...])
    hs_ref[...] = a * h0_ref[...] + u
    for j in range(C_CONV - 2):
        nbuf_ref[j] = buf_ref[j + 1]
    nbuf_ref[C_CONV - 2] = xb


def rglru_sample(proj_s, buf_t, h0, wts):
    db = proj_s.shape[0]
    w = C_WIDTH
    whole = lambda a: pl.BlockSpec(a.shape, lambda i: (0,) * a.ndim)
    names = ("cw", "cb", "wax", "ba", "bx", "lam")
    return pl.pallas_call(
        _rglru_sample_kernel,
        grid=(1,),
        in_specs=[pl.BlockSpec((db, w), lambda i: (0, 0)), whole(buf_t), whole(h0)] + [whole(wts[n]) for n in names],
        out_specs=[pl.BlockSpec((db, w), lambda i: (0, 0)), whole(buf_t)],
        out_shape=[jax.ShapeDtypeStruct((db, w), F32), jax.ShapeDtypeStruct(buf_t.shape, F32)],
        compiler_params=_cparams("arbitrary"),
        name="rglru_sample",
    )(proj_s, buf_t, h0, *[wts[n] for n in names])


TM_PROMPT = 512
TQ_FLASH = 512
FLASH_LOOKAHEAD = 2
T_HGRN = 256
T_RGLRU = 256
PAGES_PER_BLOCK = 32
SEQS_HGRN_SAMPLE = 4

def kernel(x_prompt, x_sample, cache_ckv, cache_kpe, page_table, state_hgrn, state_conv, state_lru, norm_g, mla_w_in, mla_g_q_lora, mla_g_kv, mla_w_uq, mla_w_ukv, mla_g_q, mla_g_k, mla_w_out, hgrn_w_in, hgrn_lower_bounds, hgrn_g_norm, hgrn_w_out, rglru_w_in, rglru_conv_w, rglru_conv_b, rglru_w_a, rglru_b_a, rglru_w_x, rglru_b_x, rglru_L, rglru_w_out):
    bsz, s_p, d = x_prompt.shape
    db, s_s, _ = x_sample.shape
    assert s_s == 1 and d == D_MODEL
    depth = norm_g.shape[0]
    past_len = page_table.shape[1] * PAGE_SIZE
    xp = x_prompt.reshape(bsz * s_p, d)
    xs = x_sample.reshape(db, d)
    tabs_p = _rope_tables(jnp.arange(s_p))
    tabs_s = _rope_tables(jnp.full((db,), past_len))
    cache_kpe_t = cache_kpe.transpose(0, 1, 3, 2)

    ckv_p, kpe_p, ckv_s, kpe_s = [], [], [], []
    hg_p, hg_s, cv_p, cv_s, lr_p, lr_s = [], [], [], [], [], []
    for li in range(depth):
        kind, j = li % N_MIXERS, li // N_MIXERS
        if kind == 0:
            wts = _mla_weights(mla_w_in[j], mla_g_q_lora[j], mla_g_kv[j], mla_w_uq[j], mla_w_ukv[j],
                               mla_g_q[j], mla_g_k[j])
            w_out = mla_w_out[j].astype(BF16)
            q, k, vt, gate, ckv, kpe_blk = mla_proj(xp, norm_g[li], wts, tabs_p, s_p, TM_PROMPT, BF16, "g_q_base2")
            xp = flash_prompt(q, k, vt, gate, xp, w_out, bsz, s_p, TQ_FLASH)
            ckv_p.append(ckv.reshape(bsz, s_p, A_KV_LORA))
            kpe_p.append(kpe_blk[:, A_NOPE:A_QK].reshape(bsz, s_p, A_ROPE))

            q_s, _, _, gate_s, ckv_n, kpe_blk_s = mla_proj(xs, norm_g[li], wts, tabs_s, db, db, F32, "g_q")
            kpe_n = kpe_blk_s[:, A_NOPE:A_QK]
            o_s = mla_sample_attention(q_s, ckv_n, kpe_n, cache_ckv, cache_kpe_t, j, page_table, wts,
                                       PAGES_PER_BLOCK)
            xs = gated_out(o_s, gate_s, 0, w_out, xs, db)
            ckv_s.append(ckv_n.reshape(db, 1, A_KV_LORA))
            kpe_s.append(kpe_n.reshape(db, 1, A_ROPE))
        elif kind == 1:
            w_in = hgrn_w_in[j].astype(BF16)
            w_out = hgrn_w_out[j].astype(BF16)
            proj = norm_matmul(xp, norm_g[li], w_in, TM_PROMPT // 2)
            xp, st = hgrn_prompt(proj, xp, w_out, hgrn_g_norm[j], hgrn_lower_bounds, li, bsz, s_p, T_HGRN)
            hg_p.append(st)

            proj_s = norm_matmul(xs, norm_g[li], w_in, db)
            o_s, st_s = hgrn_sample(proj_s, hgrn_lower_bounds, li, state_hgrn[j])
            xs = gated_out(o_s, proj_s, 3, w_out, xs, db, hgrn_g_norm[j])
            hg_s.append(st_s)
        else:
            w_in = rglru_w_in[j].astype(BF16)
            w_out = rglru_w_out[j].astype(BF16)
            wts = _rglru_weights(rglru_conv_w[j], rglru_conv_b[j], rglru_w_a[j], rglru_b_a[j],
                                 rglru_w_x[j], rglru_b_x[j], rglru_L[j])
            proj = norm_matmul(xp, norm_g[li], w_in, TM_PROMPT)
            xp, tail, hl = rglru_prompt(proj, xp, w_out, wts, bsz, s_p, T_RGLRU)
            cv_p.append(tail[:, SUBLANES - (C_CONV - 1):, :])
            lr_p.append(hl.reshape(bsz, C_WIDTH))

            proj_s = norm_matmul(xs, norm_g[li], w_in, db)
            hs_s, nbuf = rglru_sample(proj_s, state_conv[j].transpose(1, 0, 2), state_lru[j], wts)
            xs = gated_out(hs_s, proj_s, 1, w_out, xs, db)
            cv_s.append(nbuf.transpose(1, 0, 2))
            lr_s.append(hs_s)

    return (xp.reshape(bsz, s_p, d), xs.reshape(db, 1, d),
            jnp.stack(ckv_p), jnp.stack(kpe_p), jnp.stack(ckv_s), jnp.stack(kpe_s),
            jnp.stack(hg_p), jnp.stack(hg_s), jnp.stack(cv_p), jnp.stack(cv_s),
            jnp.stack(lr_p), jnp.stack(lr_s))
```

```python
import functools
import math

import jax
import jax.numpy as jnp
from jax import lax
from jax.experimental import pallas as pl
from jax.experimental.pallas import tpu as pltpu

F32 = jnp.float32
BF16 = jnp.bfloat16

LANES = 128
SUBLANES = 8
VMEM_LIMIT_BYTES = 56 * 1024 * 1024

D_MODEL = 1024
PAGE_SIZE = 128
N_MIXERS = 3
EPS = 1e-6
ROPE_THETA = 10000.0
NEG_BIG = -1e30

A_HEADS = 8
A_NOPE = 64
A_ROPE = 32
A_QK = A_NOPE + A_ROPE
A_V = 64
A_Q_LORA = 384
A_KV_LORA = 256
A_WIDTH = A_HEADS * A_V
A_HALF = A_ROPE // 2
A_PADW = A_HEADS * LANES
A_VL = A_V + 2 * SUBLANES

B_HEADS = 8
B_DK = 128
B_DV = 128
B_CHUNK = 64

C_WIDTH = 1024
C_HEADS = 8
C_BW = 128
C_CONV = 4
C_GATE_C = 8.0


def _cparams(*sem):
    return pltpu.CompilerParams(dimension_semantics=sem, vmem_limit_bytes=VMEM_LIMIT_BYTES)


def _sigmoid(x):
    return 1.0 / (1.0 + jnp.exp(-x))


def _rms_rows(x, g):
    ms = jnp.mean(x * x, axis=-1, keepdims=True)
    return x * lax.rsqrt(ms + EPS) * g


def _dot_nt(a, b):
    return lax.dot_general(a, b, (((1,), (1,)), ((), ())), preferred_element_type=F32)


def _dot(a, b):
    return jnp.dot(a, b, preferred_element_type=F32)


def _norm_matmul_kernel(x_ref, g_ref, w_ref, o_ref):
    h = _rms_rows(x_ref[...], g_ref[...])
    o_ref[...] = _dot(h.astype(BF16), w_ref[...])


def norm_matmul(x, g, w_bf16, tm):
    m, d = x.shape
    n = w_bf16.shape[1]
    return pl.pallas_call(
        _norm_matmul_kernel,
        grid=(m // tm,),
        in_specs=[pl.BlockSpec((tm, d), lambda i: (i, 0)),
                  pl.BlockSpec((1, d), lambda i: (0, 0)),
                  pl.BlockSpec((d, n), lambda i: (0, 0))],
        out_specs=pl.BlockSpec((tm, n), lambda i: (i, 0)),
        out_shape=jax.ShapeDtypeStruct((m, n), F32),
        compiler_params=_cparams("parallel"),
        name="norm_matmul",
    )(x, g.reshape(1, d), w_bf16)


def _gated_out_kernel(o_ref, gate_ref, w_ref, x_ref, gn_ref, y_ref, *, head_norm):
    o = o_ref[...]
    if head_norm:
        parts = []
        for h in range(o.shape[1] // LANES):
            parts.append(_rms_rows(o[:, h * LANES:(h + 1) * LANES], gn_ref[...]))
        o = jnp.concatenate(parts, axis=1)
    gate = gate_ref[...]
    z = o * (gate * _sigmoid(gate))
    y_ref[...] = x_ref[...] + _dot(z.astype(BF16), w_ref[...])


def gated_out(o, gate_arr, gate_col, w_bf16, x, tm, g_norm=None):
    m, w = o.shape
    d = x.shape[1]
    head_norm = g_norm is not None
    gn = (g_norm if head_norm else jnp.ones((LANES,), F32)).reshape(1, LANES)
    return pl.pallas_call(
        functools.partial(_gated_out_kernel, head_norm=head_norm),
        grid=(m // tm,),
        in_specs=[pl.BlockSpec((tm, w), lambda i: (i, 0)),
                  pl.BlockSpec((tm, w), lambda i: (i, gate_col)),
                  pl.BlockSpec((w, d), lambda i: (0, 0)),
                  pl.BlockSpec((tm, d), lambda i: (i, 0)),
                  pl.BlockSpec((1, LANES), lambda i: (0, 0))],
        out_specs=pl.BlockSpec((tm, d), lambda i: (i, 0)),
        out_shape=jax.ShapeDtypeStruct((m, d), F32),
        compiler_params=_cparams("parallel"),
        name="gated_out",
    )(o, gate_arr, w_bf16, x, gn)


def _rope_lanes(x, c, s1, s2):
    n = x.shape[1]
    return x * c + pltpu.roll(x, n - A_HALF, 1) * s1 + pltpu.roll(x, A_HALF, 1) * s2


def _head_norm_blocks(x, g):
    parts = []
    for h in range(A_HEADS):
        blk = x[:, h * LANES:(h + 1) * LANES]
        ss = jnp.sum(blk * blk, axis=-1, keepdims=True)
        parts.append(blk * lax.rsqrt(ss * (1.0 / A_QK) + EPS) * g[:, h * LANES:(h + 1) * LANES])
    return jnp.concatenate(parts, axis=1)


def _mla_proj_kernel(x_ref, gn_ref, win_ref, gql_ref, gkv_ref, wuq_ref, gq_ref, wuk_ref, wuvt_ref, gk_ref,
                     c_ref, s1_ref, s2_ref,
                     q_ref, k_ref, vt_ref, gate_ref, ckv_ref, kpe_ref):
    h = _rms_rows(x_ref[...], gn_ref[...])
    p = _dot(h.astype(BF16), win_ref[...])
    gate_ref[...] = p[:, :A_WIDTH]
    cq = p[:, A_WIDTH:A_WIDTH + A_Q_LORA]
    ckv = p[:, A_WIDTH + A_Q_LORA:A_WIDTH + A_Q_LORA + A_KV_LORA]
    kpe = p[:, A_WIDTH + A_Q_LORA + A_KV_LORA:]
    c, s1, s2 = c_ref[...], s1_ref[...], s2_ref[...]
    c8 = jnp.concatenate([c] * A_HEADS, axis=1)
    sin8 = jnp.concatenate([s2 - s1] * A_HEADS, axis=1)

    q2 = _dot(_rms_rows(cq, gql_ref[...]).astype(BF16), wuq_ref[...])
    q = q2[:, :A_PADW] * c8 + q2[:, A_PADW:] * sin8
    q_ref[...] = _head_norm_blocks(q, gq_ref[...]).astype(q_ref.dtype)

    ckvn = _rms_rows(ckv, gkv_ref[...])
    ckv_ref[...] = ckvn
    kper = _rope_lanes(kpe, c, s1, s2)
    kpe_ref[...] = kper
    cb = ckvn.astype(BF16)
    kn = _dot(cb, wuk_ref[...]) + jnp.concatenate([kper] * A_HEADS, axis=1)
    k_ref[...] = _head_norm_blocks(kn, gk_ref[...]).astype(k_ref.dtype)
    vt = _dot_nt(wuvt_ref[...], cb)
    ones = jnp.ones((A_VL - A_V, vt.shape[1]), F32)
    vt_ref[...] = jnp.concatenate(
        [blk for h in range(A_HEADS) for blk in (vt[h * A_V:(h + 1) * A_V, :], ones)], axis=0).astype(vt_ref.dtype)


def mla_proj(x, g_norm, wts, rope_tabs, rows_per_seq, tm, qkv_dtype, q_gain):
    m, d = x.shape
    c, s1, s2 = rope_tabs
    nblk = rows_per_seq // tm
    full = lambda a: pl.BlockSpec(a.shape, lambda i: (0, 0))
    tab = pl.BlockSpec((tm, LANES), lambda i: (i % nblk, 0))
    row = lambda n: pl.BlockSpec((tm, n), lambda i: (i, 0))
    return pl.pallas_call(
        _mla_proj_kernel,
        grid=(m // tm,),
        in_specs=[row(d), pl.BlockSpec((1, d), lambda i: (0, 0)), full(wts["w_in"]),
                  full(wts["g_ql"]), full(wts["g_kv"]), full(wts["w_uq"]), full(wts[q_gain]),
                  full(wts["w_uk"]), full(wts["w_uvt"]), full(wts["g_k"]), tab, tab, tab],
        out_specs=[row(A_PADW), row(A_PADW), pl.BlockSpec((A_HEADS * A_VL, tm), lambda i: (0, i)),
                   row(A_WIDTH), row(A_KV_LORA), row(LANES)],
        out_shape=[jax.ShapeDtypeStruct((m, A_PADW), qkv_dtype),
                   jax.ShapeDtypeStruct((m, A_PADW), qkv_dtype),
                   jax.ShapeDtypeStruct((A_HEADS * A_VL, m), qkv_dtype),
                   jax.ShapeDtypeStruct((m, A_WIDTH), F32),
                   jax.ShapeDtypeStruct((m, A_KV_LORA), F32),
                   jax.ShapeDtypeStruct((m, LANES), F32)],
        compiler_params=_cparams("parallel"),
        name="mla_proj",
    )(x, g_norm.reshape(1, d), wts["w_in"], wts["g_ql"], wts["g_kv"], wts["w_uq"], wts[q_gain],
      wts["w_uk"], wts["w_uvt"], wts["g_k"], c, s1, s2)


def _pad_heads(a, used):
    pad = [(0, 0)] * (a.ndim - 1) + [(0, LANES - used)]
    a = jnp.pad(a, pad)
    return a.reshape(a.shape[:-2] + (a.shape[-2] * LANES,))


def _mla_weights(w_in, g_ql, g_kv, w_uq, w_ukv, g_q, g_k):
    cq, ckv, kpe, gate = jnp.split(w_in, [A_Q_LORA, A_Q_LORA + A_KV_LORA, A_Q_LORA + A_KV_LORA + A_ROPE], axis=1)
    kpe_blk = jnp.pad(kpe, ((0, 0), (A_NOPE, LANES - A_QK)))
    scale = A_QK ** -0.5
    ukv = w_ukv.reshape(A_KV_LORA, A_HEADS, A_NOPE + A_V)
    uk, uv = ukv[:, :, :A_NOPE], ukv[:, :, A_NOPE:]
    ukt_pad = _pad_heads(uk, A_NOPE).T
    ukt_hi = ukt_pad.astype(BF16)
    uq = w_uq.reshape(A_Q_LORA, A_HEADS, A_QK)
    x1, x2 = uq[:, :, A_NOPE:A_NOPE + A_HALF], uq[:, :, A_NOPE + A_HALF:]
    uq_rot = jnp.concatenate([jnp.zeros_like(uq[:, :, :A_NOPE]), -x2, x1], axis=2)
    return {
        "w_in": jnp.concatenate([gate, cq, ckv, kpe_blk], axis=1).astype(BF16),
        "g_ql": g_ql.reshape(1, -1), "g_kv": g_kv.reshape(1, -1),
        "w_uq": jnp.concatenate([_pad_heads(uq, A_QK), _pad_heads(uq_rot, A_QK)], axis=1).astype(BF16),
        "g_q": _pad_heads(jnp.broadcast_to(g_q * scale, (A_HEADS, A_QK)), A_QK).reshape(1, -1),
        "g_q_base2": _pad_heads(jnp.broadcast_to(g_q * (scale * math.log2(math.e)), (A_HEADS, A_QK)),
                                A_QK).reshape(1, -1),
        "g_k": _pad_heads(jnp.broadcast_to(g_k, (A_HEADS, A_QK)), A_QK).reshape(1, -1),
        "w_uk": _pad_heads(uk, A_NOPE).astype(BF16),
        "w_uv": uv.reshape(A_KV_LORA, A_WIDTH).astype(BF16),
        "w_uvt": uv.reshape(A_KV_LORA, A_WIDTH).T.astype(BF16),
        "ukt": uk.reshape(A_KV_LORA, A_HEADS * A_NOPE).T.astype(BF16),
        "ukt_hi": ukt_hi,
        "ukt_lo": (ukt_pad - ukt_hi.astype(F32)).astype(BF16),
    }


def _rope_tables(pos):
    inv = 1.0 / (ROPE_THETA ** (jnp.arange(0, A_ROPE, 2, dtype=F32) / A_ROPE))
    ang = pos.astype(F32)[:, None] * inv[None, :]
    cos, sin = jnp.cos(ang), jnp.sin(ang)
    n = pos.shape[0]
    z = lambda w: jnp.zeros((n, w), F32)
    c = jnp.concatenate([jnp.ones((n, A_NOPE), F32), cos, cos, z(LANES - A_QK)], axis=1)
    s1 = jnp.concatenate([z(A_NOPE), -sin, z(A_HALF), z(LANES - A_QK)], axis=1)
    s2 = jnp.concatenate([z(A_NOPE + A_HALF), sin, z(LANES - A_QK)], axis=1)
    return c, s1, s2


def _flash_kernel(qi_ref, kj_ref, q_ref, k_ref, vt_ref, gate_ref, x_ref, wout_ref, y_ref, m_scr, acc_scr):
    p_id = pl.program_id(1)
    qi = qi_ref[p_id]
    kj = kj_ref[p_id]
    tq, tk = q_ref.shape[0], k_ref.shape[0]

    @pl.when(kj == 0)
    def _():
        m_scr[...] = jnp.full(m_scr.shape, NEG_BIG, F32)
        acc_scr[...] = jnp.zeros(acc_scr.shape, F32)

    def step(masked):
        if masked:
            key = lax.broadcasted_iota(jnp.int32, (tk, tq), 0)
            qry = lax.broadcasted_iota(jnp.int32, (tk, tq), 1)
            keep = key <= qry

        def scores(h):
            return _dot_nt(k_ref[:, h * LANES:(h + 1) * LANES], q_ref[:, h * LANES:(h + 1) * LANES])

        ahead = [scores(h) for h in range(FLASH_LOOKAHEAD)]
        for h in range(A_HEADS):
            st = ahead.pop(0)
            if h + FLASH_LOOKAHEAD < A_HEADS:
                ahead.append(scores(h + FLASH_LOOKAHEAD))
            if masked:
                st = jnp.where(keep, st, NEG_BIG)
            m_prev = m_scr[h]
            m_new = jnp.maximum(m_prev, jnp.max(st, axis=0, keepdims=True))
            p = jnp.exp2(st - m_new)
            corr = jnp.exp2(m_prev - m_new)
            rows = slice(h * A_VL, (h + 1) * A_VL)
            acc_scr[rows, :] = acc_scr[rows, :] * corr + _dot(vt_ref[rows, :], p.astype(BF16))
            m_scr[h] = m_new

    @pl.when(kj < qi)
    def _():
        step(False)

    @pl.when(kj == qi)
    def _():
        step(True)
        ot = jnp.concatenate([acc_scr[h * A_VL:h * A_VL + A_V, :] / acc_scr[h * A_VL + A_V:h * A_VL + A_V + 1, :]
                              for h in range(A_HEADS)], axis=0)
        gate = gate_ref[...]
        z = ot.T * (gate * _sigmoid(gate))
        y_ref[...] = x_ref[...] + _dot(z.astype(BF16), wout_ref[...])


def flash_prompt(q, k, vt, gate, x, w_out_bf16, batch, seq, tq):
    nq = seq // tq
    d = x.shape[1]
    pairs = [(i, j) for i in range(nq) for j in range(i + 1)]
    qi = jnp.asarray([p[0] for p in pairs], jnp.int32)
    kj = jnp.asarray([p[1] for p in pairs], jnp.int32)
    qrow = lambda n: pl.BlockSpec((tq, n), lambda b, p, qi, kj: (b * nq + qi[p], 0))
    grid_spec = pltpu.PrefetchScalarGridSpec(
        num_scalar_prefetch=2,
        grid=(batch, len(pairs)),
        in_specs=[qrow(A_PADW),
                  pl.BlockSpec((tq, A_PADW), lambda b, p, qi, kj: (b * nq + kj[p], 0)),
                  pl.BlockSpec((A_HEADS * A_VL, tq), lambda b, p, qi, kj: (0, b * nq + kj[p])),
                  qrow(A_WIDTH), qrow(d), pl.BlockSpec(w_out_bf16.shape, lambda b, p, qi, kj: (0, 0))],
        out_specs=qrow(d),
        scratch_shapes=[pltpu.VMEM((A_HEADS, 1, tq), F32), pltpu.VMEM((A_HEADS * A_VL, tq), F32)])
    return pl.pallas_call(
        _flash_kernel, grid_spec=grid_spec,
        out_shape=jax.ShapeDtypeStruct((batch * seq, d), F32),
        compiler_params=_cparams("parallel", "arbitrary"),
        name="flash_prompt",
    )(qi, kj, q, k, vt, gate, x, w_out_bf16)


def _absorb_query_kernel(q_ref, gk_ref, ukhi_ref, uklo_ref, qg_ref, qabs_ref):
    qg = q_ref[...] * gk_ref[...]
    qg_ref[...] = qg
    for h in range(A_HEADS):
        blk = qg[:, h * LANES:(h + 1) * LANES]
        q_hi = blk.astype(BF16)
        q_lo = (blk - q_hi.astype(F32)).astype(BF16)
        w_hi = ukhi_ref[h * LANES:(h + 1) * LANES, :]
        qabs_ref[h] = _dot(q_hi, w_hi) + _dot(q_hi, uklo_ref[h * LANES:(h + 1) * LANES, :]) + _dot(q_lo, w_hi)


def absorb_query(q_s, wts):
    db = q_s.shape[0]
    whole = lambda a: pl.BlockSpec(a.shape, lambda i: (0,) * a.ndim)
    return pl.pallas_call(
        _absorb_query_kernel,
        grid=(1,),
        in_specs=[whole(q_s), whole(wts["g_k"]), whole(wts["ukt_hi"]), whole(wts["ukt_lo"])],
        out_specs=[pl.BlockSpec((db, A_PADW), lambda i: (0, 0)),
                   pl.BlockSpec((A_HEADS, db, A_KV_LORA), lambda i: (0, 0, 0))],
        out_shape=[jax.ShapeDtypeStruct((db, A_PADW), F32),
                   jax.ShapeDtypeStruct((A_HEADS, db, A_KV_LORA), F32)],
        compiler_params=_cparams("arbitrary"),
        name="absorb_query",
    )(q_s, wts["g_k"], wts["ukt_hi"], wts["ukt_lo"])


def _sample_attn_kernel(pt_ref, qabs_ref, qpe_ref, ukt_ref, wuv_ref, cnew_ref, knew_ref, ckv_hbm, kpe_hbm, o_ref,
                        lhs_scr, cb_scr, s_scr, ckv_buf, kpe_buf, sems, *, la, n_pages, pages_per_block):
    b = pl.program_id(0)
    last = pl.num_programs(0) - 1
    slot = b % 2
    nxt = jnp.minimum(b + 1, last)
    n_nope = A_HEADS * A_NOPE
    n_past = n_pages * PAGE_SIZE

    def page_copies(seq, to_slot, i):
        page = pt_ref[seq * n_pages + i]
        rows = pl.ds(i * PAGE_SIZE, PAGE_SIZE)
        return (pltpu.make_async_copy(ckv_hbm.at[la, page], ckv_buf.at[to_slot, rows, :], sems.at[0, to_slot]),
                pltpu.make_async_copy(kpe_hbm.at[la, page], kpe_buf.at[to_slot, :, rows], sems.at[1, to_slot]))

    def start_pages(seq, to_slot, pages):
        for i in pages:
            for cp in page_copies(seq, to_slot, i):
                cp.start()

    def wait_pages(to_slot):
        pltpu.make_async_copy(ckv_buf.at[to_slot], ckv_buf.at[to_slot], sems.at[0, to_slot]).wait()
        pltpu.make_async_copy(kpe_buf.at[to_slot], kpe_buf.at[to_slot], sems.at[1, to_slot]).wait()

    @pl.when(b == 0)
    def _():
        start_pages(b, slot, range(n_pages))

    wait_pages(slot)
    lhs_scr[:n_nope, :] = ukt_ref[...]
    lhs_scr[n_nope:, :] = qabs_ref[...]

    def score(c, kpe_t, start):
        n = c.shape[0]
        cb = c.astype(BF16)
        cb_scr[start:start + n, :] = cb
        a = _dot_nt(lhs_scr[...], cb)
        kn = a[:n_nope]
        ss = jnp.sum((kn * kn).reshape(A_HEADS, A_NOPE, n), axis=1)
        s_pe = _dot(qpe_ref[...], kpe_t.astype(BF16))[:A_HEADS]
        pe_ss = jnp.sum(kpe_t * kpe_t, axis=0, keepdims=True)
        rs = lax.rsqrt((ss + pe_ss) * (1.0 / A_QK) + EPS)
        return (a[n_nope:n_nope + A_HEADS] + s_pe) * rs

    start_pages(nxt, 1 - slot, range(n_pages // 2))
    for blk in range(0, n_pages, pages_per_block):
        start, n = blk * PAGE_SIZE, pages_per_block * PAGE_SIZE
        s_scr[:, start:start + n] = score(ckv_buf[slot, start:start + n, :], kpe_buf[slot, :, start:start + n], start)
        if blk == 0:
            start_pages(nxt, 1 - slot, range(n_pages // 2, n_pages))
    s_new = score(jnp.broadcast_to(cnew_ref[...], (LANES, A_KV_LORA)),
                  jnp.broadcast_to(knew_ref[...], (A_ROPE, LANES)), n_past)
    s_scr[:, n_past:] = jnp.where(lax.broadcasted_iota(jnp.int32, s_new.shape, 1) < 1, s_new, NEG_BIG)

    s = s_scr[...]
    p = jnp.exp(s - jnp.max(s, axis=-1, keepdims=True))
    pb = p.astype(BF16)
    acc = jnp.concatenate([_dot(pb, cb_scr[:, :LANES]), _dot(pb, cb_scr[:, LANES:])], axis=1)
    lat = acc / jnp.sum(p, axis=-1, keepdims=True)
    full = _dot(lat.astype(BF16), wuv_ref[...])
    lane = lax.broadcasted_iota(jnp.int32, full.shape, 1)
    row = lax.broadcasted_iota(jnp.int32, full.shape, 0)
    sel = (lane >= row * A_V) & (lane < (row + 1) * A_V)
    o_ref[...] = jnp.sum(jnp.where(sel, full, 0.0), axis=0, keepdims=True)

    @pl.when(b == last)
    def _():
        wait_pages(1 - slot)


def mla_sample_attention(q_s, ckv_new, kpe_new, cache_ckv, cache_kpe_t, la, page_table, wts, pages_per_block):
    db, n_pages = page_table.shape
    assert n_pages % pages_per_block == 0
    n_past = n_pages * PAGE_SIZE
    n_tok = n_past + LANES
    row3 = lambda r, n: pl.BlockSpec((None, r, n), lambda b, pt: (b, 0, 0))
    full = lambda a: pl.BlockSpec(a.shape, lambda b, pt: (0, 0))
    hbm = pl.BlockSpec(memory_space=pl.ANY)
    q_rows = 2 * SUBLANES
    lhs_rows = A_HEADS * A_NOPE + q_rows
    qg, qabs = absorb_query(q_s, wts)
    pad_rows = lambda a: jnp.pad(a, ((0, 0), (0, q_rows - A_HEADS), (0, 0))).astype(BF16)
    qabs = pad_rows(qabs.transpose(1, 0, 2))
    qpe = pad_rows(qg.reshape(db, A_HEADS, LANES)[:, :, A_NOPE:A_QK])
    grid_spec = pltpu.PrefetchScalarGridSpec(
        num_scalar_prefetch=1,
        grid=(db,),
        in_specs=[row3(q_rows, A_KV_LORA), row3(q_rows, A_ROPE), full(wts["ukt"]),
                  full(wts["w_uv"]), row3(1, A_KV_LORA), row3(A_ROPE, 1), hbm, hbm],
        out_specs=pl.BlockSpec((None, 1, A_WIDTH), lambda b, pt: (b, 0, 0)),
        scratch_shapes=[pltpu.VMEM((lhs_rows, A_KV_LORA), BF16),
                        pltpu.VMEM((n_tok, A_KV_LORA), BF16), pltpu.VMEM((A_HEADS, n_tok), F32),
                        pltpu.VMEM((2, n_past, A_KV_LORA), F32), pltpu.VMEM((2, A_ROPE, n_past), F32),
                        pltpu.SemaphoreType.DMA((2, 2))])
    out = pl.pallas_call(
        functools.partial(_sample_attn_kernel, la=la, n_pages=n_pages, pages_per_block=pages_per_block),
        grid_spec=grid_spec,
        out_shape=jax.ShapeDtypeStruct((db, 1, A_WIDTH), F32),
        compiler_params=_cparams("arbitrary"),
        name="mla_sample_attention",
    )(page_table.reshape(-1), qabs, qpe, wts["ukt"],
      wts["w_uv"], ckv_new.reshape(db, 1, A_KV_LORA), kpe_new.reshape(db, A_ROPE, 1), cache_ckv, cache_kpe_t)
    return out.reshape(db, A_WIDTH)


def _hgrn_lower_bound(lb_all, li):
    e = jnp.exp(lb_all - jnp.max(lb_all, axis=0, keepdims=True))
    smx = e / jnp.sum(e, axis=0, keepdims=True)
    return jnp.sum(smx[1:li + 1], axis=0, keepdims=True) if li > 0 else jnp.zeros_like(smx[:1])


def _hgrn_gates(q, f, lb):
    g = jnp.log(lb + (1.0 - lb) * _sigmoid(f))
    k = (1.0 - lb) * _sigmoid(-f)
    qf = q * _sigmoid(q) * (B_DK ** -0.5)
    return qf, k, g


def _cumsum_rows(x):
    n = x.shape[0]
    row = lax.broadcasted_iota(jnp.int32, x.shape, 0)
    d = 1
    while d < n:
        x = x + jnp.where(row >= d, pltpu.roll(x, d, 0), 0.0)
        d *= 2
    return x


def _bcast_row_in_groups(x, j):
    n, w = x.shape
    x3 = x.reshape(n // SUBLANES, SUBLANES, w)
    return jnp.broadcast_to(x3[:, j:j + 1, :], x3.shape).reshape(n, w)


def _intra_chunk_masks(c):
    tt = lax.broadcasted_iota(jnp.int32, (c, c), 0)
    ss = lax.broadcasted_iota(jnp.int32, (c, c), 1)
    masks = {"diag": tt == ss}
    half = c // 2
    while half >= 1:
        blk = 2 * half
        keep = ((tt % blk) >= half) & ((ss % blk) < half)
        masks[half] = keep & ((tt // blk) == (ss // blk)) if blk < c else keep
        half //= 2
    return masks


def _intra_chunk_att(qf, k, b, masks):
    c = qf.shape[0]
    row = lax.broadcasted_iota(jnp.int32, (c, 1), 0)
    att = jnp.where(masks["diag"], jnp.sum(qf * k, axis=-1, keepdims=True), 0.0)
    half = c // 2
    while half >= 1:
        blk = 2 * half
        if blk >= SUBLANES:
            bref = jnp.concatenate(
                [jnp.broadcast_to(b[m:m + 1], (blk, b.shape[1])) for m in range(half, c, blk)], axis=0)
        else:
            rm = row % SUBLANES
            bref = _bcast_row_in_groups(b, SUBLANES - half)
            for m in range(SUBLANES - half - blk, 0, -blk):
                bref = jnp.where(rm < m + half, _bcast_row_in_groups(b, m), bref)
        e = jnp.exp(-jnp.abs(b - bref))
        p = _dot_nt((qf * e).astype(BF16), (k * e).astype(BF16))
        att = att + jnp.where(masks[half], p, 0.0)
        half //= 2
    return att


def _hgrn_prompt_kernel(q_ref, f_ref, v_ref, gate_ref, x_ref, wout_ref, gn_ref, lb_ref, y_ref, st_ref,
                        state_scr, o_scr, *, li):
    t = pl.program_id(1)

    @pl.when(t == 0)
    def _():
        state_scr[...] = jnp.zeros(state_scr.shape, F32)

    lb_all = _hgrn_lower_bound(lb_ref[...], li)
    masks = _intra_chunk_masks(B_CHUNK)
    for ci in range(q_ref.shape[0] // B_CHUNK):
        sl = slice(ci * B_CHUNK, (ci + 1) * B_CHUNK)
        for h in range(B_HEADS):
            hl = slice(h * LANES, (h + 1) * LANES)
            qf, k, g = _hgrn_gates(q_ref[sl, hl], f_ref[sl, hl], lb_all[:, hl])
            b = _cumsum_rows(g)
            vb = v_ref[sl, hl].astype(BF16)
            intra = _dot(_intra_chunk_att(qf, k, b, masks).astype(BF16), vb)
            blast = b[B_CHUNK - 1:B_CHUNK]
            kd = (k * jnp.exp(blast - b)).astype(BF16)
            upd = lax.dot_general(vb, kd, (((0,), (0,)), ((), ())), preferred_element_type=F32)
            st = state_scr[h]
            o_scr[sl, hl] = intra + _dot_nt((qf * jnp.exp(b)).astype(BF16), st.astype(BF16))
            state_scr[h] = jnp.exp(blast) * st + upd

    o = jnp.concatenate([_rms_rows(o_scr[:, h * LANES:(h + 1) * LANES], gn_ref[...]) for h in range(B_HEADS)],
                        axis=1)
    gate = gate_ref[...]
    z = o * (gate * _sigmoid(gate))
    y_ref[...] = x_ref[...] + _dot(z.astype(BF16), wout_ref[...])

    @pl.when(t == pl.num_programs(1) - 1)
    def _():
        for h in range(B_HEADS):
            st_ref[h] = state_scr[h].T


def hgrn_prompt(proj, x, w_out_bf16, g_norm, lower_bounds, li, batch, seq, tblk):
    nt = seq // tblk
    w = B_HEADS * B_DK
    d = x.shape[1]
    rows = lambda n, col: pl.BlockSpec((tblk, n), lambda b, t: (b * nt + t, col))
    full = lambda a: pl.BlockSpec(a.shape, lambda b, t: (0,) * a.ndim)
    gn = g_norm.reshape(1, B_DV)
    return pl.pallas_call(
        functools.partial(_hgrn_prompt_kernel, li=li),
        grid=(batch, nt),
        in_specs=[rows(w, 0), rows(w, 1), rows(w, 2), rows(w, 3), rows(d, 0), full(w_out_bf16), full(gn),
                  full(lower_bounds)],
        out_specs=[rows(d, 0), pl.BlockSpec((None, B_HEADS, B_DK, B_DV), lambda b, t: (b, 0, 0, 0))],
        out_shape=[jax.ShapeDtypeStruct((batch * seq, d), F32),
                   jax.ShapeDtypeStruct((batch, B_HEADS, B_DK, B_DV), F32)],
        scratch_shapes=[pltpu.VMEM((B_HEADS, B_DV, B_DK), F32), pltpu.VMEM((tblk, w), F32)],
        compiler_params=_cparams("parallel", "arbitrary"),
        name="hgrn_prompt",
    )(proj, proj, proj, proj, x, w_out_bf16, gn, lower_bounds)


def _hgrn_sample_gates_kernel(p_ref, lb_ref, qe_ref, eg_ref, k_ref, av_ref, *, li):
    w = B_HEADS * B_DK
    lb = _hgrn_lower_bound(lb_ref[...], li)
    qf, k, g = _hgrn_gates(p_ref[:, :w], p_ref[:, w:2 * w], lb)
    v = p_ref[:, 2 * w:3 * w]
    eg = jnp.exp(g)
    qe_ref[...] = qf * eg
    eg_ref[...] = eg
    k_ref[...] = k
    qk = qf * k
    av_ref[...] = jnp.concatenate(
        [jnp.sum(qk[:, h * LANES:(h + 1) * LANES], axis=-1, keepdims=True) * v[:, h * LANES:(h + 1) * LANES]
         for h in range(B_HEADS)], axis=1)


def _hgrn_sample_state_kernel(st_ref, qe_ref, egt_ref, kt_ref, v_ref, av_ref, o_ref, ns_ref):
    for s in range(st_ref.shape[0]):
        for h in range(B_HEADS):
            st = st_ref[s, h]
            ns_ref[s, h] = egt_ref[s, :, h:h + 1] * st + kt_ref[s, :, h:h + 1] * v_ref[s, h:h + 1, :]
            qe = jnp.broadcast_to(qe_ref[s, h:h + 1, :], (2 * SUBLANES, B_DK)).astype(BF16)
            o_ref[s, h:h + 1, :] = _dot(qe, st.astype(BF16))[:1] + av_ref[s, h:h + 1, :]


def hgrn_sample(proj_s, lower_bounds, li, state):
    db = proj_s.shape[0]
    w = B_HEADS * B_DK
    whole = lambda a: pl.BlockSpec(a.shape, lambda i: (0,) * a.ndim)
    qe, eg, k, av = pl.pallas_call(
        functools.partial(_hgrn_sample_gates_kernel, li=li),
        grid=(1,),
        in_specs=[whole(proj_s), whole(lower_bounds)],
        out_specs=[pl.BlockSpec((db, w), lambda i: (0, 0))] * 4,
        out_shape=[jax.ShapeDtypeStruct((db, w), F32)] * 4,
        compiler_params=_cparams("arbitrary"),
        name="hgrn_sample_gates",
    )(proj_s, lower_bounds)
    heads = lambda a: a.reshape(db, B_HEADS, B_DK)
    cols = lambda a: heads(a).transpose(0, 2, 1)
    sb = SEQS_HGRN_SAMPLE
    hrow = pl.BlockSpec((sb, B_HEADS, B_DK), lambda b: (b, 0, 0))
    hcol = pl.BlockSpec((sb, B_DK, B_HEADS), lambda b: (b, 0, 0))
    stspec = pl.BlockSpec((sb, B_HEADS, B_DK, B_DV), lambda b: (b, 0, 0, 0))
    o, new_state = pl.pallas_call(
        _hgrn_sample_state_kernel,
        grid=(db // sb,),
        in_specs=[stspec, hrow, hcol, hcol, hrow, hrow],
        out_specs=[hrow, stspec],
        out_shape=[jax.ShapeDtypeStruct((db, B_HEADS, B_DV), F32),
                   jax.ShapeDtypeStruct(state.shape, F32)],
        compiler_params=_cparams("parallel"),
        name="hgrn_sample_state",
    )(state, heads(qe), cols(eg), cols(k), heads(proj_s[:, 2 * w:3 * w]), heads(av))
    return o.reshape(db, w), new_state


def _rglru_gates(y, wax_ref, ba, bx, lam):
    yb = y.astype(BF16)
    rs, is_ = [], []
    for n in range(C_HEADS):
        ax = _dot(yb[:, n * C_BW:(n + 1) * C_BW], wax_ref[n])
        rs.append(ax[:, :C_BW])
        is_.append(ax[:, C_BW:])
    r = _sigmoid(jnp.concatenate(rs, axis=1) + ba)
    i = _sigmoid(jnp.concatenate(is_, axis=1) + bx)
    softplus = jnp.maximum(-lam, 0.0) + jnp.log1p(jnp.exp(-jnp.abs(lam)))
    log_a = -C_GATE_C * r * softplus
    a = jnp.exp(log_a)
    th = jnp.tanh(log_a)
    u = jnp.sqrt(-2.0 * th / (1.0 - th)) * (i * y)
    return a, u


def _rglru_prompt_kernel(xb_ref, gate_ref, x_ref, wout_ref, cw_ref, cb_ref, wax_ref, ba_ref, bx_ref, lam_ref,
                         y_ref, tail_ref, hl_ref, prev_scr, h_scr, a_scr, u_scr, hs_scr):
    t = pl.program_id(1)
    tb = xb_ref.shape[0]

    @pl.when(t == 0)
    def _():
        prev_scr[...] = jnp.zeros(prev_scr.shape, F32)
        h_scr[...] = jnp.zeros(h_scr.shape, F32)

    xb = xb_ref[...]
    prev = prev_scr[...]
    row = lax.broadcasted_iota(jnp.int32, (tb, 1), 0)
    y = cb_ref[...] + cw_ref[C_CONV - 1:C_CONV, :] * xb
    for d in range(1, C_CONV):
        sh = pltpu.roll(xb, d, 0)
        for r in range(d):
            sh = jnp.where(row == r, prev[SUBLANES - d + r:SUBLANES - d + r + 1, :], sh)
        y = y + cw_ref[C_CONV - 1 - d:C_CONV - d, :] * sh
    prev_scr[...] = xb[tb - SUBLANES:, :]
    a, u = _rglru_gates(y, wax_ref, ba_ref[...], bx_ref[...], lam_ref[...])
    a_scr[...] = a
    u_scr[...] = u

    def body(i, h):
        base = pl.multiple_of(i * SUBLANES, SUBLANES)
        for r in range(SUBLANES):
            h = a_scr[pl.ds(base + r, 1), :] * h + u_scr[pl.ds(base + r, 1), :]
            hs_scr[pl.ds(base + r, 1), :] = h
        return h

    h = lax.fori_loop(0, tb // SUBLANES, body, h_scr[...])
    h_scr[...] = h
    gate = gate_ref[...]
    z = hs_scr[...] * (gate * _sigmoid(gate))
    y_ref[...] = x_ref[...] + _dot(z.astype(BF16), wout_ref[...])

    @pl.when(t == pl.num_programs(1) - 1)
    def _():
        tail_ref[...] = xb[tb - SUBLANES:, :]
        hl_ref[...] = h


def _rglru_weights(conv_w, conv_b, w_a, b_a, w_x, b_x, lam):
    r1 = lambda a: a.reshape(1, -1)
    return {"cw": conv_w, "cb": r1(conv_b), "wax": jnp.concatenate([w_a, w_x], axis=2).astype(BF16),
            "ba": r1(b_a), "bx": r1(b_x), "lam": r1(lam)}


def rglru_prompt(proj, x, w_out_bf16, wts, batch, seq, tblk):
    nt = seq // tblk
    w = C_WIDTH
    d = x.shape[1]
    full = lambda a: pl.BlockSpec(a.shape, lambda b, t: (0,) * a.ndim)
    rows = lambda n, col: pl.BlockSpec((tblk, n), lambda b, t: (b * nt + t, col))
    names = ("cw", "cb", "wax", "ba", "bx", "lam")
    return pl.pallas_call(
        _rglru_prompt_kernel,
        grid=(batch, nt),
        in_specs=[rows(w, 0), rows(w, 1), rows(d, 0), full(w_out_bf16)] + [full(wts[n]) for n in names],
        out_specs=[rows(d, 0),
                   pl.BlockSpec((None, SUBLANES, w), lambda b, t: (b, 0, 0)),
                   pl.BlockSpec((None, 1, w), lambda b, t: (b, 0, 0))],
        out_shape=[jax.ShapeDtypeStruct((batch * seq, d), F32),
                   jax.ShapeDtypeStruct((batch, SUBLANES, w), F32),
                   jax.ShapeDtypeStruct((batch, 1, w), F32)],
        scratch_shapes=[pltpu.VMEM((SUBLANES, w), F32), pltpu.VMEM((1, w), F32),
                        pltpu.VMEM((tblk, w), F32), pltpu.VMEM((tblk, w), F32), pltpu.VMEM((tblk, w), F32)],
        compiler_params=_cparams("parallel", "arbitrary"),
        name="rglru_prompt",
    )(proj, proj, x, w_out_bf16, *[wts[n] for n in names])


def _rglru_sample_kernel(xb_ref, buf_ref, h0_ref, cw_ref, cb_ref, wax_ref, ba_ref, bx_ref, lam_ref,
                         hs_ref, nbuf_ref):
    xb = xb_ref[...]
    y = cb_ref[...] + cw_ref[C_CONV - 1:C_CONV, :] * xb
    for j in range(C_CONV - 1):
        y = y + cw_ref[j:j + 1, :] * buf_ref[j]
    a, u = _rglru_gates(y, wax_ref, ba_ref[...], bx_ref[...], lam_ref[...])
    hs_ref[...] = a * h0_ref[...] + u
    for j in range(C_CONV - 2):
        nbuf_ref[j] = buf_ref[j + 1]
    nbuf_ref[C_CONV - 2] = xb


def rglru_sample(proj_s, buf_t, h0, wts):
    db = proj_s.shape[0]
    w = C_WIDTH
    whole = lambda a: pl.BlockSpec(a.shape, lambda i: (0,) * a.ndim)
    names = ("cw", "cb", "wax", "ba", "bx", "lam")
    return pl.pallas_call(
        _rglru_sample_kernel,
        grid=(1,),
        in_specs=[pl.BlockSpec((db, w), lambda i: (0, 0)), whole(buf_t), whole(h0)] + [whole(wts[n]) for n in names],
        out_specs=[pl.BlockSpec((db, w), lambda i: (0, 0)), whole(buf_t)],
        out_shape=[jax.ShapeDtypeStruct((db, w), F32), jax.ShapeDtypeStruct(buf_t.shape, F32)],
        compiler_params=_cparams("arbitrary"),
        name="rglru_sample",
    )(proj_s, buf_t, h0, *[wts[n] for n in names])


TM_PROMPT = 512
TQ_FLASH = 512
FLASH_LOOKAHEAD = 2
T_HGRN = 256
T_RGLRU = 256
PAGES_PER_BLOCK = 16
SEQS_HGRN_SAMPLE = 4

def kernel(x_prompt, x_sample, cache_ckv, cache_kpe, page_table, state_hgrn, state_conv, state_lru, norm_g, mla_w_in, mla_g_q_lora, mla_g_kv, mla_w_uq, mla_w_ukv, mla_g_q, mla_g_k, mla_w_out, hgrn_w_in, hgrn_lower_bounds, hgrn_g_norm, hgrn_w_out, rglru_w_in, rglru_conv_w, rglru_conv_b, rglru_w_a, rglru_b_a, rglru_w_x, rglru_b_x, rglru_L, rglru_w_out):
    bsz, s_p, d = x_prompt.shape
    db, s_s, _ = x_sample.shape
    assert s_s == 1 and d == D_MODEL
    depth = norm_g.shape[0]
    past_len = page_table.shape[1] * PAGE_SIZE
    xp = x_prompt.reshape(bsz * s_p, d)
    xs = x_sample.reshape(db, d)
    tabs_p = _rope_tables(jnp.arange(s_p))
    tabs_s = _rope_tables(jnp.full((db,), past_len))
    cache_kpe_t = cache_kpe.transpose(0, 1, 3, 2)

    ckv_p, kpe_p, ckv_s, kpe_s = [], [], [], []
    hg_p, hg_s, cv_p, cv_s, lr_p, lr_s = [], [], [], [], [], []
    for li in range(depth):
        kind, j = li % N_MIXERS, li // N_MIXERS
        if kind == 0:
            wts = _mla_weights(mla_w_in[j], mla_g_q_lora[j], mla_g_kv[j], mla_w_uq[j], mla_w_ukv[j],
                               mla_g_q[j], mla_g_k[j])
            w_out = mla_w_out[j].astype(BF16)
            q, k, vt, gate, ckv, kpe_blk = mla_proj(xp, norm_g[li], wts, tabs_p, s_p, TM_PROMPT, BF16, "g_q_base2")
            xp = flash_prompt(q, k, vt, gate, xp, w_out, bsz, s_p, TQ_FLASH)
            ckv_p.append(ckv.reshape(bsz, s_p, A_KV_LORA))
            kpe_p.append(kpe_blk[:, A_NOPE:A_QK].reshape(bsz, s_p, A_ROPE))

            q_s, _, _, gate_s, ckv_n, kpe_blk_s = mla_proj(xs, norm_g[li], wts, tabs_s, db, db, F32, "g_q")
            kpe_n = kpe_blk_s[:, A_NOPE:A_QK]
            o_s = mla_sample_attention(q_s, ckv_n, kpe_n, cache_ckv, cache_kpe_t, j, page_table, wts,
                                       PAGES_PER_BLOCK)
            xs = gated_out(o_s, gate_s, 0, w_out, xs, db)
            ckv_s.append(ckv_n.reshape(db, 1, A_KV_LORA))
            kpe_s.append(kpe_n.reshape(db, 1, A_ROPE))
        elif kind == 1:
            w_in = hgrn_w_in[j].astype(BF16)
            w_out = hgrn_w_out[j].astype(BF16)
            proj = norm_matmul(xp, norm_g[li], w_in, TM_PROMPT // 2)
            xp, st = hgrn_prompt(proj, xp, w_out, hgrn_g_norm[j], hgrn_lower_bounds, li, bsz, s_p, T_HGRN)
            hg_p.append(st)

            proj_s = norm_matmul(xs, norm_g[li], w_in, db)
            o_s, st_s = hgrn_sample(proj_s, hgrn_lower_bounds, li, state_hgrn[j])
            xs = gated_out(o_s, proj_s, 3, w_out, xs, db, hgrn_g_norm[j])
            hg_s.append(st_s)
        else:
            w_in = rglru_w_in[j].astype(BF16)
            w_out = rglru_w_out[j].astype(BF16)
            wts = _rglru_weights(rglru_conv_w[j], rglru_conv_b[j], rglru_w_a[j], rglru_b_a[j],
                                 rglru_w_x[j], rglru_b_x[j], rglru_L[j])
            proj = norm_matmul(xp, norm_g[li], w_in, TM_PROMPT)
            xp, tail, hl = rglru_prompt(proj, xp, w_out, wts, bsz, s_p, T_RGLRU)
            cv_p.append(tail[:, SUBLANES - (C_CONV - 1):, :])
            lr_p.append(hl.reshape(bsz, C_WIDTH))

            proj_s = norm_matmul(xs, norm_g[li], w_in, db)
            hs_s, nbuf = rglru_sample(proj_s, state_conv[j].transpose(1, 0, 2), state_lru[j], wts)
            xs = gated_out(hs_s, proj_s, 1, w_out, xs, db)
            cv_s.append(nbuf.transpose(1, 0, 2))
            lr_s.append(hs_s)

    return (xp.reshape(bsz, s_p, d), xs.reshape(db, 1, d),
            jnp.stack(ckv_p), jnp.stack(kpe_p), jnp.stack(ckv_s), jnp.stack(kpe_s),
            jnp.stack(hg_p), jnp.stack(hg_s), jnp.stack(cv_p), jnp.stack(cv_s),
            jnp.stack(lr_p), jnp.stack(lr_s))
```

```python
import functools
import math

import jax
import jax.numpy as jnp
from jax import lax
from jax.experimental import pallas as pl
from jax.experimental.pallas import tpu as pltpu

F32 = jnp.float32
BF16 = jnp.bfloat16

LANES = 128
SUBLANES = 8
VMEM_LIMIT_BYTES = 56 * 1024 * 1024

D_MODEL = 1024
PAGE_SIZE = 128
N_MIXERS = 3
EPS = 1e-6
ROPE_THETA = 10000.0
NEG_BIG = -1e30

A_HEADS = 8
A_NOPE = 64
A_ROPE = 32
A_QK = A_NOPE + A_ROPE
A_V = 64
A_Q_LORA = 384
A_KV_LORA = 256
A_WIDTH = A_HEADS * A_V
A_HALF = A_ROPE // 2
A_PADW = A_HEADS * LANES
A_VL = A_V + 2 * SUBLANES

B_HEADS = 8
B_DK = 128
B_DV = 128
B_CHUNK = 64

C_WIDTH = 1024
C_HEADS = 8
C_BW = 128
C_CONV = 4
C_GATE_C = 8.0


def _cparams(*sem):
    return pltpu.CompilerParams(dimension_semantics=sem, vmem_limit_bytes=VMEM_LIMIT_BYTES)


def _sigmoid(x):
    return 1.0 / (1.0 + jnp.exp(-x))


def _rms_rows(x, g):
    ms = jnp.mean(x * x, axis=-1, keepdims=True)
    return x * lax.rsqrt(ms + EPS) * g


def _dot_nt(a, b):
    return lax.dot_general(a, b, (((1,), (1,)), ((), ())), preferred_element_type=F32)


def _dot(a, b):
    return jnp.dot(a, b, preferred_element_type=F32)


def _norm_matmul_kernel(x_ref, g_ref, w_ref, o_ref):
    h = _rms_rows(x_ref[...], g_ref[...])
    o_ref[...] = _dot(h.astype(BF16), w_ref[...])


def norm_matmul(x, g, w_bf16, tm):
    m, d = x.shape
    n = w_bf16.shape[1]
    return pl.pallas_call(
        _norm_matmul_kernel,
        grid=(m // tm,),
        in_specs=[pl.BlockSpec((tm, d), lambda i: (i, 0)),
                  pl.BlockSpec((1, d), lambda i: (0, 0)),
                  pl.BlockSpec((d, n), lambda i: (0, 0))],
        out_specs=pl.BlockSpec((tm, n), lambda i: (i, 0)),
        out_shape=jax.ShapeDtypeStruct((m, n), F32),
        compiler_params=_cparams("parallel"),
        name="norm_matmul",
    )(x, g.reshape(1, d), w_bf16)


def _gated_out_kernel(o_ref, gate_ref, w_ref, x_ref, gn_ref, y_ref, *, head_norm):
    o = o_ref[...]
    if head_norm:
        parts = []
        for h in range(o.shape[1] // LANES):
            parts.append(_rms_rows(o[:, h * LANES:(h + 1) * LANES], gn_ref[...]))
        o = jnp.concatenate(parts, axis=1)
    gate = gate_ref[...]
    z = o * (gate * _sigmoid(gate))
    y_ref[...] = x_ref[...] + _dot(z.astype(BF16), w_ref[...])


def gated_out(o, gate_arr, gate_col, w_bf16, x, tm, g_norm=None):
    m, w = o.shape
    d = x.shape[1]
    head_norm = g_norm is not None
    gn = (g_norm if head_norm else jnp.ones((LANES,), F32)).reshape(1, LANES)
    return pl.pallas_call(
        functools.partial(_gated_out_kernel, head_norm=head_norm),
        grid=(m // tm,),
        in_specs=[pl.BlockSpec((tm, w), lambda i: (i, 0)),
                  pl.BlockSpec((tm, w), lambda i: (i, gate_col)),
                  pl.BlockSpec((w, d), lambda i: (0, 0)),
                  pl.BlockSpec((tm, d), lambda i: (i, 0)),
                  pl.BlockSpec((1, LANES), lambda i: (0, 0))],
        out_specs=pl.BlockSpec((tm, d), lambda i: (i, 0)),
        out_shape=jax.ShapeDtypeStruct((m, d), F32),
        compiler_params=_cparams("parallel"),
        name="gated_out",
    )(o, gate_arr, w_bf16, x, gn)


def _rope_lanes(x, c, s1, s2):
    n = x.shape[1]
    return x * c + pltpu.roll(x, n - A_HALF, 1) * s1 + pltpu.roll(x, A_HALF, 1) * s2


def _head_norm_blocks(x, g):
    parts = []
    for h in range(A_HEADS):
        blk = x[:, h * LANES:(h + 1) * LANES]
        ss = jnp.sum(blk * blk, axis=-1, keepdims=True)
        parts.append(blk * lax.rsqrt(ss * (1.0 / A_QK) + EPS) * g[:, h * LANES:(h + 1) * LANES])
    return jnp.concatenate(parts, axis=1)


def _mla_proj_kernel(x_ref, gn_ref, win_ref, gql_ref, gkv_ref, wuq_ref, gq_ref, wuk_ref, wuvt_ref, gk_ref,
                     c_ref, s1_ref, s2_ref,
                     q_ref, k_ref, vt_ref, gate_ref, ckv_ref, kpe_ref):
    h = _rms_rows(x_ref[...], gn_ref[...])
    p = _dot(h.astype(BF16), win_ref[...])
    gate_ref[...] = p[:, :A_WIDTH]
    cq = p[:, A_WIDTH:A_WIDTH + A_Q_LORA]
    ckv = p[:, A_WIDTH + A_Q_LORA:A_WIDTH + A_Q_LORA + A_KV_LORA]
    kpe = p[:, A_WIDTH + A_Q_LORA + A_KV_LORA:]
    c, s1, s2 = c_ref[...], s1_ref[...], s2_ref[...]
    c8 = jnp.concatenate([c] * A_HEADS, axis=1)
    sin8 = jnp.concatenate([s2 - s1] * A_HEADS, axis=1)

    q2 = _dot(_rms_rows(cq, gql_ref[...]).astype(BF16), wuq_ref[...])
    q = q2[:, :A_PADW] * c8 + q2[:, A_PADW:] * sin8
    q_ref[...] = _head_norm_blocks(q, gq_ref[...]).astype(q_ref.dtype)

    ckvn = _rms_rows(ckv, gkv_ref[...])
    ckv_ref[...] = ckvn
    kper = _rope_lanes(kpe, c, s1, s2)
    kpe_ref[...] = kper
    cb = ckvn.astype(BF16)
    kn = _dot(cb, wuk_ref[...]) + jnp.concatenate([kper] * A_HEADS, axis=1)
    k_ref[...] = _head_norm_blocks(kn, gk_ref[...]).astype(k_ref.dtype)
    vt = _dot_nt(wuvt_ref[...], cb)
    ones = jnp.ones((A_VL - A_V, vt.shape[1]), F32)
    vt_ref[...] = jnp.concatenate(
        [blk for h in range(A_HEADS) for blk in (vt[h * A_V:(h + 1) * A_V, :], ones)], axis=0).astype(vt_ref.dtype)


def mla_proj(x, g_norm, wts, rope_tabs, rows_per_seq, tm, qkv_dtype, q_gain):
    m, d = x.shape
    c, s1, s2 = rope_tabs
    nblk = rows_per_seq // tm
    full = lambda a: pl.BlockSpec(a.shape, lambda i: (0, 0))
    tab = pl.BlockSpec((tm, LANES), lambda i: (i % nblk, 0))
    row = lambda n: pl.BlockSpec((tm, n), lambda i: (i, 0))
    return pl.pallas_call(
        _mla_proj_kernel,
        grid=(m // tm,),
        in_specs=[row(d), pl.BlockSpec((1, d), lambda i: (0, 0)), full(wts["w_in"]),
                  full(wts["g_ql"]), full(wts["g_kv"]), full(wts["w_uq"]), full(wts[q_gain]),
                  full(wts["w_uk"]), full(wts["w_uvt"]), full(wts["g_k"]), tab, tab, tab],
        out_specs=[row(A_PADW), row(A_PADW), pl.BlockSpec((A_HEADS * A_VL, tm), lambda i: (0, i)),
                   row(A_WIDTH), row(A_KV_LORA), row(LANES)],
        out_shape=[jax.ShapeDtypeStruct((m, A_PADW), qkv_dtype),
                   jax.ShapeDtypeStruct((m, A_PADW), qkv_dtype),
                   jax.ShapeDtypeStruct((A_HEADS * A_VL, m), qkv_dtype),
                   jax.ShapeDtypeStruct((m, A_WIDTH), F32),
                   jax.ShapeDtypeStruct((m, A_KV_LORA), F32),
                   jax.ShapeDtypeStruct((m, LANES), F32)],
        compiler_params=_cparams("parallel"),
        name="mla_proj",
    )(x, g_norm.reshape(1, d), wts["w_in"], wts["g_ql"], wts["g_kv"], wts["w_uq"], wts[q_gain],
      wts["w_uk"], wts["w_uvt"], wts["g_k"], c, s1, s2)


def _pad_heads(a, used):
    pad = [(0, 0)] * (a.ndim - 1) + [(0, LANES - used)]
    a = jnp.pad(a, pad)
    return a.reshape(a.shape[:-2] + (a.shape[-2] * LANES,))


def _mla_weights(w_in, g_ql, g_kv, w_uq, w_ukv, g_q, g_k):
    cq, ckv, kpe, gate = jnp.split(w_in, [A_Q_LORA, A_Q_LORA + A_KV_LORA, A_Q_LORA + A_KV_LORA + A_ROPE], axis=1)
    kpe_blk = jnp.pad(kpe, ((0, 0), (A_NOPE, LANES - A_QK)))
    scale = A_QK ** -0.5
    ukv = w_ukv.reshape(A_KV_LORA, A_HEADS, A_NOPE + A_V)
    uk, uv = ukv[:, :, :A_NOPE], ukv[:, :, A_NOPE:]
    ukt_pad = _pad_heads(uk, A_NOPE).T
    ukt_hi = ukt_pad.astype(BF16)
    uq = w_uq.reshape(A_Q_LORA, A_HEADS, A_QK)
    x1, x2 = uq[:, :, A_NOPE:A_NOPE + A_HALF], uq[:, :, A_NOPE + A_HALF:]
    uq_rot = jnp.concatenate([jnp.zeros_like(uq[:, :, :A_NOPE]), -x2, x1], axis=2)
    return {
        "w_in": jnp.concatenate([gate, cq, ckv, kpe_blk], axis=1).astype(BF16),
        "g_ql": g_ql.reshape(1, -1), "g_kv": g_kv.reshape(1, -1),
        "w_uq": jnp.concatenate([_pad_heads(uq, A_QK), _pad_heads(uq_rot, A_QK)], axis=1).astype(BF16),
        "g_q": _pad_heads(jnp.broadcast_to(g_q * scale, (A_HEADS, A_QK)), A_QK).reshape(1, -1),
        "g_q_base2": _pad_heads(jnp.broadcast_to(g_q * (scale * math.log2(math.e)), (A_HEADS, A_QK)),
                                A_QK).reshape(1, -1),
        "g_k": _pad_heads(jnp.broadcast_to(g_k, (A_HEADS, A_QK)), A_QK).reshape(1, -1),
        "w_uk": _pad_heads(uk, A_NOPE).astype(BF16),
        "w_uv": uv.reshape(A_KV_LORA, A_WIDTH).astype(BF16),
        "w_uvt": uv.reshape(A_KV_LORA, A_WIDTH).T.astype(BF16),
        "ukt": uk.reshape(A_KV_LORA, A_HEADS * A_NOPE).T.astype(BF16),
        "ukt_hi": ukt_hi,
        "ukt_lo": (ukt_pad - ukt_hi.astype(F32)).astype(BF16),
    }


def _rope_tables(pos):
    inv = 1.0 / (ROPE_THETA ** (jnp.arange(0, A_ROPE, 2, dtype=F32) / A_ROPE))
    ang = pos.astype(F32)[:, None] * inv[None, :]
    cos, sin = jnp.cos(ang), jnp.sin(ang)
    n = pos.shape[0]
    z = lambda w: jnp.zeros((n, w), F32)
    c = jnp.concatenate([jnp.ones((n, A_NOPE), F32), cos, cos, z(LANES - A_QK)], axis=1)
    s1 = jnp.concatenate([z(A_NOPE), -sin, z(A_HALF), z(LANES - A_QK)], axis=1)
    s2 = jnp.concatenate([z(A_NOPE + A_HALF), sin, z(LANES - A_QK)], axis=1)
    return c, s1, s2


def _flash_kernel(qi_ref, kj_ref, q_ref, k_ref, vt_ref, gate_ref, x_ref, wout_ref, y_ref, m_scr, acc_scr):
    p_id = pl.program_id(1)
    qi = qi_ref[p_id]
    kj = kj_ref[p_id]
    tq, tk = q_ref.shape[0], k_ref.shape[0]

    @pl.when(kj == 0)
    def _():
        m_scr[...] = jnp.full(m_scr.shape, NEG_BIG, F32)
        acc_scr[...] = jnp.zeros(acc_scr.shape, F32)

    def step(masked):
        if masked:
            key = lax.broadcasted_iota(jnp.int32, (tk, tq), 0)
            qry = lax.broadcasted_iota(jnp.int32, (tk, tq), 1)
            keep = key <= qry

        def scores(h):
            return _dot_nt(k_ref[:, h * LANES:(h + 1) * LANES], q_ref[:, h * LANES:(h + 1) * LANES])

        ahead = [scores(h) for h in range(FLASH_LOOKAHEAD)]
        for h in range(A_HEADS):
            st = ahead.pop(0)
            if h + FLASH_LOOKAHEAD < A_HEADS:
                ahead.append(scores(h + FLASH_LOOKAHEAD))
            if masked:
                st = jnp.where(keep, st, NEG_BIG)
            m_prev = m_scr[h]
            m_new = jnp.maximum(m_prev, jnp.max(st, axis=0, keepdims=True))
            p = jnp.exp2(st - m_new)
            corr = jnp.exp2(m_prev - m_new)
            rows = slice(h * A_VL, (h + 1) * A_VL)
            acc_scr[rows, :] = acc_scr[rows, :] * corr + _dot(vt_ref[rows, :], p.astype(BF16))
            m_scr[h] = m_new

    @pl.when(kj < qi)
    def _():
        step(False)

    @pl.when(kj == qi)
    def _():
        step(True)
        ot = jnp.concatenate([acc_scr[h * A_VL:h * A_VL + A_V, :] / acc_scr[h * A_VL + A_V:h * A_VL + A_V + 1, :]
                              for h in range(A_HEADS)], axis=0)
        gate = gate_ref[...]
        z = ot.T * (gate * _sigmoid(gate))
        y_ref[...] = x_ref[...] + _dot(z.astype(BF16), wout_ref[...])


def flash_prompt(q, k, vt, gate, x, w_out_bf16, batch, seq, tq):
    nq = seq // tq
    d = x.shape[1]
    pairs = [(i, j) for i in range(nq) for j in range(i + 1)]
    qi = jnp.asarray([p[0] for p in pairs], jnp.int32)
    kj = jnp.asarray([p[1] for p in pairs], jnp.int32)
    qrow = lambda n: pl.BlockSpec((tq, n), lambda b, p, qi, kj: (b * nq + qi[p], 0))
    grid_spec = pltpu.PrefetchScalarGridSpec(
        num_scalar_prefetch=2,
        grid=(batch, len(pairs)),
        in_specs=[qrow(A_PADW),
                  pl.BlockSpec((tq, A_PADW), lambda b, p, qi, kj: (b * nq + kj[p], 0)),
                  pl.BlockSpec((A_HEADS * A_VL, tq), lambda b, p, qi, kj: (0, b * nq + kj[p])),
                  qrow(A_WIDTH), qrow(d), pl.BlockSpec(w_out_bf16.shape, lambda b, p, qi, kj: (0, 0))],
        out_specs=qrow(d),
        scratch_shapes=[pltpu.VMEM((A_HEADS, 1, tq), F32), pltpu.VMEM((A_HEADS * A_VL, tq), F32)])
    return pl.pallas_call(
        _flash_kernel, grid_spec=grid_spec,
        out_shape=jax.ShapeDtypeStruct((batch * seq, d), F32),
        compiler_params=_cparams("parallel", "arbitrary"),
        name="flash_prompt",
    )(qi, kj, q, k, vt, gate, x, w_out_bf16)


def _absorb_query_kernel(q_ref, gk_ref, ukhi_ref, uklo_ref, qg_ref, qabs_ref):
    qg = q_ref[...] * gk_ref[...]
    qg_ref[...] = qg
    for h in range(A_HEADS):
        blk = qg[:, h * LANES:(h + 1) * LANES]
        q_hi = blk.astype(BF16)
        q_lo = (blk - q_hi.astype(F32)).astype(BF16)
        w_hi = ukhi_ref[h * LANES:(h + 1) * LANES, :]
        qabs_ref[h] = _dot(q_hi, w_hi) + _dot(q_hi, uklo_ref[h * LANES:(h + 1) * LANES, :]) + _dot(q_lo, w_hi)


def absorb_query(q_s, wts):
    db = q_s.shape[0]
    whole = lambda a: pl.BlockSpec(a.shape, lambda i: (0,) * a.ndim)
    return pl.pallas_call(
        _absorb_query_kernel,
        grid=(1,),
        in_specs=[whole(q_s), whole(wts["g_k"]), whole(wts["ukt_hi"]), whole(wts["ukt_lo"])],
        out_specs=[pl.BlockSpec((db, A_PADW), lambda i: (0, 0)),
                   pl.BlockSpec((A_HEADS, db, A_KV_LORA), lambda i: (0, 0, 0))],
        out_shape=[jax.ShapeDtypeStruct((db, A_PADW), F32),
                   jax.ShapeDtypeStruct((A_HEADS, db, A_KV_LORA), F32)],
        compiler_params=_cparams("arbitrary"),
        name="absorb_query",
    )(q_s, wts["g_k"], wts["ukt_hi"], wts["ukt_lo"])


def _sample_attn_kernel(pt_ref, qabs_ref, qpe_ref, ukt_ref, wuv_ref, cnew_ref, knew_ref, ckv_hbm, kpe_hbm, o_ref,
                        lhs_scr, cb_scr, s_scr, ckv_buf, kpe_buf, sems, *, la, n_pages, pages_per_block):
    b = pl.program_id(0)
    last = pl.num_programs(0) - 1
    slot = b % 2
    nxt = jnp.minimum(b + 1, last)
    n_nope = A_HEADS * A_NOPE
    n_past = n_pages * PAGE_SIZE

    def page_copies(seq, to_slot, i):
        page = pt_ref[seq * n_pages + i]
        rows = pl.ds(i * PAGE_SIZE, PAGE_SIZE)
        return (pltpu.make_async_copy(ckv_hbm.at[la, page], ckv_buf.at[to_slot, rows, :], sems.at[0, to_slot]),
                pltpu.make_async_copy(kpe_hbm.at[la, page], kpe_buf.at[to_slot, :, rows], sems.at[1, to_slot]))

    def start_pages(seq, to_slot, pages):
        for i in pages:
            for cp in page_copies(seq, to_slot, i):
                cp.start()

    def wait_pages(to_slot):
        pltpu.make_async_copy(ckv_buf.at[to_slot], ckv_buf.at[to_slot], sems.at[0, to_slot]).wait()
        pltpu.make_async_copy(kpe_buf.at[to_slot], kpe_buf.at[to_slot], sems.at[1, to_slot]).wait()

    @pl.when(b == 0)
    def _():
        start_pages(b, slot, range(n_pages))

    wait_pages(slot)
    lhs_scr[:n_nope, :] = ukt_ref[...]
    lhs_scr[n_nope:, :] = qabs_ref[...]

    def score(c, kpe_t, start):
        n = c.shape[0]
        cb = c.astype(BF16)
        cb_scr[start:start + n, :] = cb
        a = _dot_nt(lhs_scr[...], cb)
        kn = a[:n_nope]
        ss = jnp.sum((kn * kn).reshape(A_HEADS, A_NOPE, n), axis=1)
        s_pe = _dot(qpe_ref[...], kpe_t.astype(BF16))[:A_HEADS]
        pe_ss = jnp.sum(kpe_t * kpe_t, axis=0, keepdims=True)
        rs = lax.rsqrt((ss + pe_ss) * (1.0 / A_QK) + EPS)
        return (a[n_nope:n_nope + A_HEADS] + s_pe) * rs

    start_pages(nxt, 1 - slot, range(n_pages // 2))
    for blk in range(0, n_pages, pages_per_block):
        start, n = blk * PAGE_SIZE, pages_per_block * PAGE_SIZE
        s_scr[:, start:start + n] = score(ckv_buf[slot, start:start + n, :], kpe_buf[slot, :, start:start + n], start)
        if blk == 0:
            start_pages(nxt, 1 - slot, range(n_pages // 2, n_pages))
    s_new = score(jnp.broadcast_to(cnew_ref[...], (LANES, A_KV_LORA)),
                  jnp.broadcast_to(knew_ref[...], (A_ROPE, LANES)), n_past)
    s_scr[:, n_past:] = jnp.where(lax.broadcasted_iota(jnp.int32, s_new.shape, 1) < 1, s_new, NEG_BIG)

    s = s_scr[...]
    p = jnp.exp(s - jnp.max(s, axis=-1, keepdims=True))
    pb = p.astype(BF16)
    acc = jnp.concatenate([_dot(pb, cb_scr[:, :LANES]), _dot(pb, cb_scr[:, LANES:])], axis=1)
    lat = acc / jnp.sum(p, axis=-1, keepdims=True)
    full = _dot(lat.astype(BF16), wuv_ref[...])
    lane = lax.broadcasted_iota(jnp.int32, full.shape, 1)
    row = lax.broadcasted_iota(jnp.int32, full.shape, 0)
    sel = (lane >= row * A_V) & (lane < (row + 1) * A_V)
    o_ref[...] = jnp.sum(jnp.where(sel, full, 0.0), axis=0, keepdims=True)

    @pl.when(b == last)
    def _():
        wait_pages(1 - slot)


def mla_sample_attention(q_s, ckv_new, kpe_new, cache_ckv, cache_kpe_t, la, page_table, wts, pages_per_block):
    db, n_pages = page_table.shape
    assert n_pages % pages_per_block == 0
    n_past = n_pages * PAGE_SIZE
    n_tok = n_past + LANES
    row3 = lambda r, n: pl.BlockSpec((None, r, n), lambda b, pt: (b, 0, 0))
    full = lambda a: pl.BlockSpec(a.shape, lambda b, pt: (0, 0))
    hbm = pl.BlockSpec(memory_space=pl.ANY)
    q_rows = 2 * SUBLANES
    lhs_rows = A_HEADS * A_NOPE + q_rows
    qg, qabs = absorb_query(q_s, wts)
    pad_rows = lambda a: jnp.pad(a, ((0, 0), (0, q_rows - A_HEADS), (0, 0))).astype(BF16)
    qabs = pad_rows(qabs.transpose(1, 0, 2))
    qpe = pad_rows(qg.reshape(db, A_HEADS, LANES)[:, :, A_NOPE:A_QK])
    grid_spec = pltpu.PrefetchScalarGridSpec(
        num_scalar_prefetch=1,
        grid=(db,),
        in_specs=[row3(q_rows, A_KV_LORA), row3(q_rows, A_ROPE), full(wts["ukt"]),
                  full(wts["w_uv"]), row3(1, A_KV_LORA), row3(A_ROPE, 1), hbm, hbm],
        out_specs=pl.BlockSpec((None, 1, A_WIDTH), lambda b, pt: (b, 0, 0)),
        scratch_shapes=[pltpu.VMEM((lhs_rows, A_KV_LORA), BF16),
                        pltpu.VMEM((n_tok, A_KV_LORA), BF16), pltpu.VMEM((A_HEADS, n_tok), F32),
                        pltpu.VMEM((2, n_past, A_KV_LORA), F32), pltpu.VMEM((2, A_ROPE, n_past), F32),
                        pltpu.SemaphoreType.DMA((2, 2))])
    out = pl.pallas_call(
        functools.partial(_sample_attn_kernel, la=la, n_pages=n_pages, pages_per_block=pages_per_block),
        grid_spec=grid_spec,
        out_shape=jax.ShapeDtypeStruct((db, 1, A_WIDTH), F32),
        compiler_params=_cparams("arbitrary"),
        name="mla_sample_attention",
    )(page_table.reshape(-1), qabs, qpe, wts["ukt"],
      wts["w_uv"], ckv_new.reshape(db, 1, A_KV_LORA), kpe_new.reshape(db, A_ROPE, 1), cache_ckv, cache_kpe_t)
    return out.reshape(db, A_WIDTH)


def _hgrn_lower_bound(lb_all, li):
    e = jnp.exp(lb_all - jnp.max(lb_all, axis=0, keepdims=True))
    smx = e / jnp.sum(e, axis=0, keepdims=True)
    return jnp.sum(smx[1:li + 1], axis=0, keepdims=True) if li > 0 else jnp.zeros_like(smx[:1])


def _hgrn_gates(q, f, lb):
    g = jnp.log(lb + (1.0 - lb) * _sigmoid(f))
    k = (1.0 - lb) * _sigmoid(-f)
    qf = q * _sigmoid(q) * (B_DK ** -0.5)
    return qf, k, g


def _cumsum_rows(x):
    n = x.shape[0]
    row = lax.broadcasted_iota(jnp.int32, x.shape, 0)
    d = 1
    while d < n:
        x = x + jnp.where(row >= d, pltpu.roll(x, d, 0), 0.0)
        d *= 2
    return x


def _bcast_row_in_groups(x, j):
    n, w = x.shape
    x3 = x.reshape(n // SUBLANES, SUBLANES, w)
    return jnp.broadcast_to(x3[:, j:j + 1, :], x3.shape).reshape(n, w)


def _intra_chunk_masks(c):
    tt = lax.broadcasted_iota(jnp.int32, (c, c), 0)
    ss = lax.broadcasted_iota(jnp.int32, (c, c), 1)
    masks = {"diag": tt == ss}
    half = c // 2
    while half >= 1:
        blk = 2 * half
        keep = ((tt % blk) >= half) & ((ss % blk) < half)
        masks[half] = keep & ((tt // blk) == (ss // blk)) if blk < c else keep
        half //= 2
    return masks


def _intra_chunk_att(qf, k, b, masks):
    c = qf.shape[0]
    row = lax.broadcasted_iota(jnp.int32, (c, 1), 0)
    att = jnp.where(masks["diag"], jnp.sum(qf * k, axis=-1, keepdims=True), 0.0)
    half = c // 2
    while half >= 1:
        blk = 2 * half
        if blk >= SUBLANES:
            bref = jnp.concatenate(
                [jnp.broadcast_to(b[m:m + 1], (blk, b.shape[1])) for m in range(half, c, blk)], axis=0)
        else:
            rm = row % SUBLANES
            bref = _bcast_row_in_groups(b, SUBLANES - half)
            for m in range(SUBLANES - half - blk, 0, -blk):
                bref = jnp.where(rm < m + half, _bcast_row_in_groups(b, m), bref)
        e = jnp.exp(-jnp.abs(b - bref))
        p = _dot_nt((qf * e).astype(BF16), (k * e).astype(BF16))
        att = att + jnp.where(masks[half], p, 0.0)
        half //= 2
    return att


def _hgrn_prompt_kernel(q_ref, f_ref, v_ref, gate_ref, x_ref, wout_ref, gn_ref, lb_ref, y_ref, st_ref,
                        state_scr, o_scr, *, li):
    t = pl.program_id(1)

    @pl.when(t == 0)
    def _():
        state_scr[...] = jnp.zeros(state_scr.shape, F32)

    lb_all = _hgrn_lower_bound(lb_ref[...], li)
    masks = _intra_chunk_masks(B_CHUNK)
    for ci in range(q_ref.shape[0] // B_CHUNK):
        sl = slice(ci * B_CHUNK, (ci + 1) * B_CHUNK)
        for h in range(B_HEADS):
            hl = slice(h * LANES, (h + 1) * LANES)
            qf, k, g = _hgrn_gates(q_ref[sl, hl], f_ref[sl, hl], lb_all[:, hl])
            b = _cumsum_rows(g)
            vb = v_ref[sl, hl].astype(BF16)
            intra = _dot(_intra_chunk_att(qf, k, b, masks).astype(BF16), vb)
            blast = b[B_CHUNK - 1:B_CHUNK]
            kd = (k * jnp.exp(blast - b)).astype(BF16)
            upd = lax.dot_general(vb, kd, (((0,), (0,)), ((), ())), preferred_element_type=F32)
            st = state_scr[h]
            o_scr[sl, hl] = intra + _dot_nt((qf * jnp.exp(b)).astype(BF16), st.astype(BF16))
            state_scr[h] = jnp.exp(blast) * st + upd

    o = jnp.concatenate([_rms_rows(o_scr[:, h * LANES:(h + 1) * LANES], gn_ref[...]) for h in range(B_HEADS)],
                        axis=1)
    gate = gate_ref[...]
    z = o * (gate * _sigmoid(gate))
    y_ref[...] = x_ref[...] + _dot(z.astype(BF16), wout_ref[...])

    @pl.when(t == pl.num_programs(1) - 1)
    def _():
        for h in range(B_HEADS):
            st_ref[h] = state_scr[h].T


def hgrn_prompt(proj, x, w_out_bf16, g_norm, lower_bounds, li, batch, seq, tblk):
    nt = seq // tblk
    w = B_HEADS * B_DK
    d = x.shape[1]
    rows = lambda n, col: pl.BlockSpec((tblk, n), lambda b, t: (b * nt + t, col))
    full = lambda a: pl.BlockSpec(a.shape, lambda b, t: (0,) * a.ndim)
    gn = g_norm.reshape(1, B_DV)
    return pl.pallas_call(
        functools.partial(_hgrn_prompt_kernel, li=li),
        grid=(batch, nt),
        in_specs=[rows(w, 0), rows(w, 1), rows(w, 2), rows(w, 3), rows(d, 0), full(w_out_bf16), full(gn),
                  full(lower_bounds)],
        out_specs=[rows(d, 0), pl.BlockSpec((None, B_HEADS, B_DK, B_DV), lambda b, t: (b, 0, 0, 0))],
        out_shape=[jax.ShapeDtypeStruct((batch * seq, d), F32),
                   jax.ShapeDtypeStruct((batch, B_HEADS, B_DK, B_DV), F32)],
        scratch_shapes=[pltpu.VMEM((B_HEADS, B_DV, B_DK), F32), pltpu.VMEM((tblk, w), F32)],
        compiler_params=_cparams("parallel", "arbitrary"),
        name="hgrn_prompt",
    )(proj, proj, proj, proj, x, w_out_bf16, gn, lower_bounds)


def _hgrn_sample_gates_kernel(p_ref, lb_ref, qe_ref, eg_ref, k_ref, av_ref, *, li):
    w = B_HEADS * B_DK
    lb = _hgrn_lower_bound(lb_ref[...], li)
    qf, k, g = _hgrn_gates(p_ref[:, :w], p_ref[:, w:2 * w], lb)
    v = p_ref[:, 2 * w:3 * w]
    eg = jnp.exp(g)
    qe_ref[...] = qf * eg
    eg_ref[...] = eg
    k_ref[...] = k
    qk = qf * k
    av_ref[...] = jnp.concatenate(
        [jnp.sum(qk[:, h * LANES:(h + 1) * LANES], axis=-1, keepdims=True) * v[:, h * LANES:(h + 1) * LANES]
         for h in range(B_HEADS)], axis=1)


def _hgrn_sample_state_kernel(st_ref, qe_ref, egt_ref, kt_ref, v_ref, av_ref, o_ref, ns_ref):
    for s in range(st_ref.shape[0]):
        for h in range(B_HEADS):
            st = st_ref[s, h]
            ns_ref[s, h] = egt_ref[s, :, h:h + 1] * st + kt_ref[s, :, h:h + 1] * v_ref[s, h:h + 1, :]
            qe = jnp.broadcast_to(qe_ref[s, h:h + 1, :], (2 * SUBLANES, B_DK)).astype(BF16)
            o_ref[s, h:h + 1, :] = _dot(qe, st.astype(BF16))[:1] + av_ref[s, h:h + 1, :]


def hgrn_sample(proj_s, lower_bounds, li, state):
    db = proj_s.shape[0]
    w = B_HEADS * B_DK
    whole = lambda a: pl.BlockSpec(a.shape, lambda i: (0,) * a.ndim)
    qe, eg, k, av = pl.pallas_call(
        functools.partial(_hgrn_sample_gates_kernel, li=li),
        grid=(1,),
        in_specs=[whole(proj_s), whole(lower_bounds)],
        out_specs=[pl.BlockSpec((db, w), lambda i: (0, 0))] * 4,
        out_shape=[jax.ShapeDtypeStruct((db, w), F32)] * 4,
        compiler_params=_cparams("arbitrary"),
        name="hgrn_sample_gates",
    )(proj_s, lower_bounds)
    heads = lambda a: a.reshape(db, B_HEADS, B_DK)
    cols = lambda a: heads(a).transpose(0, 2, 1)
    sb = SEQS_HGRN_SAMPLE
    hrow = pl.BlockSpec((sb, B_HEADS, B_DK), lambda b: (b, 0, 0))
    hcol = pl.BlockSpec((sb, B_DK, B_HEADS), lambda b: (b, 0, 0))
    stspec = pl.BlockSpec((sb, B_HEADS, B_DK, B_DV), lambda b: (b, 0, 0, 0))
    o, new_state = pl.pallas_call(
        _hgrn_sample_state_kernel,
        grid=(db // sb,),
        in_specs=[stspec, hrow, hcol, hcol, hrow, hrow],
        out_specs=[hrow, stspec],
        out_shape=[jax.ShapeDtypeStruct((db, B_HEADS, B_DV), F32),
                   jax.ShapeDtypeStruct(state.shape, F32)],
        compiler_params=_cparams("parallel"),
        name="hgrn_sample_state",
    )(state, heads(qe), cols(eg), cols(k), heads(proj_s[:, 2 * w:3 * w]), heads(av))
    return o.reshape(db, w), new_state


def _rglru_gates(y, wax_ref, ba, bx, lam):
    yb = y.astype(BF16)
    rs, is_ = [], []
    for n in range(C_HEADS):
        ax = _dot(yb[:, n * C_BW:(n + 1) * C_BW], wax_ref[n])
        rs.append(ax[:, :C_BW])
        is_.append(ax[:, C_BW:])
    r = _sigmoid(jnp.concatenate(rs, axis=1) + ba)
    i = _sigmoid(jnp.concatenate(is_, axis=1) + bx)
    softplus = jnp.maximum(-lam, 0.0) + jnp.log1p(jnp.exp(-jnp.abs(lam)))
    log_a = -C_GATE_C * r * softplus
    a = jnp.exp(log_a)
    th = jnp.tanh(log_a)
    u = jnp.sqrt(-2.0 * th / (1.0 - th)) * (i * y)
    return a, u


def _rglru_prompt_kernel(xb_ref, gate_ref, x_ref, wout_ref, cw_ref, cb_ref, wax_ref, ba_ref, bx_ref, lam_ref,
                         y_ref, tail_ref, hl_ref, prev_scr, h_scr, a_scr, u_scr, hs_scr):
    t = pl.program_id(1)
    tb = xb_ref.shape[0]

    @pl.when(t == 0)
    def _():
        prev_scr[...] = jnp.zeros(prev_scr.shape, F32)
        h_scr[...] = jnp.zeros(h_scr.shape, F32)

    xb = xb_ref[...]
    prev = prev_scr[...]
    row = lax.broadcasted_iota(jnp.int32, (tb, 1), 0)
    y = cb_ref[...] + cw_ref[C_CONV - 1:C_CONV, :] * xb
    for d in range(1, C_CONV):
        sh = pltpu.roll(xb, d, 0)
        for r in range(d):
            sh = jnp.where(row == r, prev[SUBLANES - d + r:SUBLANES - d + r + 1, :], sh)
        y = y + cw_ref[C_CONV - 1 - d:C_CONV - d, :] * sh
    prev_scr[...] = xb[tb - SUBLANES:, :]
    a, u = _rglru_gates(y, wax_ref, ba_ref[...], bx_ref[...], lam_ref[...])
    a_scr[...] = a
    u_scr[...] = u

    def body(i, h):
        base = pl.multiple_of(i * SUBLANES, SUBLANES)
        for r in range(SUBLANES):
            h = a_scr[pl.ds(base + r, 1), :] * h + u_scr[pl.ds(base + r, 1), :]
            hs_scr[pl.ds(base + r, 1), :] = h
        return h

    h = lax.fori_loop(0, tb // SUBLANES, body, h_scr[...])
    h_scr[...] = h
    gate = gate_ref[...]
    z = hs_scr[...] * (gate * _sigmoid(gate))
    y_ref[...] = x_ref[...] + _dot(z.astype(BF16), wout_ref[...])

    @pl.when(t == pl.num_programs(1) - 1)
    def _():
        tail_ref[...] = xb[tb - SUBLANES:, :]
        hl_ref[...] = h


def _rglru_weights(conv_w, conv_b, w_a, b_a, w_x, b_x, lam):
    r1 = lambda a: a.reshape(1, -1)
    return {"cw": conv_w, "cb": r1(conv_b), "wax": jnp.concatenate([w_a, w_x], axis=2).astype(BF16),
            "ba": r1(b_a), "bx": r1(b_x), "lam": r1(lam)}


def rglru_prompt(proj, x, w_out_bf16, wts, batch, seq, tblk):
    nt = seq // tblk
    w = C_WIDTH
    d = x.shape[1]
    full = lambda a: pl.BlockSpec(a.shape, lambda b, t: (0,) * a.ndim)
    rows = lambda n, col: pl.BlockSpec((tblk, n), lambda b, t: (b * nt + t, col))
    names = ("cw", "cb", "wax", "ba", "bx", "lam")
    return pl.pallas_call(
        _rglru_prompt_kernel,
        grid=(batch, nt),
        in_specs=[rows(w, 0), rows(w, 1), rows(d, 0), full(w_out_bf16)] + [full(wts[n]) for n in names],
        out_specs=[rows(d, 0),
                   pl.BlockSpec((None, SUBLANES, w), lambda b, t: (b, 0, 0)),
                   pl.BlockSpec((None, 1, w), lambda b, t: (b, 0, 0))],
        out_shape=[jax.ShapeDtypeStruct((batch * seq, d), F32),
                   jax.ShapeDtypeStruct((batch, SUBLANES, w), F32),
                   jax.ShapeDtypeStruct((batch, 1, w), F32)],
        scratch_shapes=[pltpu.VMEM((SUBLANES, w), F32), pltpu.VMEM((1, w), F32),
                        pltpu.VMEM((tblk, w), F32), pltpu.VMEM((tblk, w), F32), pltpu.VMEM((tblk, w), F32)],
        compiler_params=_cparams("parallel", "arbitrary"),
        name="rglru_prompt",
    )(proj, proj, x, w_out_bf16, *[wts[n] for n in names])


def _rglru_sample_kernel(xb_ref, buf_ref, h0_ref, cw_ref, cb_ref, wax_ref, ba_ref, bx_ref, lam_ref,
                         hs_ref, nbuf_ref):
    xb = xb_ref[...]
    y = cb_ref[...] + cw_ref[C_CONV - 1:C_CONV, :] * xb
    for j in range(C_CONV - 1):
        y = y + cw_ref[j:j + 1, :] * buf_ref[j]
    a, u = _rglru_gates(y, wax_ref, ba_ref[...], bx_ref[...], lam_ref[...])
    hs_ref[...] = a * h0_ref[...] + u
    for j in range(C_CONV - 2):
        nbuf_ref[j] = buf_ref[j + 1]
    nbuf_ref[C_CONV - 2] = xb


def rglru_sample(proj_s, buf_t, h0, wts):
    db = proj_s.shape[0]
    w = C_WIDTH
    whole = lambda a: pl.BlockSpec(a.shape, lambda i: (0,) * a.ndim)
    names = ("cw", "cb", "wax", "ba", "bx", "lam")
    return pl.pallas_call(
        _rglru_sample_kernel,
        grid=(1,),
        in_specs=[pl.BlockSpec((db, w), lambda i: (0, 0)), whole(buf_t), whole(h0)] + [whole(wts[n]) for n in names],
        out_specs=[pl.BlockSpec((db, w), lambda i: (0, 0)), whole(buf_t)],
        out_shape=[jax.ShapeDtypeStruct((db, w), F32), jax.ShapeDtypeStruct(buf_t.shape, F32)],
        compiler_params=_cparams("arbitrary"),
        name="rglru_sample",
    )(proj_s, buf_t, h0, *[wts[n] for n in names])


TM_PROMPT = 512
TQ_FLASH = 512
FLASH_LOOKAHEAD = 2
T_HGRN = 256
T_RGLRU = 512
PAGES_PER_BLOCK = 16
SEQS_HGRN_SAMPLE = 8

def kernel(x_prompt, x_sample, cache_ckv, cache_kpe, page_table, state_hgrn, state_conv, state_lru, norm_g, mla_w_in, mla_g_q_lora, mla_g_kv, mla_w_uq, mla_w_ukv, mla_g_q, mla_g_k, mla_w_out, hgrn_w_in, hgrn_lower_bounds, hgrn_g_norm, hgrn_w_out, rglru_w_in, rglru_conv_w, rglru_conv_b, rglru_w_a, rglru_b_a, rglru_w_x, rglru_b_x, rglru_L, rglru_w_out):
    bsz, s_p, d = x_prompt.shape
    db, s_s, _ = x_sample.shape
    assert s_s == 1 and d == D_MODEL
    depth = norm_g.shape[0]
    past_len = page_table.shape[1] * PAGE_SIZE
    xp = x_prompt.reshape(bsz * s_p, d)
    xs = x_sample.reshape(db, d)
    tabs_p = _rope_tables(jnp.arange(s_p))
    tabs_s = _rope_tables(jnp.full((db,), past_len))
    cache_kpe_t = cache_kpe.transpose(0, 1, 3, 2)

    ckv_p, kpe_p, ckv_s, kpe_s = [], [], [], []
    hg_p, hg_s, cv_p, cv_s, lr_p, lr_s = [], [], [], [], [], []
    for li in range(depth):
        kind, j = li % N_MIXERS, li // N_MIXERS
        if kind == 0:
            wts = _mla_weights(mla_w_in[j], mla_g_q_lora[j], mla_g_kv[j], mla_w_uq[j], mla_w_ukv[j],
                               mla_g_q[j], mla_g_k[j])
            w_out = mla_w_out[j].astype(BF16)
            q, k, vt, gate, ckv, kpe_blk = mla_proj(xp, norm_g[li], wts, tabs_p, s_p, TM_PROMPT, BF16, "g_q_base2")
            xp = flash_prompt(q, k, vt, gate, xp, w_out, bsz, s_p, TQ_FLASH)
            ckv_p.append(ckv.reshape(bsz, s_p, A_KV_LORA))
            kpe_p.append(kpe_blk[:, A_NOPE:A_QK].reshape(bsz, s_p, A_ROPE))

            q_s, _, _, gate_s, ckv_n, kpe_blk_s = mla_proj(xs, norm_g[li], wts, tabs_s, db, db, F32, "g_q")
            kpe_n = kpe_blk_s[:, A_NOPE:A_QK]
            o_s = mla_sample_attention(q_s, ckv_n, kpe_n, cache_ckv, cache_kpe_t, j, page_table, wts,
                                       PAGES_PER_BLOCK)
            xs = gated_out(o_s, gate_s, 0, w_out, xs, db)
            ckv_s.append(ckv_n.reshape(db, 1, A_KV_LORA))
            kpe_s.append(kpe_n.reshape(db, 1, A_ROPE))
        elif kind == 1:
            w_in = hgrn_w_in[j].astype(BF16)
            w_out = hgrn_w_out[j].astype(BF16)
            proj = norm_matmul(xp, norm_g[li], w_in, TM_PROMPT // 2)
            xp, st = hgrn_prompt(proj, xp, w_out, hgrn_g_norm[j], hgrn_lower_bounds, li, bsz, s_p, T_HGRN)
            hg_p.append(st)

            proj_s = norm_matmul(xs, norm_g[li], w_in, db)
            o_s, st_s = hgrn_sample(proj_s, hgrn_lower_bounds, li, state_hgrn[j])
            xs = gated_out(o_s, proj_s, 3, w_out, xs, db, hgrn_g_norm[j])
            hg_s.append(st_s)
        else:
            w_in = rglru_w_in[j].astype(BF16)
            w_out = rglru_w_out[j].astype(BF16)
            wts = _rglru_weights(rglru_conv_w[j], rglru_conv_b[j], rglru_w_a[j], rglru_b_a[j],
                                 rglru_w_x[j], rglru_b_x[j], rglru_L[j])
            proj = norm_matmul(xp, norm_g[li], w_in, TM_PROMPT)
            xp, tail, hl = rglru_prompt(proj, xp, w_out, wts, bsz, s_p, T_RGLRU)
            cv_p.append(tail[:, SUBLANES - (C_CONV - 1):, :])
            lr_p.append(hl.reshape(bsz, C_WIDTH))

            proj_s = norm_matmul(xs, norm_g[li], w_in, db)
            hs_s, nbuf = rglru_sample(proj_s, state_conv[j].transpose(1, 0, 2), state_lru[j], wts)
            xs = gated_out(hs_s, proj_s, 1, w_out, xs, db)
            cv_s.append(nbuf.transpose(1, 0, 2))
            lr_s.append(hs_s)

    return (xp.reshape(bsz, s_p, d), xs.reshape(db, 1, d),
            jnp.stack(ckv_p), jnp.stack(kpe_p), jnp.stack(ckv_s), jnp.stack(kpe_s),
            jnp.stack(hg_p), jnp.stack(hg_s), jnp.stack(cv_p), jnp.stack(cv_s),
            jnp.stack(lr_p), jnp.stack(lr_s))
```
